```python
import jax, jax.numpy as jnp
from jax import lax
import numpy as np

D_MODEL = 2048
BATCH = 4
SEQ = 2048
DEPTH = 2
DEC_BATCH = 128
DEC_SEQ = 4
PAST_LEN = 16384
PAGE_SIZE = 128

CONV_WIDTH = D_MODEL // 2
CONV_K = 3
POOL_WIDTH = D_MODEL // 2
POOL_WINDOWS = (2, 4, 8, 16)
N_POOL_GROUPS = 4
POOL_GROUP = POOL_WIDTH // N_POOL_GROUPS
POOL_STATE = 15
CHUNK = 128
GMLP_WIDTH = D_MODEL // 2
GMLP_HEAD = 128
GMLP_HEADS = GMLP_WIDTH // GMLP_HEAD
N_BRANCH = 3
IN_COLS = 3 * CONV_WIDTH + POOL_WIDTH + 2 * GMLP_WIDTH + N_BRANCH * D_MODEL
N_GROUPS_MOE = 4
EXP_PER_GROUP = 4
N_EXPERTS = N_GROUPS_MOE * EXP_PER_GROUP
TOP_K = 2
D_FF_EXPERT = D_MODEL // 4
EPS = 1e-6

kernel_name = 'hybrid_conv_pool_gmlp_hmoe_step'


def rmsnorm(x, g):
    xf = x.astype(jnp.float32)
    r = lax.rsqrt(jnp.mean(xf * xf, axis=-1, keepdims=True) + EPS)
    return (xf * r).astype(x.dtype) * g


def conv_mixer(bg, cg, hc, conv_w, prev):
    z = cg * hc
    zz = jnp.concatenate([prev.astype(z.dtype), z], axis=1)
    L = z.shape[1]
    y = sum(conv_w[k] * zz[:, k:k + L] for k in range(CONV_K))
    return bg * y, zz[:, L:]


def pool_mixer(p, prev, pos0, pool_w, pool_scale):
    B, L, _ = p.shape
    pp = jnp.concatenate([prev.astype(p.dtype), p], axis=1)
    cs = jnp.pad(jnp.cumsum(pp.astype(jnp.float32), axis=1), ((0, 0), (1, 0), (0, 0)))
    end = cs[:, POOL_STATE + 1:POOL_STATE + 1 + L]
    t = jnp.arange(L)
    means = []
    for g, w in enumerate(POOL_WINDOWS):
        sl = slice(g * POOL_GROUP, (g + 1) * POOL_GROUP)
        start = cs[:, POOL_STATE + 1 - w:POOL_STATE + 1 - w + L, sl]
        cnt = jnp.minimum(pos0 + t + 1, w).astype(jnp.float32)[None, :, None]
        means.append((end[..., sl] - start) / cnt)
    mean = jnp.concatenate(means, axis=-1).astype(p.dtype)
    d = (mean - p).reshape(B, L, N_POOL_GROUPS, POOL_GROUP)
    y = jnp.einsum('blgi,gio->blgo', d, pool_w).reshape(B, L, POOL_WIDTH) * pool_scale
    return y, pp[:, L:]


def gmlp_mixer(u, v, g_v, w_s, b_s):
    B, L, _ = v.shape
    vn = rmsnorm(v, g_v)
    lc = min(L, CHUNK)
    nc = L // lc
    mask = jnp.tril(jnp.ones((lc, lc), dtype=bool))
    ws = jnp.where(mask[None], w_s[:, :lc, :lc], 0)
    vr = vn.reshape(B, nc, lc, GMLP_HEADS, GMLP_HEAD)
    s = jnp.einsum('hts,bnshc->bnthc', ws, vr) + b_s[:, :lc].T[None, None, :, :, None]
    return u * s.reshape(B, L, GMLP_WIDTH), vn


def hier_moe(h, w_rg, b_rg, w_re, b_re, w_gate, w_up, w_down):
    B, L, D = h.shape
    t = h.reshape(-1, D)
    n = t.shape[0]
    grp_logits = (t @ w_rg + b_rg).astype(jnp.float32)
    grp_prob = jax.nn.softmax(grp_logits, axis=-1)
    _, gsel = lax.top_k(grp_logits, 1)
    gp = jnp.take_along_axis(grp_prob, gsel, axis=1)
    exp_logits = (t @ w_re + b_re).astype(jnp.float32).reshape(n, N_GROUPS_MOE, EXP_PER_GROUP)
    idx = jnp.broadcast_to(gsel[:, :, None], (n, 1, EXP_PER_GROUP))
    sel_logits = jnp.take_along_axis(exp_logits, idx, axis=1)[:, 0]
    top_v, top_i = lax.top_k(sel_logits, TOP_K)
    top_w = jax.nn.softmax(top_v, axis=-1) * gp
    expert_id = gsel * EXP_PER_GROUP + top_i
    combine = jnp.sum(jax.nn.one_hot(expert_id, N_EXPERTS, dtype=jnp.float32) * top_w[..., None], axis=1)
    a = jnp.einsum('nd,edf->enf', t, w_gate)
    b = jnp.einsum('nd,edf->enf', t, w_up)
    hid = jax.nn.silu(a) * b * combine.T.astype(t.dtype)[:, :, None]
    y = jnp.einsum('enf,efd->nd', hid, w_down)
    return y.reshape(B, L, D)


def decoder_layer(x, c, prev_conv, prev_pool, pos0, w_ada, b_ada, g_norm1, w_in, conv_w, pool_w, pool_scale,
                  g_v, w_s, b_s, w_br, w_o, g_norm2, w_rg, b_rg, w_re, b_re, w_gate, w_up, w_down):
    mod = (c @ w_ada + b_ada)[:, None, :]
    sh1, sc1, gt1, sh2, sc2, gt2 = jnp.split(mod, 6, axis=-1)
    h = rmsnorm(x, g_norm1) * (1 + sc1) + sh1
    proj = h @ w_in
    cuts = [CONV_WIDTH, 2 * CONV_WIDTH, 3 * CONV_WIDTH, 3 * CONV_WIDTH + POOL_WIDTH,
            3 * CONV_WIDTH + POOL_WIDTH + GMLP_WIDTH, 3 * CONV_WIDTH + POOL_WIDTH + 2 * GMLP_WIDTH]
    bg, cg, hc, p, u, v, gates = jnp.split(proj, cuts, axis=-1)
    y_a, conv_state = conv_mixer(bg, cg, hc, conv_w, prev_conv)
    y_b, pool_state = pool_mixer(p, prev_pool, pos0, pool_w, pool_scale)
    y_c, v_rows = gmlp_mixer(u, v, g_v, w_s, b_s)
    branches = jnp.einsum('blnw,nwd->blnd', jnp.stack([y_a, y_b, y_c], axis=2), w_br)
    gate = jax.nn.sigmoid(gates.reshape(gates.shape[0], gates.shape[1], N_BRANCH, D_MODEL))
    merged = jnp.sum(gate * branches, axis=2)
    x = x + gt1 * (merged @ w_o)
    h2 = rmsnorm(x, g_norm2) * (1 + sc2) + sh2
    x = x + gt2 * hier_moe(h2, w_rg, b_rg, w_re, b_re, w_gate, w_up, w_down)
    return x, conv_state, pool_state, v_rows


def run_trunk(x, c, state_conv, state_pool, pos0, weights, g_final):
    convs, pools, vs = [], [], []
    for l in range(DEPTH):
        lw = [w[l] for w in weights]
        x, cst, pst, vr = decoder_layer(x, c, state_conv[l], state_pool[l], pos0, *lw)
        convs.append(cst)
        pools.append(pst)
        vs.append(vr)
    return rmsnorm(x, g_final), jnp.stack(convs), jnp.stack(pools), jnp.stack(vs)


def setup_inputs(seed: int = 0) -> dict:
    key = jax.random.key(seed)
    ks = jax.random.split(key, 32)
    f32 = jnp.float32

    def nrm(k, shape, scale=1.0):
        return jax.random.normal(k, shape, f32) * scale

    D = D_MODEL
    return {
        'x_prompt': nrm(ks[0], (BATCH, SEQ, D)),
        'x_sample': nrm(ks[1], (DEC_BATCH, DEC_SEQ, D)),
        'c_prompt': nrm(ks[2], (BATCH, D)),
        'c_sample': nrm(ks[3], (DEC_BATCH, D)),
        'state_conv': nrm(ks[4], (DEPTH, DEC_BATCH, CONV_K - 1, CONV_WIDTH)),
        'state_pool': nrm(ks[5], (DEPTH, DEC_BATCH, POOL_STATE, POOL_WIDTH)),
        'w_ada': nrm(ks[6], (DEPTH, D, 6 * D), 0.2 * D ** -0.5),
        'b_ada': nrm(ks[7], (DEPTH, 6 * D), 0.01),
        'g_norm1': 1.0 + nrm(ks[8], (DEPTH, D), 0.05),
        'w_in': nrm(ks[9], (DEPTH, D, IN_COLS), D ** -0.5),
        'conv_w': nrm(ks[10], (DEPTH, CONV_K, CONV_WIDTH), CONV_K ** -0.5),
        'pool_w': nrm(ks[11], (DEPTH, N_POOL_GROUPS, POOL_GROUP, POOL_GROUP), POOL_GROUP ** -0.5),
        'pool_scale': 1.0 + nrm(ks[12], (DEPTH, POOL_WIDTH), 0.1),
        'g_v': 1.0 + nrm(ks[13], (DEPTH, GMLP_WIDTH), 0.05),
        'w_s': nrm(ks[14], (DEPTH, GMLP_HEADS, CHUNK, CHUNK), CHUNK ** -0.5),
        'b_s': 1.0 + nrm(ks[15], (DEPTH, GMLP_HEADS, CHUNK), 0.1),
        'w_br': nrm(ks[16], (DEPTH, N_BRANCH, CONV_WIDTH, D), CONV_WIDTH ** -0.5),
        'w_o': nrm(ks[17], (DEPTH, D, D), D ** -0.5),
        'g_norm2': 1.0 + nrm(ks[18], (DEPTH, D), 0.05),
        'w_rg': nrm(ks[19], (DEPTH, D, N_GROUPS_MOE), D ** -0.5),
        'b_rg': nrm(ks[20], (DEPTH, N_GROUPS_MOE), 0.01),
        'w_re': nrm(ks[21], (DEPTH, D, N_EXPERTS), D ** -0.5),
        'b_re': nrm(ks[22], (DEPTH, N_EXPERTS), 0.01),
        'w_gate': nrm(ks[23], (DEPTH, N_EXPERTS, D, D_FF_EXPERT), D ** -0.5),
        'w_up': nrm(ks[24], (DEPTH, N_EXPERTS, D, D_FF_EXPERT), D ** -0.5),
        'w_down': nrm(ks[25], (DEPTH, N_EXPERTS, D_FF_EXPERT, D), D_FF_EXPERT ** -0.5),
        'g_final': 1.0 + nrm(ks[26], (D,), 0.05),
    }


def reference(x_prompt, x_sample, c_prompt, c_sample, state_conv, state_pool, w_ada, b_ada, g_norm1, w_in,
              conv_w, pool_w, pool_scale, g_v, w_s, b_s, w_br, w_o, g_norm2, w_rg, b_rg, w_re, b_re,
              w_gate, w_up, w_down, g_final):
    weights = (w_ada, b_ada, g_norm1, w_in, conv_w, pool_w, pool_scale, g_v, w_s, b_s, w_br, w_o, g_norm2,
               w_rg, b_rg, w_re, b_re, w_gate, w_up, w_down)
    zero_conv = jnp.zeros((DEPTH, x_prompt.shape[0], CONV_K - 1, CONV_WIDTH), x_prompt.dtype)
    zero_pool = jnp.zeros((DEPTH, x_prompt.shape[0], POOL_STATE, POOL_WIDTH), x_prompt.dtype)
    y_prompt, conv_p, pool_p, _ = run_trunk(x_prompt, c_prompt, zero_conv, zero_pool, 0, weights, g_final)
    y_sample, conv_s, pool_s, v_s = run_trunk(x_sample, c_sample, state_conv, state_pool, PAST_LEN, weights, g_final)
    return (y_prompt, y_sample, conv_p, pool_p, conv_s, pool_s, v_s)
```

```python
import functools

import jax
import jax.numpy as jnp
from jax import lax
from jax.experimental import pallas as pl
from jax.experimental.pallas import tpu as pltpu

F32 = jnp.float32
BF16 = jnp.bfloat16

D = 2048
BATCH = 4
SEQ = 2048
DEPTH = 2
DEC_BATCH = 128
DEC_SEQ = 4
W = D // 2
CONV_K = 3
POOL_STATE = 15
N_POOL_GROUPS = 4
POOL_GROUP = W // N_POOL_GROUPS
CHUNK = 128
HEADS = W // CHUNK
IN_COLS = 6 * W + 3 * D
N_GROUPS_MOE = 4
EXP_PER_GROUP = 4
N_EXPERTS = 16
TOP_K = 2
D_FF = D // 4
EPS = 1e-6

NP = BATCH * SEQ
NS = DEC_BATCH * DEC_SEQ
NT = NP + NS
TM = 512
N_TILES = NT // TM
CB = 256
TE = 256
MAX_TILES = (NT * TOP_K) // TE + N_EXPERTS
P_ROWS = MAX_TILES * TE
TK = 256
TW = 256
LANES = 128
MOD_ROWS = 144
MOD_PROMPT_BLOCK = DEC_BATCH // 8

VMEM_LIMIT_V7X = 56 * 1024 * 1024


def _dot(a, b):
    return jnp.dot(a, b, preferred_element_type=F32)


def _params(sem, vmem=VMEM_LIMIT_V7X):
    return pltpu.CompilerParams(dimension_semantics=sem, vmem_limit_bytes=vmem)


def _sigmoid(x):
    return 1.0 / (1.0 + jnp.exp(-x))


def _rms(x):
    return lax.rsqrt(jnp.mean(x * x, axis=-1, keepdims=True) + EPS)


def _ada_kernel(c_ref, w_ref, b_ref, o_ref):
    o_ref[...] = _dot(c_ref[...], w_ref[...].astype(BF16)) + b_ref[...]


def _ada(c_all, w_ada, b_ada):
    nb = 512
    return pl.pallas_call(
        _ada_kernel,
        out_shape=jax.ShapeDtypeStruct((DEPTH, MOD_ROWS, 6 * D), F32),
        grid=(DEPTH, 6 * D // nb),
        in_specs=[
            pl.BlockSpec((MOD_ROWS, D), lambda l, j: (0, 0)),
            pl.BlockSpec((None, D, nb), lambda l, j: (l, 0, j)),
            pl.BlockSpec((None, 1, nb), lambda l, j: (l, 0, j)),
        ],
        out_specs=pl.BlockSpec((None, MOD_ROWS, nb), lambda l, j: (l, 0, j)),
        compiler_params=_params(("arbitrary", "arbitrary")),
        name="ada",
    )(c_all, w_ada, b_ada.reshape(DEPTH, 1, 6 * D))


def _mod_specs(l, q, tile_rows):
    del tile_rows
    return [
        pl.BlockSpec((None, 8, D), lambda i, *_: (l, MOD_PROMPT_BLOCK, q)),
        pl.BlockSpec((None, DEC_BATCH, D), lambda i, *_: (l, 0, q)),
    ]


def _mod_rows(i, tile_rows, mp_ref, ms_ref):
    tiles_per_seq = SEQ // tile_rows
    is_sample = i >= NP // tile_rows
    b = jnp.minimum(i // tiles_per_seq, BATCH - 1)
    mp = mp_ref[pl.ds(b, 1), :]
    ms = jnp.concatenate([ms_ref[...]] * (tile_rows // DEC_BATCH), axis=0)
    return jnp.where(is_sample, ms, mp)


def _norm_kernel(x_ref, g_ref, scp, scs, shp, shs, h_ref):
    i = pl.program_id(0)
    x = x_ref[...]
    sc = _mod_rows(i, TM, scp, scs)
    sh = _mod_rows(i, TM, shp, shs)
    h_ref[...] = (((x * _rms(x)) * g_ref[...]) * (1.0 + sc) + sh).astype(BF16)


def _norm(x, g, mod, l):
    return pl.pallas_call(
        _norm_kernel,
        out_shape=jax.ShapeDtypeStruct((NT, D), BF16),
        grid=(N_TILES,),
        in_specs=[pl.BlockSpec((TM, D), lambda i: (i, 0)),
                  pl.BlockSpec((None, 1, D), lambda i: (l, 0, 0))]
                 + _mod_specs(l, 1, TM) + _mod_specs(l, 0, TM),
        out_specs=pl.BlockSpec((TM, D), lambda i: (i, 0)),
        compiler_params=_params(("arbitrary",)),
        name="norm1",
    )(x, g, mod, mod, mod, mod)


def _shift_rows(a, s, row):
    return jnp.where(row >= s, pltpu.roll(a, s, 0), 0.0)


def _conv_p_kernel(h_ref, wb_ref, wc_ref, wh_ref, cw_ref, y_ref, st_ref):
    h = h_ref[...]
    z = _dot(h, wc_ref[...]) * _dot(h, wh_ref[...])
    row = lax.broadcasted_iota(jnp.int32, (SEQ, 1), 0)
    cw = cw_ref[...]
    y = cw[0:1] * _shift_rows(z, 2, row) + cw[1:2] * _shift_rows(z, 1, row) + cw[2:3] * z
    y_ref[...] = (_dot(h, wb_ref[...]) * y).astype(BF16)
    st_ref[...] = z[SEQ - 8:, :]


def _conv_p(h, w_in, conv_w, l):
    nc = W // CB
    return pl.pallas_call(
        _conv_p_kernel,
        out_shape=(jax.ShapeDtypeStruct((NP, W), BF16),
                   jax.ShapeDtypeStruct((BATCH, 8, W), F32)),
        grid=(BATCH, nc),
        in_specs=[
            pl.BlockSpec((SEQ, D), lambda b, c: (b, 0)),
            pl.BlockSpec((None, D, CB), lambda b, c: (l, 0, c)),
            pl.BlockSpec((None, D, CB), lambda b, c: (l, 0, nc + c)),
            pl.BlockSpec((None, D, CB), lambda b, c: (l, 0, 2 * nc + c)),
            pl.BlockSpec((None, CONV_K, CB), lambda b, c: (l, 0, c)),
        ],
        out_specs=(pl.BlockSpec((SEQ, CB), lambda b, c: (b, c)),
                   pl.BlockSpec((None, 8, CB), lambda b, c: (b, 0, c))),
        compiler_params=_params(("arbitrary", "arbitrary")),
        name="conv_prompt",
    )(h, w_in, w_in, w_in, conv_w)


def _pool_p_kernel(h_ref, wp_ref, pw_ref, ps_ref, y_ref, st_ref):
    g = pl.program_id(1)
    p = _dot(h_ref[...], wp_ref[...])
    row = lax.broadcasted_iota(jnp.int32, (SEQ, 1), 0)
    s = p + _shift_rows(p, 1, row)
    s = jnp.where(g >= 1, s + _shift_rows(s, 2, row), s)
    s = jnp.where(g >= 2, s + _shift_rows(s, 4, row), s)
    s = jnp.where(g >= 3, s + _shift_rows(s, 8, row), s)
    window = jnp.left_shift(2, g)
    cnt = jnp.minimum(row + 1, window).astype(F32)
    d = s / cnt - p
    y_ref[...] = (_dot(d.astype(BF16), pw_ref[...]) * ps_ref[...]).astype(BF16)
    st_ref[...] = p[SEQ - 16:, :]


def _pool_p(h, w_in, pool_w, pool_scale, l):
    col0 = 3 * W // CB
    return pl.pallas_call(
        _pool_p_kernel,
        out_shape=(jax.ShapeDtypeStruct((NP, W), BF16),
                   jax.ShapeDtypeStruct((BATCH, 16, W), F32)),
        grid=(BATCH, N_POOL_GROUPS),
        in_specs=[
            pl.BlockSpec((SEQ, D), lambda b, g: (b, 0)),
            pl.BlockSpec((None, D, CB), lambda b, g: (l, 0, col0 + g)),
            pl.BlockSpec((None, None, POOL_GROUP, POOL_GROUP), lambda b, g: (l, g, 0, 0)),
            pl.BlockSpec((None, 1, CB), lambda b, g: (l, 0, g)),
        ],
        out_specs=(pl.BlockSpec((SEQ, CB), lambda b, g: (b, g)),
                   pl.BlockSpec((None, 16, CB), lambda b, g: (b, 0, g))),
        compiler_params=_params(("arbitrary", "arbitrary")),
        name="pool_prompt",
    )(h, w_in, pool_w, pool_scale)


def _gmlp_p_kernel(h_ref, wv_ref, wu_ref, gv_ref, ws_ref, bs_ref, y_ref, vn_s):
    j = pl.program_id(1)

    @pl.when(j == 0)
    def _():
        for r0 in range(0, SEQ, TM):
            v = _dot(h_ref[r0:r0 + TM, :], wv_ref[...])
            vn = (v * _rms(v)) * gv_ref[...]
            for hd in range(HEADS):
                vn_s[hd, r0:r0 + TM, :] = vn[:, hd * CHUNK:(hd + 1) * CHUNK].astype(BF16)

    u = _dot(h_ref[...], wu_ref[...])
    heads_per_block = CB // CHUNK
    for hh in range(heads_per_block):
        head = heads_per_block * j + hh
        wsh = ws_ref[head]
        bsh = bs_ref[head]
        for n in range(SEQ // CHUNK):
            rows = slice(n * CHUNK, (n + 1) * CHUNK)
            cols = slice(hh * CHUNK, (hh + 1) * CHUNK)
            s = _dot(wsh, vn_s[head, pl.ds(n * CHUNK, CHUNK), :]) + bsh
            y_ref[rows, cols] = (u[rows, cols] * s).astype(BF16)


def _gmlp_p(h, w_in, g_v, ws_tril, bs_tile, l):
    ucol0 = 4 * W // CB
    return pl.pallas_call(
        _gmlp_p_kernel,
        out_shape=jax.ShapeDtypeStruct((NP, W), BF16),
        grid=(BATCH, W // CB),
        in_specs=[
            pl.BlockSpec((SEQ, D), lambda b, j: (b, 0)),
            pl.BlockSpec((None, D, W), lambda b, j: (l, 0, 5)),
            pl.BlockSpec((None, D, CB), lambda b, j: (l, 0, ucol0 + j)),
            pl.BlockSpec((None, 1, W), lambda b, j: (l, 0, 0)),
            pl.BlockSpec((None, HEADS, CHUNK, CHUNK), lambda b, j: (l, 0, 0, 0)),
            pl.BlockSpec((None, HEADS, CHUNK, CHUNK), lambda b, j: (l, 0, 0, 0)),
        ],
        out_specs=pl.BlockSpec((SEQ, CB), lambda b, j: (b, j)),
        scratch_shapes=[pltpu.VMEM((HEADS, SEQ, CHUNK), BF16)],
        compiler_params=_params(("arbitrary", "arbitrary")),
        name="gmlp_prompt",
    )(h, w_in, w_in, g_v, ws_tril, bs_tile)


def _mix_s_kernel(h_ref, wb_ref, wc_ref, wh_ref, wp_ref, cw_ref, pc_ref, pp_ref,
                  pw_ref, ps_ref, ya_ref, yb_ref, cst_ref, pst_ref):
    g = pl.program_id(0)
    h = h_ref[...]
    nb = DEC_BATCH
    z = _dot(h, wc_ref[...]) * _dot(h, wh_ref[...])
    bg = _dot(h, wb_ref[...])
    zs = [pc_ref[0], pc_ref[1]] + [z[t * nb:(t + 1) * nb] for t in range(DEC_SEQ)]
    cw = cw_ref[...]
    for t in range(DEC_SEQ):
        y = cw[0:1] * zs[t] + cw[1:2] * zs[t + 1] + cw[2:3] * zs[t + 2]
        ya_ref[t * nb:(t + 1) * nb, :] = (bg[t * nb:(t + 1) * nb] * y).astype(BF16)
    cst_ref[0] = zs[DEC_SEQ]
    cst_ref[1] = zs[DEC_SEQ + 1]
    p = _dot(h, wp_ref[...])
    pp = [pp_ref[k] for k in range(POOL_STATE)] + [p[t * nb:(t + 1) * nb] for t in range(DEC_SEQ)]
    window = jnp.left_shift(2, g).astype(F32)
    ds = []
    for t in range(DEC_SEQ):
        e = POOL_STATE + t
        s = pp[e] + pp[e - 1]
        s4 = s + (pp[e - 2] + pp[e - 3])
        s8 = s4 + ((pp[e - 4] + pp[e - 5]) + (pp[e - 6] + pp[e - 7]))
        s16 = s8 + (((pp[e - 8] + pp[e - 9]) + (pp[e - 10] + pp[e - 11]))
                    + ((pp[e - 12] + pp[e - 13]) + (pp[e - 14] + pp[e - 15])))
        s = jnp.where(g >= 1, s4, s)
        s = jnp.where(g >= 2, s8, s)
        s = jnp.where(g >= 3, s16, s)
        ds.append(s / window - pp[e])
    d = jnp.concatenate(ds, axis=0)
    yb_ref[...] = (_dot(d.astype(BF16), pw_ref[...]) * ps_ref[...]).astype(BF16)
    for k in range(POOL_STATE):
        pst_ref[k] = pp[DEC_SEQ + k]


def _mix_s(h, w_in, conv_w, pool_w, pool_scale, conv_t, pool_t, l):
    nc = W // CB
    srow = NP // NS
    return pl.pallas_call(
        _mix_s_kernel,
        out_shape=(jax.ShapeDtypeStruct((NS, W), BF16),
                   jax.ShapeDtypeStruct((NS, W), BF16),
                   jax.ShapeDtypeStruct((CONV_K - 1, DEC_BATCH, W), F32),
                   jax.ShapeDtypeStruct((POOL_STATE, DEC_BATCH, W), F32)),
        grid=(nc,),
        in_specs=[
            pl.BlockSpec((NS, D), lambda c: (srow, 0)),
            pl.BlockSpec((None, D, CB), lambda c: (l, 0, c)),
            pl.BlockSpec((None, D, CB), lambda c: (l, 0, nc + c)),
            pl.BlockSpec((None, D, CB), lambda c: (l, 0, 2 * nc + c)),
            pl.BlockSpec((None, D, CB), lambda c: (l, 0, 3 * nc + c)),
            pl.BlockSpec((None, CONV_K, CB), lambda c: (l, 0, c)),
            pl.BlockSpec((None, CONV_K - 1, DEC_BATCH, CB), lambda c: (l, 0, 0, c)),
            pl.BlockSpec((None, POOL_STATE, DEC_BATCH, CB), lambda c: (l, 0, 0, c)),
            pl.BlockSpec((None, None, POOL_GROUP, POOL_GROUP), lambda c: (l, c, 0, 0)),
            pl.BlockSpec((None, 1, CB), lambda c: (l, 0, c)),
        ],
        out_specs=(pl.BlockSpec((NS, CB), lambda c: (0, c)),
                   pl.BlockSpec((NS, CB), lambda c: (0, c)),
                   pl.BlockSpec((CONV_K - 1, DEC_BATCH, CB), lambda c: (0, 0, c)),
                   pl.BlockSpec((POOL_STATE, DEC_BATCH, CB), lambda c: (0, 0, c))),
        compiler_params=_params(("arbitrary",)),
        name="mix_sample",
    )(h, w_in, w_in, w_in, w_in, conv_w, conv_t, pool_t, pool_w, pool_scale)


def _gmlp_s_kernel(h_ref, wu_ref, wv_ref, gv_ref, wvec_ref, bvec_ref, yc_ref, vn_ref):
    h = h_ref[...]
    nb = DEC_BATCH
    v = _dot(h, wv_ref[...])
    vn = (v * _rms(v)) * gv_ref[...]
    vn_ref[...] = vn
    u = _dot(h, wu_ref[...])
    for t in range(DEC_SEQ):
        s = bvec_ref[t:t + 1, :]
        for sp in range(t + 1):
            k = t * DEC_SEQ + sp
            s = s + wvec_ref[k:k + 1, :] * vn[sp * nb:(sp + 1) * nb]
        yc_ref[t * nb:(t + 1) * nb, :] = (u[t * nb:(t + 1) * nb] * s).astype(BF16)


def _gmlp_s(h, w_in, g_v, wvec, bvec, l):
    srow = NP // NS
    return pl.pallas_call(
        _gmlp_s_kernel,
        out_shape=(jax.ShapeDtypeStruct((NS, W), BF16),
                   jax.ShapeDtypeStruct((NS, W), F32)),
        grid=(1,),
        in_specs=[
            pl.BlockSpec((NS, D), lambda i: (srow, 0)),
            pl.BlockSpec((None, D, W), lambda i: (l, 0, 4)),
            pl.BlockSpec((None, D, W), lambda i: (l, 0, 5)),
            pl.BlockSpec((None, 1, W), lambda i: (l, 0, 0)),
            pl.BlockSpec((None, DEC_SEQ * DEC_SEQ, W), lambda i: (l, 0, 0)),
            pl.BlockSpec((None, DEC_SEQ, W), lambda i: (l, 0, 0)),
        ],
        out_specs=(pl.BlockSpec((NS, W), lambda i: (0, 0)),
                   pl.BlockSpec((NS, W), lambda i: (0, 0))),
        compiler_params=_params(("arbitrary",)),
        name="gmlp_sample",
    )(h, w_in, w_in, g_v, wvec, bvec)


def _merge_kernel(h_ref, yap, ybp, ycp, yas, ybs, ycs, wg0, wg1, wg2, wbr_ref, o_ref):
    is_sample = pl.program_id(0) >= NP // TM
    h = h_ref[...]
    acc = None
    for n, (yp, ys, wg) in enumerate(((yap, yas, wg0), (ybp, ybs, wg1), (ycp, ycs, wg2))):
        y = jnp.where(is_sample, ys[...], yp[...])
        term = _sigmoid(_dot(h, wg[...])) * _dot(y, wbr_ref[n])
        acc = term if acc is None else acc + term
    o_ref[...] = acc.astype(BF16)


def _merge(h, y_prompt, y_sample, w_in, w_br, l):
    db = 512
    g0 = 6 * W // db
    gstep = D // db
    return pl.pallas_call(
        _merge_kernel,
        out_shape=jax.ShapeDtypeStruct((NT, D), BF16),
        grid=(N_TILES, D // db),
        in_specs=[
            pl.BlockSpec((TM, D), lambda i, d: (i, 0)),
            pl.BlockSpec((TM, W), lambda i, d: (jnp.minimum(i, NP // TM - 1), 0)),
            pl.BlockSpec((TM, W), lambda i, d: (jnp.minimum(i, NP // TM - 1), 0)),
            pl.BlockSpec((TM, W), lambda i, d: (jnp.minimum(i, NP // TM - 1), 0)),
            pl.BlockSpec((NS, W), lambda i, d: (0, 0)),
            pl.BlockSpec((NS, W), lambda i, d: (0, 0)),
            pl.BlockSpec((NS, W), lambda i, d: (0, 0)),
            pl.BlockSpec((None, D, db), lambda i, d: (l, 0, g0 + d)),
            pl.BlockSpec((None, D, db), lambda i, d: (l, 0, g0 + gstep + d)),
            pl.BlockSpec((None, D, db), lambda i, d: (l, 0, g0 + 2 * gstep + d)),
            pl.BlockSpec((None, 3, W, db), lambda i, d: (l, 0, 0, d)),
        ],
        out_specs=pl.BlockSpec((TM, db), lambda i, d: (i, d)),
        compiler_params=_params(("arbitrary", "arbitrary")),
        name="merge",
    )(h, *y_prompt, *y_sample, w_in, w_in, w_in, w_br)


def _route(logits):
    lane = lax.broadcasted_iota(jnp.int32, logits.shape, 1).astype(F32)
    neg = -jnp.inf
    big = float(LANES)
    is_grp = lane < N_GROUPS_MOE
    gl = jnp.where(is_grp, logits, neg)
    gmax = jnp.max(gl, axis=-1, keepdims=True)
    gsel = jnp.min(jnp.where(gl == gmax, lane, big), axis=-1, keepdims=True)
    gp = 1.0 / jnp.sum(jnp.where(is_grp, jnp.exp(logits - gmax), 0.0), axis=-1, keepdims=True)
    lo = N_GROUPS_MOE + gsel * EXP_PER_GROUP
    in_grp = (lane >= lo) & (lane < lo + EXP_PER_GROUP)
    el = jnp.where(in_grp, logits, neg)
    m1 = jnp.max(el, axis=-1, keepdims=True)
    i1 = jnp.min(jnp.where(el == m1, lane, big), axis=-1, keepdims=True)
    el2 = jnp.where(lane == i1, neg, el)
    m2 = jnp.max(el2, axis=-1, keepdims=True)
    i2 = jnp.min(jnp.where(el2 == m2, lane, big), axis=-1, keepdims=True)
    e = jnp.exp(m2 - m1)
    w1 = gp / (1.0 + e)
    w2 = gp * (e / (1.0 + e))
    rw = jnp.where(lane == 0.0, w1, jnp.where(lane == 1.0, w2, 0.0))
    re = jnp.where(lane == 0.0, i1 - N_GROUPS_MOE, jnp.where(lane == 1.0, i2 - N_GROUPS_MOE, 0.0))
    return rw, re.astype(jnp.int32)


def _wo_kernel(m_ref, wo_ref, x_ref, g_ref, gtp, gts, scp, scs, shp, shs, wrh_ref, wrl_ref, br_ref,
               x1_ref, h2_ref, rw_ref, re_ref):
    i = pl.program_id(0)
    gt = _mod_rows(i, TW, gtp, gts)
    x1 = x_ref[...] + gt * _dot(m_ref[...], wo_ref[...])
    x1_ref[...] = x1
    sc = _mod_rows(i, TW, scp, scs)
    sh = _mod_rows(i, TW, shp, shs)
    h2 = ((x1 * _rms(x1)) * g_ref[...]) * (1.0 + sc) + sh
    h2_ref[...] = h2
    hi = h2.astype(BF16)
    lo = (h2 - hi.astype(F32)).astype(BF16)
    logits = (_dot(hi, wrh_ref[...]) + (_dot(hi, wrl_ref[...]) + _dot(lo, wrh_ref[...]))) + br_ref[...]
    rw, re = _route(logits)
    rw_ref[...] = rw
    re_ref[...] = re


def _wo(merged, w_o, x, g2, mod, wr_hi, wr_lo, b_r, l):
    row = lambda i: (i, 0)
    return pl.pallas_call(
        _wo_kernel,
        out_shape=(jax.ShapeDtypeStruct((NT, D), F32),
                   jax.ShapeDtypeStruct((NT, D), F32),
                   jax.ShapeDtypeStruct((NT, LANES), F32),
                   jax.ShapeDtypeStruct((NT, LANES), jnp.int32)),
        grid=(NT // TW,),
        in_specs=[
            pl.BlockSpec((TW, D), row),
            pl.BlockSpec((None, D, D), lambda i: (l, 0, 0)),
            pl.BlockSpec((TW, D), row),
            pl.BlockSpec((None, 1, D), lambda i: (l, 0, 0)),
        ] + _mod_specs(l, 2, TW) + _mod_specs(l, 4, TW) + _mod_specs(l, 3, TW) + [
            pl.BlockSpec((None, D, LANES), lambda i: (l, 0, 0)),
            pl.BlockSpec((None, D, LANES), lambda i: (l, 0, 0)),
            pl.BlockSpec((None, 1, LANES), lambda i: (l, 0, 0)),
        ],
        out_specs=(pl.BlockSpec((TW, D), row), pl.BlockSpec((TW, D), row),
                   pl.BlockSpec((TW, LANES), row), pl.BlockSpec((TW, LANES), row)),
        compiler_params=_params(("arbitrary",)),
        name="wo_router",
    )(merged, w_o, x, g2, mod, mod, mod, mod, mod, mod, wr_hi, wr_lo, b_r)


def _row_gather_copy(src_hbm, idx, dst, sem):
    return pltpu.make_async_copy(src_hbm.at[pl.ds(idx, 1)], dst, sem)


def _moe_kernel(te_ref, tf_ref, nu_ref, src_ref, h2_hbm, w_ref, wg_ref, wu_ref, wd_ref, o_ref,
                xbuf, sem, wg_s, wu_s, wd_s):
    del te_ref
    t = pl.program_id(0)
    n_used = nu_ref[0]

    def issue(tile, slot):
        base = tile * TE

        def body(r, carry):
            _row_gather_copy(h2_hbm, src_ref[base + r], xbuf.at[slot, pl.ds(r, 1)], sem.at[slot]).start()
            return carry

        lax.fori_loop(0, TE, body, 0, unroll=8)

    @pl.when(t == 0)
    def _():
        issue(0, 0)

    @pl.when(t + 1 < n_used)
    def _():
        issue(t + 1, (t + 1) % 2)

    @pl.when((t < n_used) & (tf_ref[t] == 1))
    def _():
        wg_s[...] = wg_ref[...].astype(BF16)
        wu_s[...] = wu_ref[...].astype(BF16)
        wd_s[...] = wd_ref[...].astype(BF16)

    @pl.when(t < n_used)
    def _():
        slot = t % 2
        pltpu.make_async_copy(h2_hbm.at[pl.ds(0, TE)], xbuf.at[slot], sem.at[slot]).wait()
        x = xbuf[slot].astype(BF16)
        a = _dot(x, wg_s[...])
        b = _dot(x, wu_s[...])
        hid = ((a * _sigmoid(a)) * b) * w_ref[...]
        o_ref[...] = _dot(hid.astype(BF16), wd_s[...])

    @pl.when(t >= n_used)
    def _():
        o_ref[...] = jnp.zeros_like(o_ref)


def _moe(h2, w_sorted, tile_expert, tile_first, n_used, src, w_gate, w_up, w_down, l):
    grid_spec = pltpu.PrefetchScalarGridSpec(
        num_scalar_prefetch=4,
        grid=(MAX_TILES,),
        in_specs=[
            pl.BlockSpec(memory_space=pl.ANY),
            pl.BlockSpec((TE, 1), lambda t, te, tf, nu, src: (t, 0)),
            pl.BlockSpec((None, None, D, D_FF), lambda t, te, tf, nu, src: (l, te[t], 0, 0)),
            pl.BlockSpec((None, None, D, D_FF), lambda t, te, tf, nu, src: (l, te[t], 0, 0)),
            pl.BlockSpec((None, None, D_FF, D), lambda t, te, tf, nu, src: (l, te[t], 0, 0)),
        ],
        out_specs=pl.BlockSpec((TE, D), lambda t, te, tf, nu, src: (t, 0)),
        scratch_shapes=[
            pltpu.VMEM((2, TE, D), F32),
            pltpu.SemaphoreType.DMA((2,)),
            pltpu.VMEM((D, D_FF), BF16),
            pltpu.VMEM((D, D_FF), BF16),
            pltpu.VMEM((D_FF, D), BF16),
        ],
    )
    return pl.pallas_call(
        _moe_kernel,
        out_shape=jax.ShapeDtypeStruct((P_ROWS, D), F32),
        grid_spec=grid_spec,
        compiler_params=_params(("arbitrary",)),
        name="moe_experts",
    )(tile_expert, tile_first, n_used, src, h2, w_sorted, w_gate, w_up, w_down)


def _dispatch(route_e, route_w):
    ef = route_e[:, :TOP_K].reshape(-1)
    wf = route_w[:, :TOP_K].reshape(-1)
    onehot = (ef[:, None] == jnp.arange(N_EXPERTS, dtype=jnp.int32)[None, :]).astype(jnp.int32)
    csum = jnp.cumsum(onehot, axis=0)
    rank = jnp.take_along_axis(csum, ef[:, None], axis=1)[:, 0] - 1
    counts = csum[-1]
    tiles = (counts + TE - 1) // TE
    tile_end = jnp.cumsum(tiles)
    tile_start = tile_end - tiles
    dest = (tile_start[ef] * TE + rank).astype(jnp.int32)
    tids = jnp.arange(MAX_TILES, dtype=jnp.int32)
    tile_expert = jnp.minimum(jnp.searchsorted(tile_end, tids, side="right"), N_EXPERTS - 1).astype(jnp.int32)
    tile_first = (tids == tile_start[tile_expert]).astype(jnp.int32)
    n_used = tile_end[-1:].astype(jnp.int32)
    token = jnp.arange(NT * TOP_K, dtype=jnp.int32) // TOP_K
    src = jnp.zeros((P_ROWS,), jnp.int32).at[dest].set(token)
    w_sorted = jnp.zeros((P_ROWS,), F32).at[dest].set(wf).reshape(P_ROWS, 1)
    return dest, src, w_sorted, tile_expert, tile_first, n_used


def _combine_kernel(final, dest_ref, ys_hbm, x_ref, g_ref, gtp, gts, *rest):
    if final:
        y_ref, ybuf, sem = rest
    else:
        scp, scs, shp, shs, x2_ref, h_ref, ybuf, sem = rest
    i = pl.program_id(0)
    nt = pl.num_programs(0)

    def issue(tile, slot):
        base = tile * (TK * TOP_K)

        def body(r, carry):
            for k in range(TOP_K):
                _row_gather_copy(ys_hbm, dest_ref[base + TOP_K * r + k],
                                 ybuf.at[slot, k, pl.ds(r, 1)], sem.at[slot]).start()
            return carry

        lax.fori_loop(0, TK, body, 0, unroll=8)

    @pl.when(i == 0)
    def _():
        issue(0, 0)

    @pl.when(i + 1 < nt)
    def _():
        issue(i + 1, (i + 1) % 2)

    slot = i % 2
    for k in range(TOP_K):
        pltpu.make_async_copy(ys_hbm.at[pl.ds(0, TK)], ybuf.at[slot, k], sem.at[slot]).wait()
    gt = _mod_rows(i, TK, gtp, gts)
    x2 = x_ref[...] + gt * (ybuf[slot, 0] + ybuf[slot, 1])
    xn = (x2 * _rms(x2)) * g_ref[...]
    if final:
        y_ref[...] = xn
    else:
        x2_ref[...] = x2
        sc = _mod_rows(i, TK, scp, scs)
        sh = _mod_rows(i, TK, shp, shs)
        h_ref[...] = (xn * (1.0 + sc) + sh).astype(BF16)


def _combine(dest, ys, x1, g, mod, l, final):
    row = lambda i, d: (i, 0)
    nmod = lambda q, ll: [
        pl.BlockSpec((None, 8, D), lambda i, d: (ll, MOD_PROMPT_BLOCK, q)),
        pl.BlockSpec((None, DEC_BATCH, D), lambda i, d: (ll, 0, q)),
    ]
    in_specs = [pl.BlockSpec(memory_space=pl.ANY), pl.BlockSpec((TK, D), row),
                pl.BlockSpec((1, D), lambda i, d: (0, 0))] + nmod(5, l)
    args = [ys, x1, g, mod, mod]
    if final:
        out_shape = jax.ShapeDtypeStruct((NT, D), F32)
        out_specs = pl.BlockSpec((TK, D), row)
    else:
        in_specs += nmod(1, l + 1) + nmod(0, l + 1)
        args += [mod, mod, mod, mod]
        out_shape = (jax.ShapeDtypeStruct((NT, D), F32), jax.ShapeDtypeStruct((NT, D), BF16))
        out_specs = (pl.BlockSpec((TK, D), row), pl.BlockSpec((TK, D), row))
    grid_spec = pltpu.PrefetchScalarGridSpec(
        num_scalar_prefetch=1,
        grid=(NT // TK,),
        in_specs=in_specs,
        out_specs=out_specs,
        scratch_shapes=[pltpu.VMEM((2, TOP_K, TK, D), F32), pltpu.SemaphoreType.DMA((2,))],
    )
    return pl.pallas_call(
        functools.partial(_combine_kernel, final),
        out_shape=out_shape,
        grid_spec=grid_spec,
        compiler_params=_params(("arbitrary",)),
        name="combine_final" if final else "combine",
    )(dest, *args)


def kernel(x_prompt, x_sample, c_prompt, c_sample, state_conv, state_pool, w_ada, b_ada, g_norm1, w_in,
           conv_w, pool_w, pool_scale, g_v, w_s, b_s, w_br, w_o, g_norm2, w_rg, b_rg, w_re, b_re,
           w_gate, w_up, w_down, g_final):
    x = jnp.concatenate([x_prompt.reshape(NP, D),
                         x_sample.transpose(1, 0, 2).reshape(NS, D)], axis=0)
    c_all = jnp.concatenate([c_sample, c_prompt,
                             jnp.zeros((MOD_ROWS - DEC_BATCH - BATCH, D), F32)], axis=0).astype(BF16)
    mod = _ada(c_all, w_ada, b_ada)

    w_in_b = w_in.astype(BF16)
    w_br_b = w_br.astype(BF16)
    w_o_b = w_o.astype(BF16)
    pool_w_b = pool_w.astype(BF16)
    tril = jnp.tril(jnp.ones((CHUNK, CHUNK), dtype=bool))
    ws_tril = jnp.where(tril[None, None], w_s, 0.0).astype(BF16)
    bs_tile = jnp.broadcast_to(b_s[:, :, :, None], (DEPTH, HEADS, CHUNK, CHUNK))
    small = jnp.tril(jnp.ones((DEC_SEQ, DEC_SEQ), dtype=bool))
    ws_small = jnp.where(small[None, None], w_s[:, :, :DEC_SEQ, :DEC_SEQ], 0.0)
    wvec = jnp.repeat(ws_small.transpose(0, 2, 3, 1).reshape(DEPTH, DEC_SEQ * DEC_SEQ, HEADS), CHUNK, axis=-1)
    bvec = jnp.repeat(b_s[:, :, :DEC_SEQ].transpose(0, 2, 1), CHUNK, axis=-1)
    w_r = jnp.concatenate([w_rg, w_re, jnp.zeros((DEPTH, D, LANES - N_GROUPS_MOE - N_EXPERTS), F32)], axis=-1)
    wr_hi = w_r.astype(BF16)
    wr_lo = (w_r - wr_hi.astype(F32)).astype(BF16)
    b_r = jnp.concatenate([b_rg, b_re, jnp.zeros((DEPTH, LANES - N_GROUPS_MOE - N_EXPERTS), F32)],
                          axis=-1).reshape(DEPTH, 1, LANES)
    conv_t = state_conv.transpose(0, 2, 1, 3)
    pool_t = state_pool.transpose(0, 2, 1, 3)
    g1 = g_norm1.reshape(DEPTH, 1, D)
    g2 = g_norm2.reshape(DEPTH, 1, D)
    gv = g_v.reshape(DEPTH, 1, W)
    pscale = pool_scale.reshape(DEPTH, 1, W)

    h = _norm(x, g1, mod, 0)
    conv_p, pool_p, conv_s, pool_s, v_s = [], [], [], [], []
    y_final = None
    for l in range(DEPTH):
        ya, cst_p = _conv_p(h, w_in_b, conv_w, l)
        yb, pst_p = _pool_p(h, w_in_b, pool_w_b, pscale, l)
        yc = _gmlp_p(h, w_in_b, gv, ws_tril, bs_tile, l)
        ya_s, yb_s, cst_s, pst_s = _mix_s(h, w_in_b, conv_w, pool_w_b, pscale, conv_t, pool_t, l)
        yc_s, vn_s = _gmlp_s(h, w_in_b, gv, wvec, bvec, l)
        merged = _merge(h, (ya, yb, yc), (ya_s, yb_s, yc_s), w_in_b, w_br_b, l)
        x1, h2, route_w, route_e = _wo(merged, w_o_b, x, g2, mod, wr_hi, wr_lo, b_r, l)
        dest, src, w_sorted, tile_expert, tile_first, n_used = _dispatch(route_e, route_w)
        ys = _moe(h2, w_sorted, tile_expert, tile_first, n_used, src, w_gate, w_up, w_down, l)
        if l + 1 < DEPTH:
            x, h = _combine(dest, ys, x1, g1[l + 1], mod, l, final=False)
        else:
            y_final = _combine(dest, ys, x1, g_final.reshape(1, D), mod, l, final=True)
        conv_p.append(cst_p[:, 8 - (CONV_K - 1):, :])
        pool_p.append(pst_p[:, 16 - POOL_STATE:, :])
        conv_s.append(cst_s.transpose(1, 0, 2))
        pool_s.append(pst_s.transpose(1, 0, 2))
        v_s.append(vn_s.reshape(DEC_SEQ, DEC_BATCH, W).transpose(1, 0, 2))

    y_prompt = y_final[:NP].reshape(BATCH, SEQ, D)
    y_sample = y_final[NP:].reshape(DEC_SEQ, DEC_BATCH, D).transpose(1, 0, 2)
    return (y_prompt, y_sample, jnp.stack(conv_p), jnp.stack(pool_p), jnp.stack(conv_s),
            jnp.stack(pool_s), jnp.stack(v_s))
```

```python
import functools

import jax
import jax.numpy as jnp
from jax import lax
from jax.experimental import pallas as pl
from jax.experimental.pallas import tpu as pltpu

F32 = jnp.float32
BF16 = jnp.bfloat16

D = 2048
BATCH = 4
SEQ = 2048
DEPTH = 2
DEC_BATCH = 128
DEC_SEQ = 4
W = D // 2
CONV_K = 3
POOL_STATE = 15
N_POOL_GROUPS = 4
POOL_GROUP = W // N_POOL_GROUPS
CHUNK = 128
HEADS = W // CHUNK
IN_COLS = 6 * W + 3 * D
N_GROUPS_MOE = 4
EXP_PER_GROUP = 4
N_EXPERTS = 16
TOP_K = 2
D_FF = D // 4
EPS = 1e-6

NP = BATCH * SEQ
NS = DEC_BATCH * DEC_SEQ
NT = NP + NS
TM = 512
N_TILES = NT // TM
CB = 256
TE = 256
MAX_TILES = (NT * TOP_K) // TE + N_EXPERTS
P_ROWS = MAX_TILES * TE
TK = 256
TW = 256
LANES = 128
MOD_ROWS = 144
MOD_PROMPT_BLOCK = DEC_BATCH // 8

VMEM_LIMIT_V7X = 56 * 1024 * 1024


def _dot(a, b):
    return jnp.dot(a, b, preferred_element_type=F32)


def _params(sem, vmem=VMEM_LIMIT_V7X):
    return pltpu.CompilerParams(dimension_semantics=sem, vmem_limit_bytes=vmem)


def _sigmoid(x):
    return 1.0 / (1.0 + jnp.exp(-x))


def _rms(x):
    return lax.rsqrt(jnp.mean(x * x, axis=-1, keepdims=True) + EPS)


def _ada_kernel(c_ref, w_ref, b_ref, o_ref):
    o_ref[...] = _dot(c_ref[...], w_ref[...].astype(BF16)) + b_ref[...]


def _ada(c_all, w_ada, b_ada):
    nb = 512
    return pl.pallas_call(
        _ada_kernel,
        out_shape=jax.ShapeDtypeStruct((DEPTH, MOD_ROWS, 6 * D), F32),
        grid=(DEPTH, 6 * D // nb),
        in_specs=[
            pl.BlockSpec((MOD_ROWS, D), lambda l, j: (0, 0)),
            pl.BlockSpec((None, D, nb), lambda l, j: (l, 0, j)),
            pl.BlockSpec((None, 1, nb), lambda l, j: (l, 0, j)),
        ],
        out_specs=pl.BlockSpec((None, MOD_ROWS, nb), lambda l, j: (l, 0, j)),
        compiler_params=_params(("arbitrary", "arbitrary")),
        name="ada",
    )(c_all, w_ada, b_ada.reshape(DEPTH, 1, 6 * D))


def _mod_specs(l, q, tile_rows):
    del tile_rows
    return [
        pl.BlockSpec((None, 8, D), lambda i, *_: (l, MOD_PROMPT_BLOCK, q)),
        pl.BlockSpec((None, DEC_BATCH, D), lambda i, *_: (l, 0, q)),
    ]


def _mod_rows(i, tile_rows, mp_ref, ms_ref):
    tiles_per_seq = SEQ // tile_rows
    is_sample = i >= NP // tile_rows
    b = jnp.minimum(i // tiles_per_seq, BATCH - 1)
    mp = mp_ref[pl.ds(b, 1), :]
    ms = jnp.concatenate([ms_ref[...]] * (tile_rows // DEC_BATCH), axis=0)
    return jnp.where(is_sample, ms, mp)


def _norm_kernel(xp_ref, xs_ref, g_ref, scp, scs, shp, shs, h_ref):
    i = pl.program_id(0)
    x = jnp.where(i >= NP // TM, xs_ref[...], xp_ref[...])
    sc = _mod_rows(i, TM, scp, scs)
    sh = _mod_rows(i, TM, shp, shs)
    h_ref[...] = (((x * _rms(x)) * g_ref[...]) * (1.0 + sc) + sh).astype(BF16)


def _norm(x_prompt_rows, x_sample_rows, g, mod, l):
    n_ptiles = NP // TM
    return pl.pallas_call(
        _norm_kernel,
        out_shape=jax.ShapeDtypeStruct((NT, D), BF16),
        grid=(N_TILES,),
        in_specs=[pl.BlockSpec((TM, D), lambda i: (jnp.minimum(i, n_ptiles - 1), 0)),
                  pl.BlockSpec((TM, D), lambda i: (jnp.maximum(i - n_ptiles, 0), 0)),
                  pl.BlockSpec((None, 1, D), lambda i: (l, 0, 0))]
                 + _mod_specs(l, 1, TM) + _mod_specs(l, 0, TM),
        out_specs=pl.BlockSpec((TM, D), lambda i: (i, 0)),
        compiler_params=_params(("arbitrary",)),
        name="norm1",
    )(x_prompt_rows, x_sample_rows, g, mod, mod, mod, mod)


def _shift_rows(a, s, row):
    return jnp.where(row >= s, pltpu.roll(a, s, 0), 0.0)


def _conv_p_kernel(h_ref, wb_ref, wc_ref, wh_ref, cw_ref, y_ref, st_ref):
    h = h_ref[...]
    z = _dot(h, wc_ref[...]) * _dot(h, wh_ref[...])
    row = lax.broadcasted_iota(jnp.int32, (SEQ, 1), 0)
    cw = cw_ref[...]
    y = cw[0:1] * _shift_rows(z, 2, row) + cw[1:2] * _shift_rows(z, 1, row) + cw[2:3] * z
    y_ref[...] = (_dot(h, wb_ref[...]) * y).astype(BF16)
    st_ref[...] = z[SEQ - 8:, :]


def _conv_p(h, w_in, conv_w, l):
    nc = W // CB
    return pl.pallas_call(
        _conv_p_kernel,
        out_shape=(jax.ShapeDtypeStruct((NP, W), BF16),
                   jax.ShapeDtypeStruct((BATCH, 8, W), F32)),
        grid=(BATCH, nc),
        in_specs=[
            pl.BlockSpec((SEQ, D), lambda b, c: (b, 0)),
            pl.BlockSpec((None, D, CB), lambda b, c: (l, 0, c)),
            pl.BlockSpec((None, D, CB), lambda b, c: (l, 0, nc + c)),
            pl.BlockSpec((None, D, CB), lambda b, c: (l, 0, 2 * nc + c)),
            pl.BlockSpec((None, CONV_K, CB), lambda b, c: (l, 0, c)),
        ],
        out_specs=(pl.BlockSpec((SEQ, CB), lambda b, c: (b, c)),
                   pl.BlockSpec((None, 8, CB), lambda b, c: (b, 0, c))),
        compiler_params=_params(("arbitrary", "arbitrary")),
        name="conv_prompt",
    )(h, w_in, w_in, w_in, conv_w)


def _pool_p_kernel(h_ref, wp_ref, pw_ref, ps_ref, y_ref, st_ref):
    g = pl.program_id(1)
    p = _dot(h_ref[...], wp_ref[...])
    row = lax.broadcasted_iota(jnp.int32, (SEQ, 1), 0)
    s = p + _shift_rows(p, 1, row)
    s = jnp.where(g >= 1, s + _shift_rows(s, 2, row), s)
    s = jnp.where(g >= 2, s + _shift_rows(s, 4, row), s)
    s = jnp.where(g >= 3, s + _shift_rows(s, 8, row), s)
    window = jnp.left_shift(2, g)
    cnt = jnp.minimum(row + 1, window).astype(F32)
    d = s / cnt - p
    y_ref[...] = (_dot(d.astype(BF16), pw_ref[...]) * ps_ref[...]).astype(BF16)
    st_ref[...] = p[SEQ - 16:, :]


def _pool_p(h, w_in, pool_w, pool_scale, l):
    col0 = 3 * W // CB
    return pl.pallas_call(
        _pool_p_kernel,
        out_shape=(jax.ShapeDtypeStruct((NP, W), BF16),
                   jax.ShapeDtypeStruct((BATCH, 16, W), F32)),
        grid=(BATCH, N_POOL_GROUPS),
        in_specs=[
            pl.BlockSpec((SEQ, D), lambda b, g: (b, 0)),
            pl.BlockSpec((None, D, CB), lambda b, g: (l, 0, col0 + g)),
            pl.BlockSpec((None, None, POOL_GROUP, POOL_GROUP), lambda b, g: (l, g, 0, 0)),
            pl.BlockSpec((None, 1, CB), lambda b, g: (l, 0, g)),
        ],
        out_specs=(pl.BlockSpec((SEQ, CB), lambda b, g: (b, g)),
                   pl.BlockSpec((None, 16, CB), lambda b, g: (b, 0, g))),
        compiler_params=_params(("arbitrary", "arbitrary")),
        name="pool_prompt",
    )(h, w_in, pool_w, pool_scale)


def _gmlp_p_kernel(h_ref, wv_ref, wu_ref, gv_ref, ws_ref, bs_ref, y_ref, vn_s):
    j = pl.program_id(1)

    @pl.when(j == 0)
    def _():
        for r0 in range(0, SEQ, TM):
            v = _dot(h_ref[r0:r0 + TM, :], wv_ref[...])
            vn = (v * _rms(v)) * gv_ref[...]
            for hd in range(HEADS):
                vn_s[hd, r0:r0 + TM, :] = vn[:, hd * CHUNK:(hd + 1) * CHUNK].astype(BF16)

    u = _dot(h_ref[...], wu_ref[...])
    heads_per_block = CB // CHUNK
    for hh in range(heads_per_block):
        head = heads_per_block * j + hh
        wsh = ws_ref[head]
        bsh = bs_ref[head]
        for n in range(SEQ // CHUNK):
            rows = slice(n * CHUNK, (n + 1) * CHUNK)
            cols = slice(hh * CHUNK, (hh + 1) * CHUNK)
            s = _dot(wsh, vn_s[head, pl.ds(n * CHUNK, CHUNK), :]) + bsh
            y_ref[rows, cols] = (u[rows, cols] * s).astype(BF16)


def _gmlp_p(h, w_in, g_v, ws_tril, bs_tile, l):
    ucol0 = 4 * W // CB
    return pl.pallas_call(
        _gmlp_p_kernel,
        out_shape=jax.ShapeDtypeStruct((NP, W), BF16),
        grid=(BATCH, W // CB),
        in_specs=[
            pl.BlockSpec((SEQ, D), lambda b, j: (b, 0)),
            pl.BlockSpec((None, D, W), lambda b, j: (l, 0, 5)),
            pl.BlockSpec((None, D, CB), lambda b, j: (l, 0, ucol0 + j)),
            pl.BlockSpec((None, 1, W), lambda b, j: (l, 0, 0)),
            pl.BlockSpec((None, HEADS, CHUNK, CHUNK), lambda b, j: (l, 0, 0, 0)),
            pl.BlockSpec((None, HEADS, CHUNK, CHUNK), lambda b, j: (l, 0, 0, 0)),
        ],
        out_specs=pl.BlockSpec((SEQ, CB), lambda b, j: (b, j)),
        scratch_shapes=[pltpu.VMEM((HEADS, SEQ, CHUNK), BF16)],
        compiler_params=_params(("arbitrary", "arbitrary")),
        name="gmlp_prompt",
    )(h, w_in, w_in, g_v, ws_tril, bs_tile)


def _mix_s_kernel(h_ref, wb_ref, wc_ref, wh_ref, wp_ref, cw_ref, pc_ref, pp_ref,
                  pw_ref, ps_ref, ya_ref, yb_ref, cst_ref, pst_ref):
    g = pl.program_id(0)
    h = h_ref[...]
    nb = DEC_BATCH
    z = _dot(h, wc_ref[...]) * _dot(h, wh_ref[...])
    bg = _dot(h, wb_ref[...])
    zs = [pc_ref[0], pc_ref[1]] + [z[t * nb:(t + 1) * nb] for t in range(DEC_SEQ)]
    cw = cw_ref[...]
    for t in range(DEC_SEQ):
        y = cw[0:1] * zs[t] + cw[1:2] * zs[t + 1] + cw[2:3] * zs[t + 2]
        ya_ref[t * nb:(t + 1) * nb, :] = (bg[t * nb:(t + 1) * nb] * y).astype(BF16)
    cst_ref[0] = zs[DEC_SEQ]
    cst_ref[1] = zs[DEC_SEQ + 1]
    p = _dot(h, wp_ref[...])
    pp = [pp_ref[k] for k in range(POOL_STATE)] + [p[t * nb:(t + 1) * nb] for t in range(DEC_SEQ)]
    window = jnp.left_shift(2, g).astype(F32)
    ds = []
    for t in range(DEC_SEQ):
        e = POOL_STATE + t
        s = pp[e] + pp[e - 1]
        s4 = s + (pp[e - 2] + pp[e - 3])
        s8 = s4 + ((pp[e - 4] + pp[e - 5]) + (pp[e - 6] + pp[e - 7]))
        s16 = s8 + (((pp[e - 8] + pp[e - 9]) + (pp[e - 10] + pp[e - 11]))
                    + ((pp[e - 12] + pp[e - 13]) + (pp[e - 14] + pp[e - 15])))
        s = jnp.where(g >= 1, s4, s)
        s = jnp.where(g >= 2, s8, s)
        s = jnp.where(g >= 3, s16, s)
        ds.append(s / window - pp[e])
    d = jnp.concatenate(ds, axis=0)
    yb_ref[...] = (_dot(d.astype(BF16), pw_ref[...]) * ps_ref[...]).astype(BF16)
    for k in range(POOL_STATE):
        pst_ref[k] = pp[DEC_SEQ + k]


def _mix_s(h, w_in, conv_w, pool_w, pool_scale, conv_t, pool_t, l):
    nc = W // CB
    srow = NP // NS
    return pl.pallas_call(
        _mix_s_kernel,
        out_shape=(jax.ShapeDtypeStruct((NS, W), BF16),
                   jax.ShapeDtypeStruct((NS, W), BF16),
                   jax.ShapeDtypeStruct((CONV_K - 1, DEC_BATCH, W), F32),
                   jax.ShapeDtypeStruct((POOL_STATE, DEC_BATCH, W), F32)),
        grid=(nc,),
        in_specs=[
            pl.BlockSpec((NS, D), lambda c: (srow, 0)),
            pl.BlockSpec((None, D, CB), lambda c: (l, 0, c)),
            pl.BlockSpec((None, D, CB), lambda c: (l, 0, nc + c)),
            pl.BlockSpec((None, D, CB), lambda c: (l, 0, 2 * nc + c)),
            pl.BlockSpec((None, D, CB), lambda c: (l, 0, 3 * nc + c)),
            pl.BlockSpec((None, CONV_K, CB), lambda c: (l, 0, c)),
            pl.BlockSpec((None, CONV_K - 1, DEC_BATCH, CB), lambda c: (l, 0, 0, c)),
            pl.BlockSpec((None, POOL_STATE, DEC_BATCH, CB), lambda c: (l, 0, 0, c)),
            pl.BlockSpec((None, None, POOL_GROUP, POOL_GROUP), lambda c: (l, c, 0, 0)),
            pl.BlockSpec((None, 1, CB), lambda c: (l, 0, c)),
        ],
        out_specs=(pl.BlockSpec((NS, CB), lambda c: (0, c)),
                   pl.BlockSpec((NS, CB), lambda c: (0, c)),
                   pl.BlockSpec((CONV_K - 1, DEC_BATCH, CB), lambda c: (0, 0, c)),
                   pl.BlockSpec((POOL_STATE, DEC_BATCH, CB), lambda c: (0, 0, c))),
        compiler_params=_params(("arbitrary",)),
        name="mix_sample",
    )(h, w_in, w_in, w_in, w_in, conv_w, conv_t, pool_t, pool_w, pool_scale)


def _gmlp_s_kernel(h_ref, wu_ref, wv_ref, gv_ref, wvec_ref, bvec_ref, yc_ref, vn_ref):
    h = h_ref[...]
    nb = DEC_BATCH
    v = _dot(h, wv_ref[...])
    vn = (v * _rms(v)) * gv_ref[...]
    vn_ref[...] = vn
    u = _dot(h, wu_ref[...])
    for t in range(DEC_SEQ):
        s = bvec_ref[t:t + 1, :]
        for sp in range(t + 1):
            k = t * DEC_SEQ + sp
            s = s + wvec_ref[k:k + 1, :] * vn[sp * nb:(sp + 1) * nb]
        yc_ref[t * nb:(t + 1) * nb, :] = (u[t * nb:(t + 1) * nb] * s).astype(BF16)


def _gmlp_s(h, w_in, g_v, wvec, bvec, l):
    srow = NP // NS
    return pl.pallas_call(
        _gmlp_s_kernel,
        out_shape=(jax.ShapeDtypeStruct((NS, W), BF16),
                   jax.ShapeDtypeStruct((NS, W), F32)),
        grid=(1,),
        in_specs=[
            pl.BlockSpec((NS, D), lambda i: (srow, 0)),
            pl.BlockSpec((None, D, W), lambda i: (l, 0, 4)),
            pl.BlockSpec((None, D, W), lambda i: (l, 0, 5)),
            pl.BlockSpec((None, 1, W), lambda i: (l, 0, 0)),
            pl.BlockSpec((None, DEC_SEQ * DEC_SEQ, W), lambda i: (l, 0, 0)),
            pl.BlockSpec((None, DEC_SEQ, W), lambda i: (l, 0, 0)),
        ],
        out_specs=(pl.BlockSpec((NS, W), lambda i: (0, 0)),
                   pl.BlockSpec((NS, W), lambda i: (0, 0))),
        compiler_params=_params(("arbitrary",)),
        name="gmlp_sample",
    )(h, w_in, w_in, g_v, wvec, bvec)


def _merge_kernel(h_ref, yap, ybp, ycp, yas, ybs, ycs, wg0, wg1, wg2, wbr_ref, o_ref):
    is_sample = pl.program_id(0) >= NP // TM
    h = h_ref[...]
    acc = None
    for n, (yp, ys, wg) in enumerate(((yap, yas, wg0), (ybp, ybs, wg1), (ycp, ycs, wg2))):
        y = jnp.where(is_sample, ys[...], yp[...])
        term = _sigmoid(_dot(h, wg[...])) * _dot(y, wbr_ref[n])
        acc = term if acc is None else acc + term
    o_ref[...] = acc.astype(BF16)


def _merge(h, y_prompt, y_sample, w_in, w_br, l):
    db = 512
    g0 = 6 * W // db
    gstep = D // db
    return pl.pallas_call(
        _merge_kernel,
        out_shape=jax.ShapeDtypeStruct((NT, D), BF16),
        grid=(N_TILES, D // db),
        in_specs=[
            pl.BlockSpec((TM, D), lambda i, d: (i, 0)),
            pl.BlockSpec((TM, W), lambda i, d: (jnp.minimum(i, NP // TM - 1), 0)),
            pl.BlockSpec((TM, W), lambda i, d: (jnp.minimum(i, NP // TM - 1), 0)),
            pl.BlockSpec((TM, W), lambda i, d: (jnp.minimum(i, NP // TM - 1), 0)),
            pl.BlockSpec((NS, W), lambda i, d: (0, 0)),
            pl.BlockSpec((NS, W), lambda i, d: (0, 0)),
            pl.BlockSpec((NS, W), lambda i, d: (0, 0)),
            pl.BlockSpec((None, D, db), lambda i, d: (l, 0, g0 + d)),
            pl.BlockSpec((None, D, db), lambda i, d: (l, 0, g0 + gstep + d)),
            pl.BlockSpec((None, D, db), lambda i, d: (l, 0, g0 + 2 * gstep + d)),
            pl.BlockSpec((None, 3, W, db), lambda i, d: (l, 0, 0, d)),
        ],
        out_specs=pl.BlockSpec((TM, db), lambda i, d: (i, d)),
        compiler_params=_params(("arbitrary", "arbitrary")),
        name="merge",
    )(h, *y_prompt, *y_sample, w_in, w_in, w_in, w_br)


def _route(logits):
    lane = lax.broadcasted_iota(jnp.int32, logits.shape, 1).astype(F32)
    neg = -jnp.inf
    big = float(LANES)
    is_grp = lane < N_GROUPS_MOE
    gl = jnp.where(is_grp, logits, neg)
    gmax = jnp.max(gl, axis=-1, keepdims=True)
    gsel = jnp.min(jnp.where(gl == gmax, lane, big), axis=-1, keepdims=True)
    gp = 1.0 / jnp.sum(jnp.where(is_grp, jnp.exp(logits - gmax), 0.0), axis=-1, keepdims=True)
    lo = N_GROUPS_MOE + gsel * EXP_PER_GROUP
    in_grp = (lane >= lo) & (lane < lo + EXP_PER_GROUP)
    el = jnp.where(in_grp, logits, neg)
    m1 = jnp.max(el, axis=-1, keepdims=True)
    i1 = jnp.min(jnp.where(el == m1, lane, big), axis=-1, keepdims=True)
    el2 = jnp.where(lane == i1, neg, el)
    m2 = jnp.max(el2, axis=-1, keepdims=True)
    i2 = jnp.min(jnp.where(el2 == m2, lane, big), axis=-1, keepdims=True)
    e = jnp.exp(m2 - m1)
    w1 = gp / (1.0 + e)
    w2 = gp * (e / (1.0 + e))
    rw = jnp.where(lane == 0.0, w1, jnp.where(lane == 1.0, w2, 0.0))
    return rw, lane, i1, i2


def _wo_kernel(m_ref, wo_ref, xp_ref, xs_ref, g_ref, gtp, gts, scp, scs, shp, shs, wrc_ref, wrh_ref, br_ref,
               x1_ref, h2_ref, rw_ref, re_ref, cnt_ref, carry):
    i = pl.program_id(0)
    x = jnp.where(i >= NP // TW, xs_ref[...], xp_ref[...])
    gt = _mod_rows(i, TW, gtp, gts)
    x1 = x + gt * _dot(m_ref[...], wo_ref[...])
    x1_ref[...] = x1
    sc = _mod_rows(i, TW, scp, scs)
    sh = _mod_rows(i, TW, shp, shs)
    h2 = ((x1 * _rms(x1)) * g_ref[...]) * (1.0 + sc) + sh
    h2_ref[...] = h2
    hi = h2.astype(BF16)
    lo = (h2 - hi.astype(F32)).astype(BF16)
    both = _dot(hi, wrc_ref[...])
    logits = (both[:, :LANES] + (both[:, LANES:] + _dot(lo, wrh_ref[...]))) + br_ref[...]
    rw, lane, i1, i2 = _route(logits)
    rw_ref[...] = rw

    @pl.when(i == 0)
    def _():
        carry[...] = jnp.zeros_like(carry)

    o1 = (lane == i1).astype(F32)
    o2 = (lane == i2).astype(F32)
    both_hot = o1 + o2
    r = lax.broadcasted_iota(jnp.int32, (TW, TW), 0)
    c = lax.broadcasted_iota(jnp.int32, (TW, TW), 1)
    before = (r > c).astype(BF16)
    seen = _dot(before, both_hot.astype(BF16)) + carry[...]
    rank1 = jnp.sum(seen * o1, axis=-1, keepdims=True)
    rank2 = jnp.sum(seen * o2, axis=-1, keepdims=True)
    total = carry[...] + jnp.sum(both_hot, axis=0, keepdims=True)
    carry[...] = total
    cnt_ref[...] = jnp.broadcast_to(total, cnt_ref.shape).astype(jnp.int32)
    re = jnp.where(lane == 0.0, i1 - N_GROUPS_MOE,
                   jnp.where(lane == 1.0, i2 - N_GROUPS_MOE,
                             jnp.where(lane == 2.0, rank1, jnp.where(lane == 3.0, rank2, 0.0))))
    re_ref[...] = re.astype(jnp.int32)


def _wo(merged, w_o, x_prompt_rows, x_sample_rows, sample_block0, g2, mod, wr_cat, wr_hi, b_r, l):
    row = lambda i: (i, 0)
    n_ptiles = NP // TW
    return pl.pallas_call(
        _wo_kernel,
        out_shape=(jax.ShapeDtypeStruct((NT, D), F32),
                   jax.ShapeDtypeStruct((NT, D), F32),
                   jax.ShapeDtypeStruct((NT, LANES), F32),
                   jax.ShapeDtypeStruct((NT, LANES), jnp.int32),
                   jax.ShapeDtypeStruct((8, LANES), jnp.int32)),
        grid=(NT // TW,),
        in_specs=[
            pl.BlockSpec((TW, D), row),
            pl.BlockSpec((None, D, D), lambda i: (l, 0, 0)),
            pl.BlockSpec((TW, D), lambda i: (jnp.minimum(i, n_ptiles - 1), 0)),
            pl.BlockSpec((TW, D), lambda i: (sample_block0 + jnp.maximum(i - n_ptiles, 0), 0)),
            pl.BlockSpec((None, 1, D), lambda i: (l, 0, 0)),
        ] + _mod_specs(l, 2, TW) + _mod_specs(l, 4, TW) + _mod_specs(l, 3, TW) + [
            pl.BlockSpec((None, D, 2 * LANES), lambda i: (l, 0, 0)),
            pl.BlockSpec((None, D, LANES), lambda i: (l, 0, 0)),
            pl.BlockSpec((None, 1, LANES), lambda i: (l, 0, 0)),
        ],
        out_specs=(pl.BlockSpec((TW, D), row), pl.BlockSpec((TW, D), row),
                   pl.BlockSpec((TW, LANES), row), pl.BlockSpec((TW, LANES), row),
                   pl.BlockSpec((8, LANES), lambda i: (0, 0))),
        scratch_shapes=[pltpu.VMEM((1, LANES), F32)],
        compiler_params=_params(("arbitrary",)),
        name="wo_router",
    )(merged, w_o, x_prompt_rows, x_sample_rows, g2, mod, mod, mod, mod, mod, mod, wr_cat, wr_hi, b_r)


def _row_gather_copy(src_hbm, idx, dst, sem):
    return pltpu.make_async_copy(src_hbm.at[pl.ds(idx, 1)], dst, sem)


def _moe_kernel(te_ref, tf_ref, nu_ref, dest_ref, h2_hbm, wg_ref, wu_ref, wd_ref, o_ref,
                src_ref, xbuf, sem, wg_s, wu_s, wd_s):
    del te_ref
    t = pl.program_id(0)
    n_used = nu_ref[0]

    def issue(tile, slot):
        base = tile * TE

        def body(r, carry):
            _row_gather_copy(h2_hbm, src_ref[base + r], xbuf.at[slot, pl.ds(r, 1)], sem.at[slot]).start()
            return carry

        lax.fori_loop(0, TE, body, 0, unroll=8)

    @pl.when(t == 0)
    def _():
        def clear(p, carry):
            src_ref[p] = 0
            return carry

        lax.fori_loop(0, P_ROWS, clear, 0, unroll=16)

        def invert(q, carry):
            src_ref[dest_ref[q]] = lax.shift_right_logical(q, 1)
            return carry

        lax.fori_loop(0, NT * TOP_K, invert, 0, unroll=16)
        issue(0, 0)

    @pl.when(t + 1 < n_used)
    def _():
        issue(t + 1, (t + 1) % 2)

    @pl.when((t < n_used) & (tf_ref[t] == 1))
    def _():
        wg_s[...] = wg_ref[...].astype(BF16)
        wu_s[...] = wu_ref[...].astype(BF16)
        wd_s[...] = wd_ref[...].astype(BF16)

    @pl.when(t < n_used)
    def _():
        slot = t % 2
        pltpu.make_async_copy(h2_hbm.at[pl.ds(0, TE)], xbuf.at[slot], sem.at[slot]).wait()
        x = xbuf[slot].astype(BF16)
        a = _dot(x, wg_s[...])
        b = _dot(x, wu_s[...])
        hid = (a * _sigmoid(a)) * b
        o_ref[...] = _dot(hid.astype(BF16), wd_s[...])

    @pl.when(t >= n_used)
    def _():
        o_ref[...] = jnp.zeros_like(o_ref)


def _moe(h2, tile_expert, tile_first, n_used, dest, w_gate, w_up, w_down, l):
    grid_spec = pltpu.PrefetchScalarGridSpec(
        num_scalar_prefetch=4,
        grid=(MAX_TILES,),
        in_specs=[
            pl.BlockSpec(memory_space=pl.ANY),
            pl.BlockSpec((None, None, D, D_FF), lambda t, te, tf, nu, dst: (l, te[t], 0, 0)),
            pl.BlockSpec((None, None, D, D_FF), lambda t, te, tf, nu, dst: (l, te[t], 0, 0)),
            pl.BlockSpec((None, None, D_FF, D), lambda t, te, tf, nu, dst: (l, te[t], 0, 0)),
        ],
        out_specs=pl.BlockSpec((TE, D), lambda t, te, tf, nu, dst: (t, 0)),
        scratch_shapes=[
            pltpu.SMEM((P_ROWS,), jnp.int32),
            pltpu.VMEM((2, TE, D), F32),
            pltpu.SemaphoreType.DMA((2,)),
            pltpu.VMEM((D, D_FF), BF16),
            pltpu.VMEM((D, D_FF), BF16),
            pltpu.VMEM((D_FF, D), BF16),
        ],
    )
    return pl.pallas_call(
        _moe_kernel,
        out_shape=jax.ShapeDtypeStruct((P_ROWS, D), F32),
        grid_spec=grid_spec,
        compiler_params=_params(("arbitrary",)),
        name="moe_experts",
    )(tile_expert, tile_first, n_used, dest, h2, w_gate, w_up, w_down)


def _dispatch(route_e, counts_lanes):
    counts = counts_lanes[0, N_GROUPS_MOE:N_GROUPS_MOE + N_EXPERTS]
    tiles = (counts + TE - 1) // TE
    eids = jnp.arange(N_EXPERTS, dtype=jnp.int32)
    tile_end = jnp.sum(jnp.where(eids[None, :] <= eids[:, None], tiles[None, :], 0), axis=1)
    tile_start = tile_end - tiles
    hot = route_e[:, :TOP_K, None] == eids[None, None, :]
    row0 = jnp.sum(jnp.where(hot, tile_start[None, None, :] * TE, 0), axis=-1)
    dest = (row0 + route_e[:, TOP_K:2 * TOP_K]).reshape(-1).astype(jnp.int32)
    tids = jnp.arange(MAX_TILES, dtype=jnp.int32)
    tile_expert = jnp.minimum(jnp.sum((tile_end[None, :] <= tids[:, None]).astype(jnp.int32), axis=1),
                              N_EXPERTS - 1)
    first_tile = jnp.sum(jnp.where(tile_expert[:, None] == eids[None, :], tile_start[None, :], 0), axis=1)
    tile_first = (tids == first_tile).astype(jnp.int32)
    n_used = tile_end[-1:].astype(jnp.int32)
    return dest, tile_expert, tile_first, n_used


def _combine_kernel(final, tile0, dest_ref, ys_hbm, x_ref, rw_ref, g_ref, gtp, gts, *rest):
    if final:
        y_ref, ybuf, sem = rest
    else:
        scp, scs, shp, shs, x2_ref, h_ref, ybuf, sem = rest
    step = pl.program_id(0)
    nsteps = pl.num_programs(0)
    i = step + tile0

    def issue(tile, slot):
        base = tile * (TK * TOP_K)

        def body(r, carry):
            for k in range(TOP_K):
                _row_gather_copy(ys_hbm, dest_ref[base + TOP_K * r + k],
                                 ybuf.at[slot, k, pl.ds(r, 1)], sem.at[slot]).start()
            return carry

        lax.fori_loop(0, TK, body, 0, unroll=8)

    @pl.when(step == 0)
    def _():
        issue(i, 0)

    @pl.when(step + 1 < nsteps)
    def _():
        issue(i + 1, (step + 1) % 2)

    slot = step % 2
    for k in range(TOP_K):
        pltpu.make_async_copy(ys_hbm.at[pl.ds(0, TK)], ybuf.at[slot, k], sem.at[slot]).wait()
    gt = _mod_rows(i, TK, gtp, gts)
    rw = rw_ref[...]
    moe = rw[:, 0:1] * ybuf[slot, 0] + rw[:, 1:2] * ybuf[slot, 1]
    x2 = x_ref[...] + gt * moe
    xn = (x2 * _rms(x2)) * g_ref[...]
    if final:
        y_ref[...] = xn
    else:
        x2_ref[...] = x2
        sc = _mod_rows(i, TK, scp, scs)
        sh = _mod_rows(i, TK, shp, shs)
        h_ref[...] = (xn * (1.0 + sc) + sh).astype(BF16)


def _combine(dest, ys, x1, route_w, g, mod, l, final, tile0=0, ntiles=NT // TK):
    row_in = lambda s, d: (s + tile0, 0)
    row_out = lambda s, d: (s, 0)
    nmod = lambda q, ll: [
        pl.BlockSpec((None, 8, D), lambda s, d: (ll, MOD_PROMPT_BLOCK, q)),
        pl.BlockSpec((None, DEC_BATCH, D), lambda s, d: (ll, 0, q)),
    ]
    in_specs = [pl.BlockSpec(memory_space=pl.ANY), pl.BlockSpec((TK, D), row_in),
                pl.BlockSpec((TK, LANES), row_in),
                pl.BlockSpec((1, D), lambda s, d: (0, 0))] + nmod(5, l)
    args = [ys, x1, route_w, g, mod, mod]
    rows = ntiles * TK
    if final:
        out_shape = jax.ShapeDtypeStruct((rows, D), F32)
        out_specs = pl.BlockSpec((TK, D), row_out)
    else:
        in_specs += nmod(1, l + 1) + nmod(0, l + 1)
        args += [mod, mod, mod, mod]
        out_shape = (jax.ShapeDtypeStruct((rows, D), F32), jax.ShapeDtypeStruct((rows, D), BF16))
        out_specs = (pl.BlockSpec((TK, D), row_out), pl.BlockSpec((TK, D), row_out))
    grid_spec = pltpu.PrefetchScalarGridSpec(
        num_scalar_prefetch=1,
        grid=(ntiles,),
        in_specs=in_specs,
        out_specs=out_specs,
        scratch_shapes=[pltpu.VMEM((2, TOP_K, TK, D), F32), pltpu.SemaphoreType.DMA((2,))],
    )
    return pl.pallas_call(
        functools.partial(_combine_kernel, final, tile0),
        out_shape=out_shape,
        grid_spec=grid_spec,
        compiler_params=_params(("arbitrary",)),
        name=("combine_final_%d" % tile0) if final else "combine",
    )(dest, *args)


def kernel(x_prompt, x_sample, c_prompt, c_sample, state_conv, state_pool, w_ada, b_ada, g_norm1, w_in,
           conv_w, pool_w, pool_scale, g_v, w_s, b_s, w_br, w_o, g_norm2, w_rg, b_rg, w_re, b_re,
           w_gate, w_up, w_down, g_final):
    xp_rows = x_prompt.reshape(NP, D)
    xs_rows = x_sample.transpose(1, 0, 2).reshape(NS, D)
    sample_block0 = 0
    c_all = jnp.concatenate([c_sample, c_prompt,
                             jnp.zeros((MOD_ROWS - DEC_BATCH - BATCH, D), F32)], axis=0).astype(BF16)
    mod = _ada(c_all, w_ada, b_ada)

    w_in_b = w_in.astype(BF16)
    w_br_b = w_br.astype(BF16)
    w_o_b = w_o.astype(BF16)
    pool_w_b = pool_w.astype(BF16)
    tril = jnp.tril(jnp.ones((CHUNK, CHUNK), dtype=bool))
    ws_tril = jnp.where(tril[None, None], w_s, 0.0).astype(BF16)
    bs_tile = jnp.broadcast_to(b_s[:, :, :, None], (DEPTH, HEADS, CHUNK, CHUNK))
    small = jnp.tril(jnp.ones((DEC_SEQ, DEC_SEQ), dtype=bool))
    ws_small = jnp.where(small[None, None], w_s[:, :, :DEC_SEQ, :DEC_SEQ], 0.0)
    wvec = jnp.repeat(ws_small.transpose(0, 2, 3, 1).reshape(DEPTH, DEC_SEQ * DEC_SEQ, HEADS), CHUNK, axis=-1)
    bvec = jnp.repeat(b_s[:, :, :DEC_SEQ].transpose(0, 2, 1), CHUNK, axis=-1)
    w_r = jnp.concatenate([w_rg, w_re, jnp.zeros((DEPTH, D, LANES - N_GROUPS_MOE - N_EXPERTS), F32)], axis=-1)
    wr_hi = w_r.astype(BF16)
    wr_lo = (w_r - wr_hi.astype(F32)).astype(BF16)
    wr_cat = jnp.concatenate([wr_hi, wr_lo], axis=-1)
    b_r = jnp.concatenate([b_rg, b_re, jnp.zeros((DEPTH, LANES - N_GROUPS_MOE - N_EXPERTS), F32)],
                          axis=-1).reshape(DEPTH, 1, LANES)
    conv_t = state_conv.transpose(0, 2, 1, 3)
    pool_t = state_pool.transpose(0, 2, 1, 3)
    g1 = g_norm1.reshape(DEPTH, 1, D)
    g2 = g_norm2.reshape(DEPTH, 1, D)
    gv = g_v.reshape(DEPTH, 1, W)
    pscale = pool_scale.reshape(DEPTH, 1, W)

    h = _norm(xp_rows, xs_rows, g1, mod, 0)
    conv_p, pool_p, conv_s, pool_s, v_s = [], [], [], [], []
    y_prompt = y_sample = None
    for l in range(DEPTH):
        ya, cst_p = _conv_p(h, w_in_b, conv_w, l)
        yb, pst_p = _pool_p(h, w_in_b, pool_w_b, pscale, l)
        yc = _gmlp_p(h, w_in_b, gv, ws_tril, bs_tile, l)
        ya_s, yb_s, cst_s, pst_s = _mix_s(h, w_in_b, conv_w, pool_w_b, pscale, conv_t, pool_t, l)
        yc_s, vn_s = _gmlp_s(h, w_in_b, gv, wvec, bvec, l)
        merged = _merge(h, (ya, yb, yc), (ya_s, yb_s, yc_s), w_in_b, w_br_b, l)
        x1, h2, route_w, route_e, counts = _wo(merged, w_o_b, xp_rows, xs_rows, sample_block0, g2, mod,
                                               wr_cat, wr_hi, b_r, l)
        dest, tile_expert, tile_first, n_used = _dispatch(route_e, counts)
        ys = _moe(h2, tile_expert, tile_first, n_used, dest, w_gate, w_up, w_down, l)
        if l + 1 < DEPTH:
            x, h = _combine(dest, ys, x1, route_w, g1[l + 1], mod, l, final=False)
            xp_rows = xs_rows = x
            sample_block0 = NP // TW
        else:
            gf = g_final.reshape(1, D)
            y_prompt = _combine(dest, ys, x1, route_w, gf, mod, l, final=True, tile0=0, ntiles=NP // TK)
            y_sample = _combine(dest, ys, x1, route_w, gf, mod, l, final=True, tile0=NP // TK,
                                ntiles=NS // TK)
        conv_p.append(cst_p[:, 8 - (CONV_K - 1):, :])
        pool_p.append(pst_p[:, 16 - POOL_STATE:, :])
        conv_s.append(cst_s.transpose(1, 0, 2))
        pool_s.append(pst_s.transpose(1, 0, 2))
        v_s.append(vn_s.reshape(DEC_SEQ, DEC_BATCH, W).transpose(1, 0, 2))

    y_prompt = y_prompt.reshape(BATCH, SEQ, D)
    y_sample = y_sample.reshape(DEC_SEQ, DEC_BATCH, D).transpose(1, 0, 2)
    return (y_prompt, y_sample, jnp.stack(conv_p), jnp.stack(pool_p), jnp.stack(conv_s),
            jnp.stack(pool_s), jnp.stack(v_s))
```

```python
import functools

import jax
import jax.numpy as jnp
from jax import lax
from jax.experimental import pallas as pl
from jax.experimental.pallas import tpu as pltpu

F32 = jnp.float32
BF16 = jnp.bfloat16

D = 2048
BATCH = 4
SEQ = 2048
DEPTH = 2
DEC_BATCH = 128
DEC_SEQ = 4
W = D // 2
CONV_K = 3
POOL_STATE = 15
N_POOL_GROUPS = 4
POOL_GROUP = W // N_POOL_GROUPS
CHUNK = 128
HEADS = W // CHUNK
IN_COLS = 6 * W + 3 * D
N_GROUPS_MOE = 4
EXP_PER_GROUP = 4
N_EXPERTS = 16
TOP_K = 2
D_FF = D // 4
EPS = 1e-6

NP = BATCH * SEQ
NS = DEC_BATCH * DEC_SEQ
NT = NP + NS
TM = 512
N_TILES = NT // TM
CB = 256
TE = 256
MAX_TILES = (NT * TOP_K) // TE + N_EXPERTS
P_ROWS = MAX_TILES * TE
TK = 256
TW = 256
LANES = 128
MOD_ROWS = 144
MOD_PROMPT_BLOCK = DEC_BATCH // 8

VMEM_LIMIT_V7X = 56 * 1024 * 1024


def _dot(a, b):
    return jnp.dot(a, b, preferred_element_type=F32)


def _params(sem, vmem=VMEM_LIMIT_V7X):
    return pltpu.CompilerParams(dimension_semantics=sem, vmem_limit_bytes=vmem)


def _sigmoid(x):
    return 1.0 / (1.0 + jnp.exp(-x))


def _rms(x):
    return lax.rsqrt(jnp.mean(x * x, axis=-1, keepdims=True) + EPS)


def _ada_kernel(c_ref, w_ref, b_ref, o_ref):
    o_ref[...] = _dot(c_ref[...], w_ref[...].astype(BF16)) + b_ref[...]


def _ada(c_all, w_ada, b_ada):
    nb = 512
    return pl.pallas_call(
        _ada_kernel,
        out_shape=jax.ShapeDtypeStruct((DEPTH, MOD_ROWS, 6 * D), F32),
        grid=(DEPTH, 6 * D // nb),
        in_specs=[
            pl.BlockSpec((MOD_ROWS, D), lambda l, j: (0, 0)),
            pl.BlockSpec((None, D, nb), lambda l, j: (l, 0, j)),
            pl.BlockSpec((None, 1, nb), lambda l, j: (l, 0, j)),
        ],
        out_specs=pl.BlockSpec((None, MOD_ROWS, nb), lambda l, j: (l, 0, j)),
        compiler_params=_params(("arbitrary", "arbitrary")),
        name="ada",
    )(c_all, w_ada, b_ada.reshape(DEPTH, 1, 6 * D))


def _mod_specs(l, q, tile_rows):
    del tile_rows
    return [
        pl.BlockSpec((None, 8, D), lambda i, *_: (l, MOD_PROMPT_BLOCK, q)),
        pl.BlockSpec((None, DEC_BATCH, D), lambda i, *_: (l, 0, q)),
    ]


def _mod_rows(i, tile_rows, mp_ref, ms_ref):
    tiles_per_seq = SEQ // tile_rows
    is_sample = i >= NP // tile_rows
    b = jnp.minimum(i // tiles_per_seq, BATCH - 1)
    mp = mp_ref[pl.ds(b, 1), :]
    ms = jnp.concatenate([ms_ref[...]] * (tile_rows // DEC_BATCH), axis=0)
    return jnp.where(is_sample, ms, mp)


def _norm_kernel(xp_ref, xs_ref, g_ref, scp, scs, shp, shs, h_ref):
    i = pl.program_id(0)
    x = jnp.where(i >= NP // TM, xs_ref[...], xp_ref[...])
    sc = _mod_rows(i, TM, scp, scs)
    sh = _mod_rows(i, TM, shp, shs)
    h_ref[...] = (((x * _rms(x)) * g_ref[...]) * (1.0 + sc) + sh).astype(BF16)


def _norm(x_prompt_rows, x_sample_rows, g, mod, l):
    n_ptiles = NP // TM
    return pl.pallas_call(
        _norm_kernel,
        out_shape=jax.ShapeDtypeStruct((NT, D), BF16),
        grid=(N_TILES,),
        in_specs=[pl.BlockSpec((TM, D), lambda i: (jnp.minimum(i, n_ptiles - 1), 0)),
                  pl.BlockSpec((TM, D), lambda i: (jnp.maximum(i - n_ptiles, 0), 0)),
                  pl.BlockSpec((None, 1, D), lambda i: (l, 0, 0))]
                 + _mod_specs(l, 1, TM) + _mod_specs(l, 0, TM),
        out_specs=pl.BlockSpec((TM, D), lambda i: (i, 0)),
        compiler_params=_params(("arbitrary",)),
        name="norm1",
    )(x_prompt_rows, x_sample_rows, g, mod, mod, mod, mod)


def _shift_rows(a, s, row):
    return jnp.where(row >= s, pltpu.roll(a, s, 0), 0.0)


def _conv_p_kernel(h_ref, wb_ref, wc_ref, wh_ref, cw_ref, y_ref, st_ref):
    h = h_ref[...]
    z = _dot(h, wc_ref[...]) * _dot(h, wh_ref[...])
    row = lax.broadcasted_iota(jnp.int32, (SEQ, 1), 0)
    cw = cw_ref[...]
    y = cw[0:1] * _shift_rows(z, 2, row) + cw[1:2] * _shift_rows(z, 1, row) + cw[2:3] * z
    y_ref[...] = (_dot(h, wb_ref[...]) * y).astype(BF16)
    st_ref[...] = z[SEQ - 8:, :]


def _conv_p(h, w_in, conv_w, l):
    nc = W // CB
    return pl.pallas_call(
        _conv_p_kernel,
        out_shape=(jax.ShapeDtypeStruct((NP, W), BF16),
                   jax.ShapeDtypeStruct((BATCH, 8, W), F32)),
        grid=(BATCH, nc),
        in_specs=[
            pl.BlockSpec((SEQ, D), lambda b, c: (b, 0)),
            pl.BlockSpec((None, D, CB), lambda b, c: (l, 0, c)),
            pl.BlockSpec((None, D, CB), lambda b, c: (l, 0, nc + c)),
            pl.BlockSpec((None, D, CB), lambda b, c: (l, 0, 2 * nc + c)),
            pl.BlockSpec((None, CONV_K, CB), lambda b, c: (l, 0, c)),
        ],
        out_specs=(pl.BlockSpec((SEQ, CB), lambda b, c: (b, c)),
                   pl.BlockSpec((None, 8, CB), lambda b, c: (b, 0, c))),
        compiler_params=_params(("arbitrary", "arbitrary")),
        name="conv_prompt",
    )(h, w_in, w_in, w_in, conv_w)


def _pool_p_kernel(h_ref, wp_ref, pw_ref, ps_ref, y_ref, st_ref):
    g = pl.program_id(1)
    p = _dot(h_ref[...], wp_ref[...])
    row = lax.broadcasted_iota(jnp.int32, (SEQ, 1), 0)
    s = p + _shift_rows(p, 1, row)
    s = jnp.where(g >= 1, s + _shift_rows(s, 2, row), s)
    s = jnp.where(g >= 2, s + _shift_rows(s, 4, row), s)
    s = jnp.where(g >= 3, s + _shift_rows(s, 8, row), s)
    window = jnp.left_shift(2, g)
    cnt = jnp.minimum(row + 1, window).astype(F32)
    d = s / cnt - p
    y_ref[...] = (_dot(d.astype(BF16), pw_ref[...]) * ps_ref[...]).astype(BF16)
    st_ref[...] = p[SEQ - 16:, :]


def _pool_p(h, w_in, pool_w, pool_scale, l):
    col0 = 3 * W // CB
    return pl.pallas_call(
        _pool_p_kernel,
        out_shape=(jax.ShapeDtypeStruct((NP, W), BF16),
                   jax.ShapeDtypeStruct((BATCH, 16, W), F32)),
        grid=(BATCH, N_POOL_GROUPS),
        in_specs=[
            pl.BlockSpec((SEQ, D), lambda b, g: (b, 0)),
            pl.BlockSpec((None, D, CB), lambda b, g: (l, 0, col0 + g)),
            pl.BlockSpec((None, None, POOL_GROUP, POOL_GROUP), lambda b, g: (l, g, 0, 0)),
            pl.BlockSpec((None, 1, CB), lambda b, g: (l, 0, g)),
        ],
        out_specs=(pl.BlockSpec((SEQ, CB), lambda b, g: (b, g)),
                   pl.BlockSpec((None, 16, CB), lambda b, g: (b, 0, g))),
        compiler_params=_params(("arbitrary", "arbitrary")),
        name="pool_prompt",
    )(h, w_in, pool_w, pool_scale)


def _gmlp_p_kernel(h_ref, wv_ref, wu_ref, gv_ref, ws_ref, bs_ref, y_ref, vn_s):
    j = pl.program_id(1)

    @pl.when(j == 0)
    def _():
        for r0 in range(0, SEQ, TM):
            v = _dot(h_ref[r0:r0 + TM, :], wv_ref[...])
            vn = (v * _rms(v)) * gv_ref[...]
            for hd in range(HEADS):
                vn_s[hd, r0:r0 + TM, :] = vn[:, hd * CHUNK:(hd + 1) * CHUNK].astype(BF16)

    u = _dot(h_ref[...], wu_ref[...])
    heads_per_block = CB // CHUNK
    for hh in range(heads_per_block):
        head = heads_per_block * j + hh
        wsh = ws_ref[head]
        bsh = bs_ref[head]
        for n in range(SEQ // CHUNK):
            rows = slice(n * CHUNK, (n + 1) * CHUNK)
            cols = slice(hh * CHUNK, (hh + 1) * CHUNK)
            s = _dot(wsh, vn_s[head, pl.ds(n * CHUNK, CHUNK), :]) + bsh
            y_ref[rows, cols] = (u[rows, cols] * s).astype(BF16)


def _gmlp_p(h, w_in, g_v, ws_tril, bs_tile, l):
    ucol0 = 4 * W // CB
    return pl.pallas_call(
        _gmlp_p_kernel,
        out_shape=jax.ShapeDtypeStruct((NP, W), BF16),
        grid=(BATCH, W // CB),
        in_specs=[
            pl.BlockSpec((SEQ, D), lambda b, j: (b, 0)),
            pl.BlockSpec((None, D, W), lambda b, j: (l, 0, 5)),
            pl.BlockSpec((None, D, CB), lambda b, j: (l, 0, ucol0 + j)),
            pl.BlockSpec((None, 1, W), lambda b, j: (l, 0, 0)),
            pl.BlockSpec((None, HEADS, CHUNK, CHUNK), lambda b, j: (l, 0, 0, 0)),
            pl.BlockSpec((None, HEADS, CHUNK, CHUNK), lambda b, j: (l, 0, 0, 0)),
        ],
        out_specs=pl.BlockSpec((SEQ, CB), lambda b, j: (b, j)),
        scratch_shapes=[pltpu.VMEM((HEADS, SEQ, CHUNK), BF16)],
        compiler_params=_params(("arbitrary", "arbitrary")),
        name="gmlp_prompt",
    )(h, w_in, w_in, g_v, ws_tril, bs_tile)


def _mix_s_kernel(h_ref, wb_ref, wc_ref, wh_ref, wp_ref, cw_ref, pc_ref, pp_ref,
                  pw_ref, ps_ref, ya_ref, yb_ref, cst_ref, pst_ref):
    g = pl.program_id(0)
    h = h_ref[...]
    nb = DEC_BATCH
    z = _dot(h, wc_ref[...]) * _dot(h, wh_ref[...])
    bg = _dot(h, wb_ref[...])
    zs = [pc_ref[0], pc_ref[1]] + [z[t * nb:(t + 1) * nb] for t in range(DEC_SEQ)]
    cw = cw_ref[...]
    for t in range(DEC_SEQ):
        y = cw[0:1] * zs[t] + cw[1:2] * zs[t + 1] + cw[2:3] * zs[t + 2]
        ya_ref[t * nb:(t + 1) * nb, :] = (bg[t * nb:(t + 1) * nb] * y).astype(BF16)
    cst_ref[0] = zs[DEC_SEQ]
    cst_ref[1] = zs[DEC_SEQ + 1]
    p = _dot(h, wp_ref[...])
    pp = [pp_ref[k] for k in range(POOL_STATE)] + [p[t * nb:(t + 1) * nb] for t in range(DEC_SEQ)]
    window = jnp.left_shift(2, g).astype(F32)
    ds = []
    for t in range(DEC_SEQ):
        e = POOL_STATE + t
        s = pp[e] + pp[e - 1]
        s4 = s + (pp[e - 2] + pp[e - 3])
        s8 = s4 + ((pp[e - 4] + pp[e - 5]) + (pp[e - 6] + pp[e - 7]))
        s16 = s8 + (((pp[e - 8] + pp[e - 9]) + (pp[e - 10] + pp[e - 11]))
                    + ((pp[e - 12] + pp[e - 13]) + (pp[e - 14] + pp[e - 15])))
        s = jnp.where(g >= 1, s4, s)
        s = jnp.where(g >= 2, s8, s)
        s = jnp.where(g >= 3, s16, s)
        ds.append(s / window - pp[e])
    d = jnp.concatenate(ds, axis=0)
    yb_ref[...] = (_dot(d.astype(BF16), pw_ref[...]) * ps_ref[...]).astype(BF16)
    for k in range(POOL_STATE):
        pst_ref[k] = pp[DEC_SEQ + k]


def _mix_s(h, w_in, conv_w, pool_w, pool_scale, conv_t, pool_t, l):
    nc = W // CB
    srow = NP // NS
    return pl.pallas_call(
        _mix_s_kernel,
        out_shape=(jax.ShapeDtypeStruct((NS, W), BF16),
                   jax.ShapeDtypeStruct((NS, W), BF16),
                   jax.ShapeDtypeStruct((CONV_K - 1, DEC_BATCH, W), F32),
                   jax.ShapeDtypeStruct((POOL_STATE, DEC_BATCH, W), F32)),
        grid=(nc,),
        in_specs=[
            pl.BlockSpec((NS, D), lambda c: (srow, 0)),
            pl.BlockSpec((None, D, CB), lambda c: (l, 0, c)),
            pl.BlockSpec((None, D, CB), lambda c: (l, 0, nc + c)),
            pl.BlockSpec((None, D, CB), lambda c: (l, 0, 2 * nc + c)),
            pl.BlockSpec((None, D, CB), lambda c: (l, 0, 3 * nc + c)),
            pl.BlockSpec((None, CONV_K, CB), lambda c: (l, 0, c)),
            pl.BlockSpec((None, CONV_K - 1, DEC_BATCH, CB), lambda c: (l, 0, 0, c)),
            pl.BlockSpec((None, POOL_STATE, DEC_BATCH, CB), lambda c: (l, 0, 0, c)),
            pl.BlockSpec((None, None, POOL_GROUP, POOL_GROUP), lambda c: (l, c, 0, 0)),
            pl.BlockSpec((None, 1, CB), lambda c: (l, 0, c)),
        ],
        out_specs=(pl.BlockSpec((NS, CB), lambda c: (0, c)),
                   pl.BlockSpec((NS, CB), lambda c: (0, c)),
                   pl.BlockSpec((CONV_K - 1, DEC_BATCH, CB), lambda c: (0, 0, c)),
                   pl.BlockSpec((POOL_STATE, DEC_BATCH, CB), lambda c: (0, 0, c))),
        compiler_params=_params(("arbitrary",)),
        name="mix_sample",
    )(h, w_in, w_in, w_in, w_in, conv_w, conv_t, pool_t, pool_w, pool_scale)


def _gmlp_s_kernel(h_ref, wu_ref, wv_ref, gv_ref, wvec_ref, bvec_ref, yc_ref, vn_ref):
    h = h_ref[...]
    nb = DEC_BATCH
    v = _dot(h, wv_ref[...])
    vn = (v * _rms(v)) * gv_ref[...]
    vn_ref[...] = vn
    u = _dot(h, wu_ref[...])
    for t in range(DEC_SEQ):
        s = bvec_ref[t:t + 1, :]
        for sp in range(t + 1):
            k = t * DEC_SEQ + sp
            s = s + wvec_ref[k:k + 1, :] * vn[sp * nb:(sp + 1) * nb]
        yc_ref[t * nb:(t + 1) * nb, :] = (u[t * nb:(t + 1) * nb] * s).astype(BF16)


def _gmlp_s(h, w_in, g_v, wvec, bvec, l):
    srow = NP // NS
    return pl.pallas_call(
        _gmlp_s_kernel,
        out_shape=(jax.ShapeDtypeStruct((NS, W), BF16),
                   jax.ShapeDtypeStruct((NS, W), F32)),
        grid=(1,),
        in_specs=[
            pl.BlockSpec((NS, D), lambda i: (srow, 0)),
            pl.BlockSpec((None, D, W), lambda i: (l, 0, 4)),
            pl.BlockSpec((None, D, W), lambda i: (l, 0, 5)),
            pl.BlockSpec((None, 1, W), lambda i: (l, 0, 0)),
            pl.BlockSpec((None, DEC_SEQ * DEC_SEQ, W), lambda i: (l, 0, 0)),
            pl.BlockSpec((None, DEC_SEQ, W), lambda i: (l, 0, 0)),
        ],
        out_specs=(pl.BlockSpec((NS, W), lambda i: (0, 0)),
                   pl.BlockSpec((NS, W), lambda i: (0, 0))),
        compiler_params=_params(("arbitrary",)),
        name="gmlp_sample",
    )(h, w_in, w_in, g_v, wvec, bvec)


def _merge_kernel(h_ref, yap, ybp, ycp, yas, ybs, ycs, wg0, wg1, wg2, wbr_ref, o_ref):
    is_sample = pl.program_id(0) >= NP // TM
    h = h_ref[...]
    acc = None
    for n, (yp, ys, wg) in enumerate(((yap, yas, wg0), (ybp, ybs, wg1), (ycp, ycs, wg2))):
        y = jnp.where(is_sample, ys[...], yp[...])
        term = _sigmoid(_dot(h, wg[...])) * _dot(y, wbr_ref[n])
        acc = term if acc is None else acc + term
    o_ref[...] = acc.astype(BF16)


def _merge(h, y_prompt, y_sample, w_in, w_br, l):
    db = 512
    g0 = 6 * W // db
    gstep = D // db
    return pl.pallas_call(
        _merge_kernel,
        out_shape=jax.ShapeDtypeStruct((NT, D), BF16),
        grid=(N_TILES, D // db),
        in_specs=[
            pl.BlockSpec((TM, D), lambda i, d: (i, 0)),
            pl.BlockSpec((TM, W), lambda i, d: (jnp.minimum(i, NP // TM - 1), 0)),
            pl.BlockSpec((TM, W), lambda i, d: (jnp.minimum(i, NP // TM - 1), 0)),
            pl.BlockSpec((TM, W), lambda i, d: (jnp.minimum(i, NP // TM - 1), 0)),
            pl.BlockSpec((NS, W), lambda i, d: (0, 0)),
            pl.BlockSpec((NS, W), lambda i, d: (0, 0)),
            pl.BlockSpec((NS, W), lambda i, d: (0, 0)),
            pl.BlockSpec((None, D, db), lambda i, d: (l, 0, g0 + d)),
            pl.BlockSpec((None, D, db), lambda i, d: (l, 0, g0 + gstep + d)),
            pl.BlockSpec((None, D, db), lambda i, d: (l, 0, g0 + 2 * gstep + d)),
            pl.BlockSpec((None, 3, W, db), lambda i, d: (l, 0, 0, d)),
        ],
        out_specs=pl.BlockSpec((TM, db), lambda i, d: (i, d)),
        compiler_params=_params(("arbitrary", "arbitrary")),
        name="merge",
    )(h, *y_prompt, *y_sample, w_in, w_in, w_in, w_br)


def _route(logits):
    lane = lax.broadcasted_iota(jnp.int32, logits.shape, 1).astype(F32)
    neg = -jnp.inf
    big = float(LANES)
    is_grp = lane < N_GROUPS_MOE
    gl = jnp.where(is_grp, logits, neg)
    gmax = jnp.max(gl, axis=-1, keepdims=True)
    gsel = jnp.min(jnp.where(gl == gmax, lane, big), axis=-1, keepdims=True)
    gp = 1.0 / jnp.sum(jnp.where(is_grp, jnp.exp(logits - gmax), 0.0), axis=-1, keepdims=True)
    lo = N_GROUPS_MOE + gsel * EXP_PER_GROUP
    in_grp = (lane >= lo) & (lane < lo + EXP_PER_GROUP)
    el = jnp.where(in_grp, logits, neg)
    m1 = jnp.max(el, axis=-1, keepdims=True)
    i1 = jnp.min(jnp.where(el == m1, lane, big), axis=-1, keepdims=True)
    el2 = jnp.where(lane == i1, neg, el)
    m2 = jnp.max(el2, axis=-1, keepdims=True)
    i2 = jnp.min(jnp.where(el2 == m2, lane, big), axis=-1, keepdims=True)
    e = jnp.exp(m2 - m1)
    w1 = gp / (1.0 + e)
    w2 = gp * (e / (1.0 + e))
    rw = jnp.where(lane == 0.0, w1, jnp.where(lane == 1.0, w2, 0.0))
    return rw, lane, i1, i2


def _wo_kernel(m_ref, wo_ref, xp_ref, xs_ref, g_ref, gtp, gts, scp, scs, shp, shs, wrc_ref, wrh_ref, br_ref,
               x1_ref, h2_ref, rw_ref, re_ref, cnt_ref, carry):
    i = pl.program_id(0)
    x = jnp.where(i >= NP // TW, xs_ref[...], xp_ref[...])
    gt = _mod_rows(i, TW, gtp, gts)
    x1 = x + gt * _dot(m_ref[...], wo_ref[...])
    x1_ref[...] = x1
    sc = _mod_rows(i, TW, scp, scs)
    sh = _mod_rows(i, TW, shp, shs)
    h2 = ((x1 * _rms(x1)) * g_ref[...]) * (1.0 + sc) + sh
    h2_ref[...] = h2
    hi = h2.astype(BF16)
    lo = (h2 - hi.astype(F32)).astype(BF16)
    both = _dot(hi, wrc_ref[...])
    logits = (both[:, :LANES] + (both[:, LANES:] + _dot(lo, wrh_ref[...]))) + br_ref[...]
    rw, lane, i1, i2 = _route(logits)
    rw_ref[...] = rw

    @pl.when(i == 0)
    def _():
        carry[...] = jnp.zeros_like(carry)

    o1 = (lane == i1).astype(F32)
    o2 = (lane == i2).astype(F32)
    both_hot = o1 + o2
    r = lax.broadcasted_iota(jnp.int32, (TW, TW), 0)
    c = lax.broadcasted_iota(jnp.int32, (TW, TW), 1)
    before = (r > c).astype(BF16)
    seen = _dot(before, both_hot.astype(BF16)) + carry[...]
    rank1 = jnp.sum(seen * o1, axis=-1, keepdims=True)
    rank2 = jnp.sum(seen * o2, axis=-1, keepdims=True)
    total = carry[...] + jnp.sum(both_hot, axis=0, keepdims=True)
    carry[...] = total
    cnt_ref[...] = jnp.broadcast_to(total, cnt_ref.shape).astype(jnp.int32)
    re = jnp.where(lane == 0.0, i1 - N_GROUPS_MOE,
                   jnp.where(lane == 1.0, i2 - N_GROUPS_MOE,
                             jnp.where(lane == 2.0, rank1, jnp.where(lane == 3.0, rank2, 0.0))))
    re_ref[...] = re.astype(jnp.int32)


def _wo(merged, w_o, x_prompt_rows, x_sample_rows, sample_block0, g2, mod, wr_cat, wr_hi, b_r, l):
    row = lambda i: (i, 0)
    n_ptiles = NP // TW
    return pl.pallas_call(
        _wo_kernel,
        out_shape=(jax.ShapeDtypeStruct((NT, D), F32),
                   jax.ShapeDtypeStruct((NT, D), F32),
                   jax.ShapeDtypeStruct((NT, LANES), F32),
                   jax.ShapeDtypeStruct((NT, LANES), jnp.int32),
                   jax.ShapeDtypeStruct((8, LANES), jnp.int32)),
        grid=(NT // TW,),
        in_specs=[
            pl.BlockSpec((TW, D), row),
            pl.BlockSpec((None, D, D), lambda i: (l, 0, 0)),
            pl.BlockSpec((TW, D), lambda i: (jnp.minimum(i, n_ptiles - 1), 0)),
            pl.BlockSpec((TW, D), lambda i: (sample_block0 + jnp.maximum(i - n_ptiles, 0), 0)),
            pl.BlockSpec((None, 1, D), lambda i: (l, 0, 0)),
        ] + _mod_specs(l, 2, TW) + _mod_specs(l, 4, TW) + _mod_specs(l, 3, TW) + [
            pl.BlockSpec((None, D, 2 * LANES), lambda i: (l, 0, 0)),
            pl.BlockSpec((None, D, LANES), lambda i: (l, 0, 0)),
            pl.BlockSpec((None, 1, LANES), lambda i: (l, 0, 0)),
        ],
        out_specs=(pl.BlockSpec((TW, D), row), pl.BlockSpec((TW, D), row),
                   pl.BlockSpec((TW, LANES), row), pl.BlockSpec((TW, LANES), row),
                   pl.BlockSpec((8, LANES), lambda i: (0, 0))),
        scratch_shapes=[pltpu.VMEM((1, LANES), F32)],
        compiler_params=_params(("arbitrary",)),
        name="wo_router",
    )(merged, w_o, x_prompt_rows, x_sample_rows, g2, mod, mod, mod, mod, mod, mod, wr_cat, wr_hi, b_r)


def _row_gather_copy(src_hbm, idx, dst, sem):
    return pltpu.make_async_copy(src_hbm.at[pl.ds(idx, 1)], dst, sem)


def _moe_kernel(te_ref, tf_ref, nu_ref, dest_ref, h2_hbm, wg_ref, wu_ref, wd_ref, o_ref,
                src_ref, xbuf, sem, wg_s, wu_s, wd_s):
    del te_ref
    t = pl.program_id(0)
    n_used = nu_ref[0]

    def issue(tile, slot):
        base = tile * TE

        def body(r, carry):
            _row_gather_copy(h2_hbm, src_ref[base + r], xbuf.at[slot, pl.ds(r, 1)], sem.at[slot]).start()
            return carry

        lax.fori_loop(0, TE, body, 0, unroll=8)

    @pl.when(t == 0)
    def _():
        def clear(p, carry):
            src_ref[p] = 0
            return carry

        lax.fori_loop(0, P_ROWS, clear, 0, unroll=16)

        def invert(q, carry):
            src_ref[dest_ref[q]] = lax.shift_right_logical(q, 1)
            return carry

        lax.fori_loop(0, NT * TOP_K, invert, 0, unroll=16)
        issue(0, 0)

    @pl.when((t < n_used) & (tf_ref[t] == 1))
    def _():
        wg_s[...] = wg_ref[...].astype(BF16)
        wu_s[...] = wu_ref[...].astype(BF16)
        wd_s[...] = wd_ref[...].astype(BF16)

    def wait_tile(slot):
        pltpu.make_async_copy(h2_hbm.at[pl.ds(0, TE)], xbuf.at[slot], sem.at[slot]).wait()

    @pl.when(t < n_used)
    def _():
        slot = t % 2
        wait_tile(slot)
        x = xbuf[slot].astype(BF16)
        base = jnp.minimum(t + 1, n_used - 1) * TE
        for r in range(TE):
            _row_gather_copy(h2_hbm, src_ref[base + r], xbuf.at[1 - slot, pl.ds(r, 1)],
                             sem.at[1 - slot]).start()
        a = _dot(x, wg_s[...])
        b = _dot(x, wu_s[...])
        hid = (a * _sigmoid(a)) * b
        o_ref[...] = _dot(hid.astype(BF16), wd_s[...])

    @pl.when(t == n_used - 1)
    def _():
        wait_tile(1 - t % 2)

    @pl.when(t >= n_used)
    def _():
        o_ref[...] = jnp.zeros_like(o_ref)


def _moe(h2, tile_expert, tile_first, n_used, dest, w_gate, w_up, w_down, l):
    grid_spec = pltpu.PrefetchScalarGridSpec(
        num_scalar_prefetch=4,
        grid=(MAX_TILES,),
        in_specs=[
            pl.BlockSpec(memory_space=pl.ANY),
            pl.BlockSpec((None, None, D, D_FF), lambda t, te, tf, nu, dst: (l, te[t], 0, 0)),
            pl.BlockSpec((None, None, D, D_FF), lambda t, te, tf, nu, dst: (l, te[t], 0, 0)),
            pl.BlockSpec((None, None, D_FF, D), lambda t, te, tf, nu, dst: (l, te[t], 0, 0)),
        ],
        out_specs=pl.BlockSpec((TE, D), lambda t, te, tf, nu, dst: (t, 0)),
        scratch_shapes=[
            pltpu.SMEM((P_ROWS,), jnp.int32),
            pltpu.VMEM((2, TE, D), F32),
            pltpu.SemaphoreType.DMA((2,)),
            pltpu.VMEM((D, D_FF), BF16),
            pltpu.VMEM((D, D_FF), BF16),
            pltpu.VMEM((D_FF, D), BF16),
        ],
    )
    return pl.pallas_call(
        _moe_kernel,
        out_shape=jax.ShapeDtypeStruct((P_ROWS, D), F32),
        grid_spec=grid_spec,
        compiler_params=_params(("arbitrary",)),
        name="moe_experts",
    )(tile_expert, tile_first, n_used, dest, h2, w_gate, w_up, w_down)


def _dispatch(route_e, counts_lanes):
    counts = counts_lanes[0, N_GROUPS_MOE:N_GROUPS_MOE + N_EXPERTS]
    tiles = (counts + TE - 1) // TE
    eids = jnp.arange(N_EXPERTS, dtype=jnp.int32)
    tile_end = jnp.sum(jnp.where(eids[None, :] <= eids[:, None], tiles[None, :], 0), axis=1)
    tile_start = tile_end - tiles
    hot = route_e[:, :TOP_K, None] == eids[None, None, :]
    row0 = jnp.sum(jnp.where(hot, tile_start[None, None, :] * TE, 0), axis=-1)
    dest = (row0 + route_e[:, TOP_K:2 * TOP_K]).reshape(-1).astype(jnp.int32)
    tids = jnp.arange(MAX_TILES, dtype=jnp.int32)
    tile_expert = jnp.minimum(jnp.sum((tile_end[None, :] <= tids[:, None]).astype(jnp.int32), axis=1),
                              N_EXPERTS - 1)
    first_tile = jnp.sum(jnp.where(tile_expert[:, None] == eids[None, :], tile_start[None, :], 0), axis=1)
    tile_first = (tids == first_tile).astype(jnp.int32)
    n_used = tile_end[-1:].astype(jnp.int32)
    return dest, tile_expert, tile_first, n_used


def _combine_kernel(final, tile0, dest_ref, ys_hbm, x_ref, rw_ref, g_ref, gtp, gts, *rest):
    if final:
        y_ref, ybuf, sem = rest
    else:
        scp, scs, shp, shs, x2_ref, h_ref, ybuf, sem = rest
    step = pl.program_id(0)
    nsteps = pl.num_programs(0)
    i = step + tile0

    def issue(tile, slot, r):
        for k in range(TOP_K):
            _row_gather_copy(ys_hbm, dest_ref[tile * (TK * TOP_K) + TOP_K * r + k],
                             ybuf.at[slot, k, pl.ds(r, 1)], sem.at[slot]).start()

    def wait_tile(slot):
        for k in range(TOP_K):
            pltpu.make_async_copy(ys_hbm.at[pl.ds(0, TK)], ybuf.at[slot, k], sem.at[slot]).wait()

    @pl.when(step == 0)
    def _():
        def body(r, carry):
            issue(i, 0, r)
            return carry

        lax.fori_loop(0, TK, body, 0, unroll=8)

    slot = step % 2
    wait_tile(slot)
    nxt = jnp.minimum(i + 1, tile0 + nsteps - 1)
    for r in range(TK):
        issue(nxt, 1 - slot, r)
    gt = _mod_rows(i, TK, gtp, gts)
    rw = rw_ref[...]
    moe = rw[:, 0:1] * ybuf[slot, 0] + rw[:, 1:2] * ybuf[slot, 1]
    x2 = x_ref[...] + gt * moe
    xn = (x2 * _rms(x2)) * g_ref[...]
    if final:
        y_ref[...] = xn
    else:
        x2_ref[...] = x2
        sc = _mod_rows(i, TK, scp, scs)
        sh = _mod_rows(i, TK, shp, shs)
        h_ref[...] = (xn * (1.0 + sc) + sh).astype(BF16)

    @pl.when(step == nsteps - 1)
    def _():
        wait_tile(1 - slot)


def _combine(dest, ys, x1, route_w, g, mod, l, final, tile0=0, ntiles=NT // TK):
    row_in = lambda s, d: (s + tile0, 0)
    row_out = lambda s, d: (s, 0)
    nmod = lambda q, ll: [
        pl.BlockSpec((None, 8, D), lambda s, d: (ll, MOD_PROMPT_BLOCK, q)),
        pl.BlockSpec((None, DEC_BATCH, D), lambda s, d: (ll, 0, q)),
    ]
    in_specs = [pl.BlockSpec(memory_space=pl.ANY), pl.BlockSpec((TK, D), row_in),
                pl.BlockSpec((TK, LANES), row_in),
                pl.BlockSpec((1, D), lambda s, d: (0, 0))] + nmod(5, l)
    args = [ys, x1, route_w, g, mod, mod]
    rows = ntiles * TK
    if final:
        out_shape = jax.ShapeDtypeStruct((rows, D), F32)
        out_specs = pl.BlockSpec((TK, D), row_out)
    else:
        in_specs += nmod(1, l + 1) + nmod(0, l + 1)
        args += [mod, mod, mod, mod]
        out_shape = (jax.ShapeDtypeStruct((rows, D), F32), jax.ShapeDtypeStruct((rows, D), BF16))
        out_specs = (pl.BlockSpec((TK, D), row_out), pl.BlockSpec((TK, D), row_out))
    grid_spec = pltpu.PrefetchScalarGridSpec(
        num_scalar_prefetch=1,
        grid=(ntiles,),
        in_specs=in_specs,
        out_specs=out_specs,
        scratch_shapes=[pltpu.VMEM((2, TOP_K, TK, D), F32), pltpu.SemaphoreType.DMA((2,))],
    )
    return pl.pallas_call(
        functools.partial(_combine_kernel, final, tile0),
        out_shape=out_shape,
        grid_spec=grid_spec,
        compiler_params=_params(("arbitrary",)),
        name=("combine_final_%d" % tile0) if final else "combine",
    )(dest, *args)


def kernel(x_prompt, x_sample, c_prompt, c_sample, state_conv, state_pool, w_ada, b_ada, g_norm1, w_in,
           conv_w, pool_w, pool_scale, g_v, w_s, b_s, w_br, w_o, g_norm2, w_rg, b_rg, w_re, b_re,
           w_gate, w_up, w_down, g_final):
    xp_rows = x_prompt.reshape(NP, D)
    xs_rows = x_sample.transpose(1, 0, 2).reshape(NS, D)
    sample_block0 = 0
    c_all = jnp.concatenate([c_sample, c_prompt,
                             jnp.zeros((MOD_ROWS - DEC_BATCH - BATCH, D), F32)], axis=0).astype(BF16)
    mod = _ada(c_all, w_ada, b_ada)

    w_in_b = w_in.astype(BF16)
    w_br_b = w_br.astype(BF16)
    w_o_b = w_o.astype(BF16)
    pool_w_b = pool_w.astype(BF16)
    tril = jnp.tril(jnp.ones((CHUNK, CHUNK), dtype=bool))
    ws_tril = jnp.where(tril[None, None], w_s, 0.0).astype(BF16)
    bs_tile = jnp.broadcast_to(b_s[:, :, :, None], (DEPTH, HEADS, CHUNK, CHUNK))
    small = jnp.tril(jnp.ones((DEC_SEQ, DEC_SEQ), dtype=bool))
    ws_small = jnp.where(small[None, None], w_s[:, :, :DEC_SEQ, :DEC_SEQ], 0.0)
    wvec = jnp.repeat(ws_small.transpose(0, 2, 3, 1).reshape(DEPTH, DEC_SEQ * DEC_SEQ, HEADS), CHUNK, axis=-1)
    bvec = jnp.repeat(b_s[:, :, :DEC_SEQ].transpose(0, 2, 1), CHUNK, axis=-1)
    w_r = jnp.concatenate([w_rg, w_re, jnp.zeros((DEPTH, D, LANES - N_GROUPS_MOE - N_EXPERTS), F32)], axis=-1)
    wr_hi = w_r.astype(BF16)
    wr_lo = (w_r - wr_hi.astype(F32)).astype(BF16)
    wr_cat = jnp.concatenate([wr_hi, wr_lo], axis=-1)
    b_r = jnp.concatenate([b_rg, b_re, jnp.zeros((DEPTH, LANES - N_GROUPS_MOE - N_EXPERTS), F32)],
                          axis=-1).reshape(DEPTH, 1, LANES)
    conv_t = state_conv.transpose(0, 2, 1, 3)
    pool_t = state_pool.transpose(0, 2, 1, 3)
    g1 = g_norm1.reshape(DEPTH, 1, D)
    g2 = g_norm2.reshape(DEPTH, 1, D)
    gv = g_v.reshape(DEPTH, 1, W)
    pscale = pool_scale.reshape(DEPTH, 1, W)

    h = _norm(xp_rows, xs_rows, g1, mod, 0)
    conv_p, pool_p, conv_s, pool_s, v_s = [], [], [], [], []
    y_prompt = y_sample = None
    for l in range(DEPTH):
        ya, cst_p = _conv_p(h, w_in_b, conv_w, l)
        yb, pst_p = _pool_p(h, w_in_b, pool_w_b, pscale, l)
        yc = _gmlp_p(h, w_in_b, gv, ws_tril, bs_tile, l)
        ya_s, yb_s, cst_s, pst_s = _mix_s(h, w_in_b, conv_w, pool_w_b, pscale, conv_t, pool_t, l)
        yc_s, vn_s = _gmlp_s(h, w_in_b, gv, wvec, bvec, l)
        merged = _merge(h, (ya, yb, yc), (ya_s, yb_s, yc_s), w_in_b, w_br_b, l)
        x1, h2, route_w, route_e, counts = _wo(merged, w_o_b, xp_rows, xs_rows, sample_block0, g2, mod,
                                               wr_cat, wr_hi, b_r, l)
        dest, tile_expert, tile_first, n_used = _dispatch(route_e, counts)
        ys = _moe(h2, tile_expert, tile_first, n_used, dest, w_gate, w_up, w_down, l)
        if l + 1 < DEPTH:
            x, h = _combine(dest, ys, x1, route_w, g1[l + 1], mod, l, final=False)
            xp_rows = xs_rows = x
            sample_block0 = NP // TW
        else:
            gf = g_final.reshape(1, D)
            y_prompt = _combine(dest, ys, x1, route_w, gf, mod, l, final=True, tile0=0, ntiles=NP // TK)
            y_sample = _combine(dest, ys, x1, route_w, gf, mod, l, final=True, tile0=NP // TK,
                                ntiles=NS // TK)
        conv_p.append(cst_p[:, 8 - (CONV_K - 1):, :])
        pool_p.append(pst_p[:, 16 - POOL_STATE:, :])
        conv_s.append(cst_s.transpose(1, 0, 2))
        pool_s.append(pst_s.transpose(1, 0, 2))
        v_s.append(vn_s.reshape(DEC_SEQ, DEC_BATCH, W).transpose(1, 0, 2))

    y_prompt = y_prompt.reshape(BATCH, SEQ, D)
    y_sample = y_sample.reshape(DEC_SEQ, DEC_BATCH, D).transpose(1, 0, 2)
    return (y_prompt, y_sample, jnp.stack(conv_p), jnp.stack(pool_p), jnp.stack(conv_s),
            jnp.stack(pool_s), jnp.stack(v_s))
```

```python
import functools

import jax
import jax.numpy as jnp
from jax import lax
from jax.experimental import pallas as pl
from jax.experimental.pallas import tpu as pltpu

F32 = jnp.float32
BF16 = jnp.bfloat16

D = 2048
BATCH = 4
SEQ = 2048
DEPTH = 2
DEC_BATCH = 128
DEC_SEQ = 4
W = D // 2
CONV_K = 3
POOL_STATE = 15
N_POOL_GROUPS = 4
POOL_GROUP = W // N_POOL_GROUPS
CHUNK = 128
HEADS = W // CHUNK
IN_COLS = 6 * W + 3 * D
N_GROUPS_MOE = 4
EXP_PER_GROUP = 4
N_EXPERTS = 16
TOP_K = 2
D_FF = D // 4
EPS = 1e-6

NP = BATCH * SEQ
NS = DEC_BATCH * DEC_SEQ
NT = NP + NS
TM = 512
N_TILES = NT // TM
CB = 256
TE = 256
MAX_TILES = (NT * TOP_K) // TE + N_EXPERTS
P_ROWS = MAX_TILES * TE
TK = 256
TW = 256
GATHER_SLOTS = 3
LANES = 128
MOD_ROWS = 144
MOD_PROMPT_BLOCK = DEC_BATCH // 8

VMEM_LIMIT_V7X = 56 * 1024 * 1024


def _dot(a, b):
    return jnp.dot(a, b, preferred_element_type=F32)


def _params(sem, vmem=VMEM_LIMIT_V7X):
    return pltpu.CompilerParams(dimension_semantics=sem, vmem_limit_bytes=vmem)


def _sigmoid(x):
    return 1.0 / (1.0 + jnp.exp(-x))


def _rms(x):
    return lax.rsqrt(jnp.mean(x * x, axis=-1, keepdims=True) + EPS)


def _ada_kernel(c_ref, w_ref, b_ref, o_ref):
    o_ref[...] = _dot(c_ref[...], w_ref[...].astype(BF16)) + b_ref[...]


def _ada(c_all, w_ada, b_ada):
    nb = 1024
    return pl.pallas_call(
        _ada_kernel,
        out_shape=jax.ShapeDtypeStruct((DEPTH, MOD_ROWS, 6 * D), F32),
        grid=(DEPTH, 6 * D // nb),
        in_specs=[
            pl.BlockSpec((MOD_ROWS, D), lambda l, j: (0, 0)),
            pl.BlockSpec((None, D, nb), lambda l, j: (l, 0, j)),
            pl.BlockSpec((None, 1, nb), lambda l, j: (l, 0, j)),
        ],
        out_specs=pl.BlockSpec((None, MOD_ROWS, nb), lambda l, j: (l, 0, j)),
        compiler_params=_params(("arbitrary", "arbitrary")),
        name="ada",
    )(c_all, w_ada, b_ada.reshape(DEPTH, 1, 6 * D))


def _mod_specs(l, q, tile_rows):
    del tile_rows
    return [
        pl.BlockSpec((None, 8, D), lambda i, *_: (l, MOD_PROMPT_BLOCK, q)),
        pl.BlockSpec((None, DEC_BATCH, D), lambda i, *_: (l, 0, q)),
    ]


def _mod_rows(i, tile_rows, mp_ref, ms_ref):
    tiles_per_seq = SEQ // tile_rows
    is_sample = i >= NP // tile_rows
    b = jnp.minimum(i // tiles_per_seq, BATCH - 1)
    mp = mp_ref[pl.ds(b, 1), :]
    ms = jnp.concatenate([ms_ref[...]] * (tile_rows // DEC_BATCH), axis=0)
    return jnp.where(is_sample, ms, mp)


def _norm_kernel(xp_ref, xs_ref, g_ref, scp, scs, shp, shs, h_ref):
    i = pl.program_id(0)
    x = jnp.where(i >= NP // TM, xs_ref[...], xp_ref[...])
    sc = _mod_rows(i, TM, scp, scs)
    sh = _mod_rows(i, TM, shp, shs)
    h_ref[...] = (((x * _rms(x)) * g_ref[...]) * (1.0 + sc) + sh).astype(BF16)


def _norm(x_prompt_rows, x_sample_rows, g, mod, l):
    n_ptiles = NP // TM
    return pl.pallas_call(
        _norm_kernel,
        out_shape=jax.ShapeDtypeStruct((NT, D), BF16),
        grid=(N_TILES,),
        in_specs=[pl.BlockSpec((TM, D), lambda i: (jnp.minimum(i, n_ptiles - 1), 0)),
                  pl.BlockSpec((TM, D), lambda i: (jnp.maximum(i - n_ptiles, 0), 0)),
                  pl.BlockSpec((None, 1, D), lambda i: (l, 0, 0))]
                 + _mod_specs(l, 1, TM) + _mod_specs(l, 0, TM),
        out_specs=pl.BlockSpec((TM, D), lambda i: (i, 0)),
        compiler_params=_params(("arbitrary",)),
        name="norm1",
    )(x_prompt_rows, x_sample_rows, g, mod, mod, mod, mod)


def _shift_rows(a, s, row):
    return jnp.where(row >= s, pltpu.roll(a, s, 0), 0.0)


def _conv_p_kernel(h_ref, wb_ref, wc_ref, wh_ref, cw_ref, y_ref, st_ref):
    h = h_ref[...]
    z = _dot(h, wc_ref[...]) * _dot(h, wh_ref[...])
    row = lax.broadcasted_iota(jnp.int32, (SEQ, 1), 0)
    cw = cw_ref[...]
    y = cw[0:1] * _shift_rows(z, 2, row) + cw[1:2] * _shift_rows(z, 1, row) + cw[2:3] * z
    y_ref[...] = (_dot(h, wb_ref[...]) * y).astype(BF16)
    st_ref[...] = z[SEQ - 8:, :]


def _conv_p(h, w_in, conv_w, l):
    nc = W // CB
    return pl.pallas_call(
        _conv_p_kernel,
        out_shape=(jax.ShapeDtypeStruct((NP, W), BF16),
                   jax.ShapeDtypeStruct((BATCH, 8, W), F32)),
        grid=(BATCH, nc),
        in_specs=[
            pl.BlockSpec((SEQ, D), lambda b, c: (b, 0)),
            pl.BlockSpec((None, D, CB), lambda b, c: (l, 0, c)),
            pl.BlockSpec((None, D, CB), lambda b, c: (l, 0, nc + c)),
            pl.BlockSpec((None, D, CB), lambda b, c: (l, 0, 2 * nc + c)),
            pl.BlockSpec((None, CONV_K, CB), lambda b, c: (l, 0, c)),
        ],
        out_specs=(pl.BlockSpec((SEQ, CB), lambda b, c: (b, c)),
                   pl.BlockSpec((None, 8, CB), lambda b, c: (b, 0, c))),
        compiler_params=_params(("arbitrary", "arbitrary")),
        name="conv_prompt",
    )(h, w_in, w_in, w_in, conv_w)


def _pool_p_kernel(h_ref, wp_ref, pw_ref, ps_ref, y_ref, st_ref):
    g = pl.program_id(1)
    p = _dot(h_ref[...], wp_ref[...])
    row = lax.broadcasted_iota(jnp.int32, (SEQ, 1), 0)
    s = p + _shift_rows(p, 1, row)
    s = jnp.where(g >= 1, s + _shift_rows(s, 2, row), s)
    s = jnp.where(g >= 2, s + _shift_rows(s, 4, row), s)
    s = jnp.where(g >= 3, s + _shift_rows(s, 8, row), s)
    window = jnp.left_shift(2, g)
    cnt = jnp.minimum(row + 1, window).astype(F32)
    d = s / cnt - p
    y_ref[...] = (_dot(d.astype(BF16), pw_ref[...]) * ps_ref[...]).astype(BF16)
    st_ref[...] = p[SEQ - 16:, :]


def _pool_p(h, w_in, pool_w, pool_scale, l):
    col0 = 3 * W // CB
    return pl.pallas_call(
        _pool_p_kernel,
        out_shape=(jax.ShapeDtypeStruct((NP, W), BF16),
                   jax.ShapeDtypeStruct((BATCH, 16, W), F32)),
        grid=(BATCH, N_POOL_GROUPS),
        in_specs=[
            pl.BlockSpec((SEQ, D), lambda b, g: (b, 0)),
            pl.BlockSpec((None, D, CB), lambda b, g: (l, 0, col0 + g)),
            pl.BlockSpec((None, None, POOL_GROUP, POOL_GROUP), lambda b, g: (l, g, 0, 0)),
            pl.BlockSpec((None, 1, CB), lambda b, g: (l, 0, g)),
        ],
        out_specs=(pl.BlockSpec((SEQ, CB), lambda b, g: (b, g)),
                   pl.BlockSpec((None, 16, CB), lambda b, g: (b, 0, g))),
        compiler_params=_params(("arbitrary", "arbitrary")),
        name="pool_prompt",
    )(h, w_in, pool_w, pool_scale)


def _gmlp_p_kernel(h_ref, wv_ref, wu_ref, gv_ref, ws_ref, bs_ref, y_ref, vn_s):
    j = pl.program_id(1)

    @pl.when(j == 0)
    def _():
        for r0 in range(0, SEQ, TM):
            v = _dot(h_ref[r0:r0 + TM, :], wv_ref[...])
            vn = (v * _rms(v)) * gv_ref[...]
            for hd in range(HEADS):
                vn_s[hd, r0:r0 + TM, :] = vn[:, hd * CHUNK:(hd + 1) * CHUNK].astype(BF16)

    u = _dot(h_ref[...], wu_ref[...])
    heads_per_block = CB // CHUNK
    for hh in range(heads_per_block):
        head = heads_per_block * j + hh
        wsh = ws_ref[head]
        bsh = bs_ref[head]
        for n in range(SEQ // CHUNK):
            rows = slice(n * CHUNK, (n + 1) * CHUNK)
            cols = slice(hh * CHUNK, (hh + 1) * CHUNK)
            s = _dot(wsh, vn_s[head, pl.ds(n * CHUNK, CHUNK), :]) + bsh
            y_ref[rows, cols] = (u[rows, cols] * s).astype(BF16)


def _gmlp_p(h, w_in, g_v, ws_tril, bs_tile, l):
    ucol0 = 4 * W // CB
    return pl.pallas_call(
        _gmlp_p_kernel,
        out_shape=jax.ShapeDtypeStruct((NP, W), BF16),
        grid=(BATCH, W // CB),
        in_specs=[
            pl.BlockSpec((SEQ, D), lambda b, j: (b, 0)),
            pl.BlockSpec((None, D, W), lambda b, j: (l, 0, 5)),
            pl.BlockSpec((None, D, CB), lambda b, j: (l, 0, ucol0 + j)),
            pl.BlockSpec((None, 1, W), lambda b, j: (l, 0, 0)),
            pl.BlockSpec((None, HEADS, CHUNK, CHUNK), lambda b, j: (l, 0, 0, 0)),
            pl.BlockSpec((None, HEADS, CHUNK, CHUNK), lambda b, j: (l, 0, 0, 0)),
        ],
        out_specs=pl.BlockSpec((SEQ, CB), lambda b, j: (b, j)),
        scratch_shapes=[pltpu.VMEM((HEADS, SEQ, CHUNK), BF16)],
        compiler_params=_params(("arbitrary", "arbitrary")),
        name="gmlp_prompt",
    )(h, w_in, w_in, g_v, ws_tril, bs_tile)


def _mix_s_kernel(h_ref, wb_ref, wc_ref, wh_ref, wp_ref, cw_ref, pc_ref, pp_ref,
                  pw_ref, ps_ref, ya_ref, yb_ref, cst_ref, pst_ref):
    g = pl.program_id(0)
    h = h_ref[...]
    nb = DEC_BATCH
    z = _dot(h, wc_ref[...]) * _dot(h, wh_ref[...])
    bg = _dot(h, wb_ref[...])
    zs = [pc_ref[0], pc_ref[1]] + [z[t * nb:(t + 1) * nb] for t in range(DEC_SEQ)]
    cw = cw_ref[...]
    for t in range(DEC_SEQ):
        y = cw[0:1] * zs[t] + cw[1:2] * zs[t + 1] + cw[2:3] * zs[t + 2]
        ya_ref[t * nb:(t + 1) * nb, :] = (bg[t * nb:(t + 1) * nb] * y).astype(BF16)
    cst_ref[0] = zs[DEC_SEQ]
    cst_ref[1] = zs[DEC_SEQ + 1]
    p = _dot(h, wp_ref[...])
    pp = [pp_ref[k] for k in range(POOL_STATE)] + [p[t * nb:(t + 1) * nb] for t in range(DEC_SEQ)]
    window = jnp.left_shift(2, g).astype(F32)
    ds = []
    for t in range(DEC_SEQ):
        e = POOL_STATE + t
        s = pp[e] + pp[e - 1]
        s4 = s + (pp[e - 2] + pp[e - 3])
        s8 = s4 + ((pp[e - 4] + pp[e - 5]) + (pp[e - 6] + pp[e - 7]))
        s16 = s8 + (((pp[e - 8] + pp[e - 9]) + (pp[e - 10] + pp[e - 11]))
                    + ((pp[e - 12] + pp[e - 13]) + (pp[e - 14] + pp[e - 15])))
        s = jnp.where(g >= 1, s4, s)
        s = jnp.where(g >= 2, s8, s)
        s = jnp.where(g >= 3, s16, s)
        ds.append(s / window - pp[e])
    d = jnp.concatenate(ds, axis=0)
    yb_ref[...] = (_dot(d.astype(BF16), pw_ref[...]) * ps_ref[...]).astype(BF16)
    for k in range(POOL_STATE):
        pst_ref[k] = pp[DEC_SEQ + k]


def _mix_s(h, w_in, conv_w, pool_w, pool_scale, conv_t, pool_t, l):
    nc = W // CB
    srow = NP // NS
    return pl.pallas_call(
        _mix_s_kernel,
        out_shape=(jax.ShapeDtypeStruct((NS, W), BF16),
                   jax.ShapeDtypeStruct((NS, W), BF16),
                   jax.ShapeDtypeStruct((CONV_K - 1, DEC_BATCH, W), F32),
                   jax.ShapeDtypeStruct((POOL_STATE, DEC_BATCH, W), F32)),
        grid=(nc,),
        in_specs=[
            pl.BlockSpec((NS, D), lambda c: (srow, 0)),
            pl.BlockSpec((None, D, CB), lambda c: (l, 0, c)),
            pl.BlockSpec((None, D, CB), lambda c: (l, 0, nc + c)),
            pl.BlockSpec((None, D, CB), lambda c: (l, 0, 2 * nc + c)),
            pl.BlockSpec((None, D, CB), lambda c: (l, 0, 3 * nc + c)),
            pl.BlockSpec((None, CONV_K, CB), lambda c: (l, 0, c)),
            pl.BlockSpec((None, CONV_K - 1, DEC_BATCH, CB), lambda c: (l, 0, 0, c)),
            pl.BlockSpec((None, POOL_STATE, DEC_BATCH, CB), lambda c: (l, 0, 0, c)),
            pl.BlockSpec((None, None, POOL_GROUP, POOL_GROUP), lambda c: (l, c, 0, 0)),
            pl.BlockSpec((None, 1, CB), lambda c: (l, 0, c)),
        ],
        out_specs=(pl.BlockSpec((NS, CB), lambda c: (0, c)),
                   pl.BlockSpec((NS, CB), lambda c: (0, c)),
                   pl.BlockSpec((CONV_K - 1, DEC_BATCH, CB), lambda c: (0, 0, c)),
                   pl.BlockSpec((POOL_STATE, DEC_BATCH, CB), lambda c: (0, 0, c))),
        compiler_params=_params(("arbitrary",)),
        name="mix_sample",
    )(h, w_in, w_in, w_in, w_in, conv_w, conv_t, pool_t, pool_w, pool_scale)


def _gmlp_s_kernel(h_ref, wu_ref, wv_ref, gv_ref, wvec_ref, bvec_ref, yc_ref, vn_ref):
    h = h_ref[...]
    nb = DEC_BATCH
    v = _dot(h, wv_ref[...])
    vn = (v * _rms(v)) * gv_ref[...]
    vn_ref[...] = vn
    u = _dot(h, wu_ref[...])
    for t in range(DEC_SEQ):
        s = bvec_ref[t:t + 1, :]
        for sp in range(t + 1):
            k = t * DEC_SEQ + sp
            s = s + wvec_ref[k:k + 1, :] * vn[sp * nb:(sp + 1) * nb]
        yc_ref[t * nb:(t + 1) * nb, :] = (u[t * nb:(t + 1) * nb] * s).astype(BF16)


def _gmlp_s(h, w_in, g_v, wvec, bvec, l):
    srow = NP // NS
    return pl.pallas_call(
        _gmlp_s_kernel,
        out_shape=(jax.ShapeDtypeStruct((NS, W), BF16),
                   jax.ShapeDtypeStruct((NS, W), F32)),
        grid=(1,),
        in_specs=[
            pl.BlockSpec((NS, D), lambda i: (srow, 0)),
            pl.BlockSpec((None, D, W), lambda i: (l, 0, 4)),
            pl.BlockSpec((None, D, W), lambda i: (l, 0, 5)),
            pl.BlockSpec((None, 1, W), lambda i: (l, 0, 0)),
            pl.BlockSpec((None, DEC_SEQ * DEC_SEQ, W), lambda i: (l, 0, 0)),
            pl.BlockSpec((None, DEC_SEQ, W), lambda i: (l, 0, 0)),
        ],
        out_specs=(pl.BlockSpec((NS, W), lambda i: (0, 0)),
                   pl.BlockSpec((NS, W), lambda i: (0, 0))),
        compiler_params=_params(("arbitrary",)),
        name="gmlp_sample",
    )(h, w_in, w_in, g_v, wvec, bvec)


def _merge_kernel(h_ref, yap, ybp, ycp, yas, ybs, ycs, wg0, wg1, wg2, wbr_ref, o_ref):
    is_sample = pl.program_id(0) >= NP // TM
    h = h_ref[...]
    acc = None
    for n, (yp, ys, wg) in enumerate(((yap, yas, wg0), (ybp, ybs, wg1), (ycp, ycs, wg2))):
        y = jnp.where(is_sample, ys[...], yp[...])
        term = _sigmoid(_dot(h, wg[...])) * _dot(y, wbr_ref[n])
        acc = term if acc is None else acc + term
    o_ref[...] = acc.astype(BF16)


def _merge(h, y_prompt, y_sample, w_in, w_br, l):
    db = 512
    g0 = 6 * W // db
    gstep = D // db
    return pl.pallas_call(
        _merge_kernel,
        out_shape=jax.ShapeDtypeStruct((NT, D), BF16),
        grid=(N_TILES, D // db),
        in_specs=[
            pl.BlockSpec((TM, D), lambda i, d: (i, 0)),
            pl.BlockSpec((TM, W), lambda i, d: (jnp.minimum(i, NP // TM - 1), 0)),
            pl.BlockSpec((TM, W), lambda i, d: (jnp.minimum(i, NP // TM - 1), 0)),
            pl.BlockSpec((TM, W), lambda i, d: (jnp.minimum(i, NP // TM - 1), 0)),
            pl.BlockSpec((NS, W), lambda i, d: (0, 0)),
            pl.BlockSpec((NS, W), lambda i, d: (0, 0)),
            pl.BlockSpec((NS, W), lambda i, d: (0, 0)),
            pl.BlockSpec((None, D, db), lambda i, d: (l, 0, g0 + d)),
            pl.BlockSpec((None, D, db), lambda i, d: (l, 0, g0 + gstep + d)),
            pl.BlockSpec((None, D, db), lambda i, d: (l, 0, g0 + 2 * gstep + d)),
            pl.BlockSpec((None, 3, W, db), lambda i, d: (l, 0, 0, d)),
        ],
        out_specs=pl.BlockSpec((TM, db), lambda i, d: (i, d)),
        compiler_params=_params(("arbitrary", "arbitrary")),
        name="merge",
    )(h, *y_prompt, *y_sample, w_in, w_in, w_in, w_br)


def _route(logits):
    lane = lax.broadcasted_iota(jnp.int32, logits.shape, 1).astype(F32)
    neg = -jnp.inf
    big = float(LANES)
    is_grp = lane < N_GROUPS_MOE
    gl = jnp.where(is_grp, logits, neg)
    gmax = jnp.max(gl, axis=-1, keepdims=True)
    gsel = jnp.min(jnp.where(gl == gmax, lane, big), axis=-1, keepdims=True)
    gp = 1.0 / jnp.sum(jnp.where(is_grp, jnp.exp(logits - gmax), 0.0), axis=-1, keepdims=True)
    lo = N_GROUPS_MOE + gsel * EXP_PER_GROUP
    in_grp = (lane >= lo) & (lane < lo + EXP_PER_GROUP)
    el = jnp.where(in_grp, logits, neg)
    m1 = jnp.max(el, axis=-1, keepdims=True)
    i1 = jnp.min(jnp.where(el == m1, lane, big), axis=-1, keepdims=True)
    el2 = jnp.where(lane == i1, neg, el)
    m2 = jnp.max(el2, axis=-1, keepdims=True)
    i2 = jnp.min(jnp.where(el2 == m2, lane, big), axis=-1, keepdims=True)
    e = jnp.exp(m2 - m1)
    w1 = gp / (1.0 + e)
    w2 = gp * (e / (1.0 + e))
    rw = jnp.where(lane == 0.0, w1, jnp.where(lane == 1.0, w2, 0.0))
    return rw, lane, i1, i2


def _wo_kernel(m_ref, wo_ref, xp_ref, xs_ref, g_ref, gtp, gts, scp, scs, shp, shs, wrc_ref, wrh_ref, br_ref,
               x1_ref, h2_ref, rw_ref, re_ref, cnt_ref, carry):
    i = pl.program_id(0)
    x = jnp.where(i >= NP // TW, xs_ref[...], xp_ref[...])
    gt = _mod_rows(i, TW, gtp, gts)
    x1 = x + gt * _dot(m_ref[...], wo_ref[...])
    x1_ref[...] = x1
    sc = _mod_rows(i, TW, scp, scs)
    sh = _mod_rows(i, TW, shp, shs)
    h2 = ((x1 * _rms(x1)) * g_ref[...]) * (1.0 + sc) + sh
    h2_ref[...] = h2
    hi = h2.astype(BF16)
    lo = (h2 - hi.astype(F32)).astype(BF16)
    both = _dot(hi, wrc_ref[...])
    logits = (both[:, :LANES] + (both[:, LANES:] + _dot(lo, wrh_ref[...]))) + br_ref[...]
    rw, lane, i1, i2 = _route(logits)
    rw_ref[...] = rw

    @pl.when(i == 0)
    def _():
        carry[...] = jnp.zeros_like(carry)

    o1 = (lane == i1).astype(F32)
    o2 = (lane == i2).astype(F32)
    both_hot = o1 + o2
    r = lax.broadcasted_iota(jnp.int32, (TW, TW), 0)
    c = lax.broadcasted_iota(jnp.int32, (TW, TW), 1)
    before = (r > c).astype(BF16)
    seen = _dot(before, both_hot.astype(BF16)) + carry[...]
    rank1 = jnp.sum(seen * o1, axis=-1, keepdims=True)
    rank2 = jnp.sum(seen * o2, axis=-1, keepdims=True)
    total = carry[...] + jnp.sum(both_hot, axis=0, keepdims=True)
    carry[...] = total
    cnt_ref[...] = jnp.broadcast_to(total, cnt_ref.shape).astype(jnp.int32)
    re = jnp.where(lane == 0.0, i1 - N_GROUPS_MOE,
                   jnp.where(lane == 1.0, i2 - N_GROUPS_MOE,
                             jnp.where(lane == 2.0, rank1, jnp.where(lane == 3.0, rank2, 0.0))))
    re_ref[...] = re.astype(jnp.int32)


def _wo(merged, w_o, x_prompt_rows, x_sample_rows, sample_block0, g2, mod, wr_cat, wr_hi, b_r, l):
    row = lambda i: (i, 0)
    n_ptiles = NP // TW
    return pl.pallas_call(
        _wo_kernel,
        out_shape=(jax.ShapeDtypeStruct((NT, D), F32),
                   jax.ShapeDtypeStruct((NT, D), F32),
                   jax.ShapeDtypeStruct((NT, LANES), F32),
                   jax.ShapeDtypeStruct((NT, LANES), jnp.int32),
                   jax.ShapeDtypeStruct((8, LANES), jnp.int32)),
        grid=(NT // TW,),
        in_specs=[
            pl.BlockSpec((TW, D), row),
            pl.BlockSpec((None, D, D), lambda i: (l, 0, 0)),
            pl.BlockSpec((TW, D), lambda i: (jnp.minimum(i, n_ptiles - 1), 0)),
            pl.BlockSpec((TW, D), lambda i: (sample_block0 + jnp.maximum(i - n_ptiles, 0), 0)),
            pl.BlockSpec((None, 1, D), lambda i: (l, 0, 0)),
        ] + _mod_specs(l, 2, TW) + _mod_specs(l, 4, TW) + _mod_specs(l, 3, TW) + [
            pl.BlockSpec((None, D, 2 * LANES), lambda i: (l, 0, 0)),
            pl.BlockSpec((None, D, LANES), lambda i: (l, 0, 0)),
            pl.BlockSpec((None, 1, LANES), lambda i: (l, 0, 0)),
        ],
        out_specs=(pl.BlockSpec((TW, D), row), pl.BlockSpec((TW, D), row),
                   pl.BlockSpec((TW, LANES), row), pl.BlockSpec((TW, LANES), row),
                   pl.BlockSpec((8, LANES), lambda i: (0, 0))),
        scratch_shapes=[pltpu.VMEM((1, LANES), F32)],
        compiler_params=_params(("arbitrary",)),
        name="wo_router",
    )(merged, w_o, x_prompt_rows, x_sample_rows, g2, mod, mod, mod, mod, mod, mod, wr_cat, wr_hi, b_r)


def _row_gather_copy(src_hbm, idx, dst, sem):
    return pltpu.make_async_copy(src_hbm.at[pl.ds(idx, 1)], dst, sem)


def _moe_kernel(te_ref, tf_ref, nu_ref, pad_ref, dest_ref, h2_hbm, wg_ref, wu_ref, wd_ref, o_ref,
                src_ref, xbuf, sem, wg_s, wu_s, wd_s):
    del te_ref
    t = pl.program_id(0)
    n_used = nu_ref[0]

    def issue(tile, slot):
        base = tile * TE

        def body(r, carry):
            _row_gather_copy(h2_hbm, src_ref[base + r], xbuf.at[slot, pl.ds(r, 1)], sem.at[slot]).start()
            return carry

        lax.fori_loop(0, TE, body, 0, unroll=8)

    @pl.when(t == 0)
    def _():
        def clear(p, carry):
            src_ref[p] = 0
            return carry

        for e in range(N_EXPERTS):
            lax.fori_loop(pad_ref[e], pad_ref[N_EXPERTS + e], clear, 0)

        def invert(n, carry):
            for k in range(TOP_K):
                src_ref[dest_ref[TOP_K * n + k]] = n
            return carry

        lax.fori_loop(0, NT, invert, 0, unroll=8)
        for k in range(GATHER_SLOTS - 1):
            issue(jnp.minimum(k, n_used - 1), k)

    @pl.when((t < n_used) & (tf_ref[t] == 1))
    def _():
        wg_s[...] = wg_ref[...].astype(BF16)
        wu_s[...] = wu_ref[...].astype(BF16)
        wd_s[...] = wd_ref[...].astype(BF16)

    def wait_tile(slot):
        pltpu.make_async_copy(h2_hbm.at[pl.ds(0, TE)], xbuf.at[slot], sem.at[slot]).wait()

    @pl.when(t < n_used)
    def _():
        slot = t % GATHER_SLOTS
        wait_tile(slot)
        x = xbuf[slot].astype(BF16)
        ahead = (t + GATHER_SLOTS - 1) % GATHER_SLOTS
        base = jnp.minimum(t + GATHER_SLOTS - 1, n_used - 1) * TE
        for r in range(TE):
            _row_gather_copy(h2_hbm, src_ref[base + r], xbuf.at[ahead, pl.ds(r, 1)], sem.at[ahead]).start()
        a = _dot(x, wg_s[...])
        b = _dot(x, wu_s[...])
        hid = (a * _sigmoid(a)) * b
        o_ref[...] = _dot(hid.astype(BF16), wd_s[...])

    @pl.when(t == n_used - 1)
    def _():
        for k in range(1, GATHER_SLOTS):
            wait_tile((t + k) % GATHER_SLOTS)

    @pl.when(t >= n_used)
    def _():
        o_ref[...] = jnp.zeros_like(o_ref)


def _moe(h2, tile_expert, tile_first, n_used, pads, dest, w_gate, w_up, w_down, l):
    grid_spec = pltpu.PrefetchScalarGridSpec(
        num_scalar_prefetch=5,
        grid=(MAX_TILES,),
        in_specs=[
            pl.BlockSpec(memory_space=pl.ANY),
            pl.BlockSpec((None, None, D, D_FF), lambda t, te, *_: (l, te[t], 0, 0)),
            pl.BlockSpec((None, None, D, D_FF), lambda t, te, *_: (l, te[t], 0, 0)),
            pl.BlockSpec((None, None, D_FF, D), lambda t, te, *_: (l, te[t], 0, 0)),
        ],
        out_specs=pl.BlockSpec((TE, D), lambda t, *_: (t, 0)),
        scratch_shapes=[
            pltpu.SMEM((P_ROWS,), jnp.int32),
            pltpu.VMEM((GATHER_SLOTS, TE, D), F32),
            pltpu.SemaphoreType.DMA((GATHER_SLOTS,)),
            pltpu.VMEM((D, D_FF), BF16),
            pltpu.VMEM((D, D_FF), BF16),
            pltpu.VMEM((D_FF, D), BF16),
        ],
    )
    return pl.pallas_call(
        _moe_kernel,
        out_shape=jax.ShapeDtypeStruct((P_ROWS, D), F32),
        grid_spec=grid_spec,
        compiler_params=_params(("arbitrary",)),
        name="moe_experts",
    )(tile_expert, tile_first, n_used, pads, dest, h2, w_gate, w_up, w_down)


def _dispatch(route_e, counts_lanes):
    counts = counts_lanes[0, N_GROUPS_MOE:N_GROUPS_MOE + N_EXPERTS]
    tiles = (counts + TE - 1) // TE
    eids = jnp.arange(N_EXPERTS, dtype=jnp.int32)
    tile_end = jnp.sum(jnp.where(eids[None, :] <= eids[:, None], tiles[None, :], 0), axis=1)
    tile_start = tile_end - tiles
    hot = route_e[:, :TOP_K, None] == eids[None, None, :]
    row0 = jnp.sum(jnp.where(hot, tile_start[None, None, :] * TE, 0), axis=-1)
    dest = (row0 + route_e[:, TOP_K:2 * TOP_K]).reshape(-1).astype(jnp.int32)
    tids = jnp.arange(MAX_TILES, dtype=jnp.int32)
    tile_expert = jnp.minimum(jnp.sum((tile_end[None, :] <= tids[:, None]).astype(jnp.int32), axis=1),
                              N_EXPERTS - 1)
    first_tile = jnp.sum(jnp.where(tile_expert[:, None] == eids[None, :], tile_start[None, :], 0), axis=1)
    tile_first = (tids == first_tile).astype(jnp.int32)
    n_used = tile_end[-1:].astype(jnp.int32)
    pads = jnp.concatenate([tile_start * TE + counts, tile_end * TE]).astype(jnp.int32)
    return dest, tile_expert, tile_first, n_used, pads


def _combine_kernel(final, tile0, dest_ref, ys_hbm, x_ref, rw_ref, g_ref, gtp, gts, *rest):
    if final:
        y_ref, ybuf, sem = rest
    else:
        scp, scs, shp, shs, x2_ref, h_ref, ybuf, sem = rest
    step = pl.program_id(0)
    nsteps = pl.num_programs(0)
    i = step + tile0

    def issue(tile, slot, r):
        for k in range(TOP_K):
            _row_gather_copy(ys_hbm, dest_ref[tile * (TK * TOP_K) + TOP_K * r + k],
                             ybuf.at[slot, k, pl.ds(r, 1)], sem.at[slot]).start()

    def wait_tile(slot):
        for k in range(TOP_K):
            pltpu.make_async_copy(ys_hbm.at[pl.ds(0, TK)], ybuf.at[slot, k], sem.at[slot]).wait()

    last = tile0 + nsteps - 1

    @pl.when(step == 0)
    def _():
        for k in range(GATHER_SLOTS - 1):
            def body(r, carry, k=k):
                issue(jnp.minimum(i + k, last), k, r)
                return carry

            lax.fori_loop(0, TK, body, 0, unroll=8)

    slot = step % GATHER_SLOTS
    wait_tile(slot)
    ahead = (step + GATHER_SLOTS - 1) % GATHER_SLOTS
    nxt = jnp.minimum(i + GATHER_SLOTS - 1, last)
    for r in range(TK):
        issue(nxt, ahead, r)
    gt = _mod_rows(i, TK, gtp, gts)
    rw = rw_ref[...]
    moe = rw[:, 0:1] * ybuf[slot, 0] + rw[:, 1:2] * ybuf[slot, 1]
    x2 = x_ref[...] + gt * moe
    xn = (x2 * _rms(x2)) * g_ref[...]
    if final:
        y_ref[...] = xn
    else:
        x2_ref[...] = x2
        sc = _mod_rows(i, TK, scp, scs)
        sh = _mod_rows(i, TK, shp, shs)
        h_ref[...] = (xn * (1.0 + sc) + sh).astype(BF16)

    @pl.when(step == nsteps - 1)
    def _():
        for k in range(1, GATHER_SLOTS):
            wait_tile((step + k) % GATHER_SLOTS)


def _combine(dest, ys, x1, route_w, g, mod, l, final, tile0=0, ntiles=NT // TK):
    row_in = lambda s, d: (s + tile0, 0)
    row_out = lambda s, d: (s, 0)
    nmod = lambda q, ll: [
        pl.BlockSpec((None, 8, D), lambda s, d: (ll, MOD_PROMPT_BLOCK, q)),
        pl.BlockSpec((None, DEC_BATCH, D), lambda s, d: (ll, 0, q)),
    ]
    in_specs = [pl.BlockSpec(memory_space=pl.ANY), pl.BlockSpec((TK, D), row_in),
                pl.BlockSpec((TK, LANES), row_in),
                pl.BlockSpec((1, D), lambda s, d: (0, 0))] + nmod(5, l)
    args = [ys, x1, route_w, g, mod, mod]
    rows = ntiles * TK
    if final:
        out_shape = jax.ShapeDtypeStruct((rows, D), F32)
        out_specs = pl.BlockSpec((TK, D), row_out)
    else:
        in_specs += nmod(1, l + 1) + nmod(0, l + 1)
        args += [mod, mod, mod, mod]
        out_shape = (jax.ShapeDtypeStruct((rows, D), F32), jax.ShapeDtypeStruct((rows, D), BF16))
        out_specs = (pl.BlockSpec((TK, D), row_out), pl.BlockSpec((TK, D), row_out))
    grid_spec = pltpu.PrefetchScalarGridSpec(
        num_scalar_prefetch=1,
        grid=(ntiles,),
        in_specs=in_specs,
        out_specs=out_specs,
        scratch_shapes=[pltpu.VMEM((GATHER_SLOTS, TOP_K, TK, D), F32),
                        pltpu.SemaphoreType.DMA((GATHER_SLOTS,))],
    )
    return pl.pallas_call(
        functools.partial(_combine_kernel, final, tile0),
        out_shape=out_shape,
        grid_spec=grid_spec,
        compiler_params=_params(("arbitrary",)),
        name=("combine_final_%d" % tile0) if final else "combine",
    )(dest, *args)


def kernel(x_prompt, x_sample, c_prompt, c_sample, state_conv, state_pool, w_ada, b_ada, g_norm1, w_in,
           conv_w, pool_w, pool_scale, g_v, w_s, b_s, w_br, w_o, g_norm2, w_rg, b_rg, w_re, b_re,
           w_gate, w_up, w_down, g_final):
    xp_rows = x_prompt.reshape(NP, D)
    xs_rows = x_sample.transpose(1, 0, 2).reshape(NS, D)
    sample_block0 = 0
    c_all = jnp.concatenate([c_sample, c_prompt,
                             jnp.zeros((MOD_ROWS - DEC_BATCH - BATCH, D), F32)], axis=0).astype(BF16)
    mod = _ada(c_all, w_ada, b_ada)

    w_in_b = w_in.astype(BF16)
    w_br_b = w_br.astype(BF16)
    w_o_b = w_o.astype(BF16)
    pool_w_b = pool_w.astype(BF16)
    tril = jnp.tril(jnp.ones((CHUNK, CHUNK), dtype=bool))
    ws_tril = jnp.where(tril[None, None], w_s, 0.0).astype(BF16)
    bs_tile = jnp.broadcast_to(b_s[:, :, :, None], (DEPTH, HEADS, CHUNK, CHUNK))
    small = jnp.tril(jnp.ones((DEC_SEQ, DEC_SEQ), dtype=bool))
    ws_small = jnp.where(small[None, None], w_s[:, :, :DEC_SEQ, :DEC_SEQ], 0.0)
    wvec = jnp.repeat(ws_small.transpose(0, 2, 3, 1).reshape(DEPTH, DEC_SEQ * DEC_SEQ, HEADS), CHUNK, axis=-1)
    bvec = jnp.repeat(b_s[:, :, :DEC_SEQ].transpose(0, 2, 1), CHUNK, axis=-1)
    w_r = jnp.concatenate([w_rg, w_re, jnp.zeros((DEPTH, D, LANES - N_GROUPS_MOE - N_EXPERTS), F32)], axis=-1)
    wr_hi = w_r.astype(BF16)
    wr_lo = (w_r - wr_hi.astype(F32)).astype(BF16)
    wr_cat = jnp.concatenate([wr_hi, wr_lo], axis=-1)
    b_r = jnp.concatenate([b_rg, b_re, jnp.zeros((DEPTH, LANES - N_GROUPS_MOE - N_EXPERTS), F32)],
                          axis=-1).reshape(DEPTH, 1, LANES)
    conv_t = state_conv.transpose(0, 2, 1, 3)
    pool_t = state_pool.transpose(0, 2, 1, 3)
    g1 = g_norm1.reshape(DEPTH, 1, D)
    g2 = g_norm2.reshape(DEPTH, 1, D)
    gv = g_v.reshape(DEPTH, 1, W)
    pscale = pool_scale.reshape(DEPTH, 1, W)

    h = _norm(xp_rows, xs_rows, g1, mod, 0)
    conv_p, pool_p, conv_s, pool_s, v_s = [], [], [], [], []
    y_prompt = y_sample = None
    for l in range(DEPTH):
        ya, cst_p = _conv_p(h, w_in_b, conv_w, l)
        yb, pst_p = _pool_p(h, w_in_b, pool_w_b, pscale, l)
        yc = _gmlp_p(h, w_in_b, gv, ws_tril, bs_tile, l)
        ya_s, yb_s, cst_s, pst_s = _mix_s(h, w_in_b, conv_w, pool_w_b, pscale, conv_t, pool_t, l)
        yc_s, vn_s = _gmlp_s(h, w_in_b, gv, wvec, bvec, l)
        merged = _merge(h, (ya, yb, yc), (ya_s, yb_s, yc_s), w_in_b, w_br_b, l)
        x1, h2, route_w, route_e, counts = _wo(merged, w_o_b, xp_rows, xs_rows, sample_block0, g2, mod,
                                               wr_cat, wr_hi, b_r, l)
        dest, tile_expert, tile_first, n_used, pads = _dispatch(route_e, counts)
        ys = _moe(h2, tile_expert, tile_first, n_used, pads, dest, w_gate, w_up, w_down, l)
        if l + 1 < DEPTH:
            x, h = _combine(dest, ys, x1, route_w, g1[l + 1], mod, l, final=False)
            xp_rows = xs_rows = x
            sample_block0 = NP // TW
        else:
            gf = g_final.reshape(1, D)
            y_prompt = _combine(dest, ys, x1, route_w, gf, mod, l, final=True, tile0=0, ntiles=NP // TK)
            y_sample = _combine(dest, ys, x1, route_w, gf, mod, l, final=True, tile0=NP // TK,
                                ntiles=NS // TK)
        conv_p.append(cst_p[:, 8 - (CONV_K - 1):, :])
        pool_p.append(pst_p[:, 16 - POOL_STATE:, :])
        conv_s.append(cst_s.transpose(1, 0, 2))
        pool_s.append(pst_s.transpose(1, 0, 2))
        v_s.append(vn_s.reshape(DEC_SEQ, DEC_BATCH, W).transpose(1, 0, 2))

    y_prompt = y_prompt.reshape(BATCH, SEQ, D)
    y_sample = y_sample.reshape(DEC_SEQ, DEC_BATCH, D).transpose(1, 0, 2)
    return (y_prompt, y_sample, jnp.stack(conv_p), jnp.stack(pool_p), jnp.stack(conv_s),
            jnp.stack(pool_s), jnp.stack(v_s))
```

```python
import functools

import jax
import jax.numpy as jnp
from jax import lax
from jax.experimental import pallas as pl
from jax.experimental.pallas import tpu as pltpu

F32 = jnp.float32
BF16 = jnp.bfloat16

D = 2048
BATCH = 4
SEQ = 2048
DEPTH = 2
DEC_BATCH = 128
DEC_SEQ = 4
W = D // 2
CONV_K = 3
POOL_STATE = 15
N_POOL_GROUPS = 4
POOL_GROUP = W // N_POOL_GROUPS
CHUNK = 128
HEADS = W // CHUNK
IN_COLS = 6 * W + 3 * D
N_GROUPS_MOE = 4
EXP_PER_GROUP = 4
N_EXPERTS = 16
TOP_K = 2
D_FF = D // 4
EPS = 1e-6

NP = BATCH * SEQ
NS = DEC_BATCH * DEC_SEQ
NT = NP + NS
TM = 512
N_TILES = NT // TM
CB = 256
TE = 256
MAX_TILES = (NT * TOP_K) // TE + N_EXPERTS
P_ROWS = MAX_TILES * TE
TK = 256
TW = 256
GATHER_SLOTS = 3
LANES = 128
MOD_ROWS = 144
MOD_PROMPT_BLOCK = DEC_BATCH // 8

VMEM_LIMIT_V7X = 56 * 1024 * 1024


def _dot(a, b):
    return jnp.dot(a, b, preferred_element_type=F32)


def _params(sem, vmem=VMEM_LIMIT_V7X):
    return pltpu.CompilerParams(dimension_semantics=sem, vmem_limit_bytes=vmem)


def _sigmoid(x):
    return 1.0 / (1.0 + jnp.exp(-x))


def _rms(x):
    return lax.rsqrt(jnp.mean(x * x, axis=-1, keepdims=True) + EPS)


def _ada_kernel(c_ref, w_ref, b_ref, o_ref):
    o_ref[...] = _dot(c_ref[...], w_ref[...].astype(BF16)) + b_ref[...]


def _ada(c_all, w_ada, b_ada):
    nb = 1024
    return pl.pallas_call(
        _ada_kernel,
        out_shape=jax.ShapeDtypeStruct((DEPTH, MOD_ROWS, 6 * D), F32),
        grid=(DEPTH, 6 * D // nb),
        in_specs=[
            pl.BlockSpec((MOD_ROWS, D), lambda l, j: (0, 0)),
            pl.BlockSpec((None, D, nb), lambda l, j: (l, 0, j)),
            pl.BlockSpec((None, 1, nb), lambda l, j: (l, 0, j)),
        ],
        out_specs=pl.BlockSpec((None, MOD_ROWS, nb), lambda l, j: (l, 0, j)),
        compiler_params=_params(("arbitrary", "arbitrary")),
        name="ada",
    )(c_all, w_ada, b_ada.reshape(DEPTH, 1, 6 * D))


def _mod_specs(l, q, tile_rows):
    del tile_rows
    return [
        pl.BlockSpec((None, 8, D), lambda i, *_: (l, MOD_PROMPT_BLOCK, q)),
        pl.BlockSpec((None, DEC_BATCH, D), lambda i, *_: (l, 0, q)),
    ]


def _mod_rows(i, tile_rows, mp_ref, ms_ref):
    tiles_per_seq = SEQ // tile_rows
    is_sample = i >= NP // tile_rows
    b = jnp.minimum(i // tiles_per_seq, BATCH - 1)
    mp = mp_ref[pl.ds(b, 1), :]
    ms = jnp.concatenate([ms_ref[...]] * (tile_rows // DEC_BATCH), axis=0)
    return jnp.where(is_sample, ms, mp)


def _norm_kernel(xp_ref, xs_ref, g_ref, scp, scs, shp, shs, h_ref):
    i = pl.program_id(0)
    x = jnp.where(i >= NP // TM, xs_ref[...], xp_ref[...])
    sc = _mod_rows(i, TM, scp, scs)
    sh = _mod_rows(i, TM, shp, shs)
    h_ref[...] = (((x * _rms(x)) * g_ref[...]) * (1.0 + sc) + sh).astype(BF16)


def _norm(x_prompt_rows, x_sample_rows, g, mod, l):
    n_ptiles = NP // TM
    return pl.pallas_call(
        _norm_kernel,
        out_shape=jax.ShapeDtypeStruct((NT, D), BF16),
        grid=(N_TILES,),
        in_specs=[pl.BlockSpec((TM, D), lambda i: (jnp.minimum(i, n_ptiles - 1), 0)),
                  pl.BlockSpec((TM, D), lambda i: (jnp.maximum(i - n_ptiles, 0), 0)),
                  pl.BlockSpec((None, 1, D), lambda i: (l, 0, 0))]
                 + _mod_specs(l, 1, TM) + _mod_specs(l, 0, TM),
        out_specs=pl.BlockSpec((TM, D), lambda i: (i, 0)),
        compiler_params=_params(("arbitrary",)),
        name="norm1",
    )(x_prompt_rows, x_sample_rows, g, mod, mod, mod, mod)


def _shift_rows(a, s, row):
    return jnp.where(row >= s, pltpu.roll(a, s, 0), 0.0)


def _conv_p_kernel(h_ref, wb_ref, wc_ref, wh_ref, cw_ref, y_ref, st_ref):
    h = h_ref[...]
    z = _dot(h, wc_ref[...].astype(BF16)) * _dot(h, wh_ref[...].astype(BF16))
    row = lax.broadcasted_iota(jnp.int32, (SEQ, 1), 0)
    cw = cw_ref[...]
    y = cw[0:1] * _shift_rows(z, 2, row) + cw[1:2] * _shift_rows(z, 1, row) + cw[2:3] * z
    y_ref[...] = (_dot(h, wb_ref[...].astype(BF16)) * y).astype(BF16)
    st_ref[...] = z[SEQ - 8:, :]


def _conv_p(h, w_in, conv_w, l):
    nc = W // CB
    return pl.pallas_call(
        _conv_p_kernel,
        out_shape=(jax.ShapeDtypeStruct((NP, W), BF16),
                   jax.ShapeDtypeStruct((BATCH, 8, W), F32)),
        grid=(BATCH, nc),
        in_specs=[
            pl.BlockSpec((SEQ, D), lambda b, c: (b, 0)),
            pl.BlockSpec((None, D, CB), lambda b, c: (l, 0, c)),
            pl.BlockSpec((None, D, CB), lambda b, c: (l, 0, nc + c)),
            pl.BlockSpec((None, D, CB), lambda b, c: (l, 0, 2 * nc + c)),
            pl.BlockSpec((None, CONV_K, CB), lambda b, c: (l, 0, c)),
        ],
        out_specs=(pl.BlockSpec((SEQ, CB), lambda b, c: (b, c)),
                   pl.BlockSpec((None, 8, CB), lambda b, c: (b, 0, c))),
        compiler_params=_params(("arbitrary", "arbitrary")),
        name="conv_prompt",
    )(h, w_in, w_in, w_in, conv_w)


def _pool_p_kernel(h_ref, wp_ref, pw_ref, ps_ref, y_ref, st_ref):
    g = pl.program_id(1)
    p = _dot(h_ref[...], wp_ref[...].astype(BF16))
    row = lax.broadcasted_iota(jnp.int32, (SEQ, 1), 0)
    s = p + _shift_rows(p, 1, row)
    s = jnp.where(g >= 1, s + _shift_rows(s, 2, row), s)
    s = jnp.where(g >= 2, s + _shift_rows(s, 4, row), s)
    s = jnp.where(g >= 3, s + _shift_rows(s, 8, row), s)
    window = jnp.left_shift(2, g)
    cnt = jnp.minimum(row + 1, window).astype(F32)
    d = s / cnt - p
    y_ref[...] = (_dot(d.astype(BF16), pw_ref[...]) * ps_ref[...]).astype(BF16)
    st_ref[...] = p[SEQ - 16:, :]


def _pool_p(h, w_in, pool_w, pool_scale, l):
    col0 = 3 * W // CB
    return pl.pallas_call(
        _pool_p_kernel,
        out_shape=(jax.ShapeDtypeStruct((NP, W), BF16),
                   jax.ShapeDtypeStruct((BATCH, 16, W), F32)),
        grid=(BATCH, N_POOL_GROUPS),
        in_specs=[
            pl.BlockSpec((SEQ, D), lambda b, g: (b, 0)),
            pl.BlockSpec((None, D, CB), lambda b, g: (l, 0, col0 + g)),
            pl.BlockSpec((None, None, POOL_GROUP, POOL_GROUP), lambda b, g: (l, g, 0, 0)),
            pl.BlockSpec((None, 1, CB), lambda b, g: (l, 0, g)),
        ],
        out_specs=(pl.BlockSpec((SEQ, CB), lambda b, g: (b, g)),
                   pl.BlockSpec((None, 16, CB), lambda b, g: (b, 0, g))),
        compiler_params=_params(("arbitrary", "arbitrary")),
        name="pool_prompt",
    )(h, w_in, pool_w, pool_scale)


def _gmlp_p_kernel(h_ref, wv_ref, wu_ref, gv_ref, ws_ref, bs_ref, y_ref, vn_s):
    j = pl.program_id(1)

    @pl.when(j == 0)
    def _():
        wv = wv_ref[...].astype(BF16)
        for r0 in range(0, SEQ, TM):
            v = _dot(h_ref[r0:r0 + TM, :], wv)
            vn = (v * _rms(v)) * gv_ref[...]
            for hd in range(HEADS):
                vn_s[hd, r0:r0 + TM, :] = vn[:, hd * CHUNK:(hd + 1) * CHUNK].astype(BF16)

    u = _dot(h_ref[...], wu_ref[...].astype(BF16))
    heads_per_block = CB // CHUNK
    for hh in range(heads_per_block):
        head = heads_per_block * j + hh
        wsh = ws_ref[head]
        bsh = bs_ref[head]
        for n in range(SEQ // CHUNK):
            rows = slice(n * CHUNK, (n + 1) * CHUNK)
            cols = slice(hh * CHUNK, (hh + 1) * CHUNK)
            s = _dot(wsh, vn_s[head, pl.ds(n * CHUNK, CHUNK), :]) + bsh
            y_ref[rows, cols] = (u[rows, cols] * s).astype(BF16)


def _gmlp_p(h, w_in, g_v, ws_tril, bs_tile, l):
    ucol0 = 4 * W // CB
    return pl.pallas_call(
        _gmlp_p_kernel,
        out_shape=jax.ShapeDtypeStruct((NP, W), BF16),
        grid=(BATCH, W // CB),
        in_specs=[
            pl.BlockSpec((SEQ, D), lambda b, j: (b, 0)),
            pl.BlockSpec((None, D, W), lambda b, j: (l, 0, 5), pipeline_mode=pl.Buffered(1)),
            pl.BlockSpec((None, D, CB), lambda b, j: (l, 0, ucol0 + j)),
            pl.BlockSpec((None, 1, W), lambda b, j: (l, 0, 0)),
            pl.BlockSpec((None, HEADS, CHUNK, CHUNK), lambda b, j: (l, 0, 0, 0)),
            pl.BlockSpec((None, HEADS, CHUNK, CHUNK), lambda b, j: (l, 0, 0, 0)),
        ],
        out_specs=pl.BlockSpec((SEQ, CB), lambda b, j: (b, j)),
        scratch_shapes=[pltpu.VMEM((HEADS, SEQ, CHUNK), BF16)],
        compiler_params=_params(("arbitrary", "arbitrary")),
        name="gmlp_prompt",
    )(h, w_in, w_in, g_v, ws_tril, bs_tile)


def _mix_s_kernel(h_ref, wb_ref, wc_ref, wh_ref, wp_ref, cw_ref, pc_ref, pp_ref,
                  pw_ref, ps_ref, ya_ref, yb_ref, cst_ref, pst_ref):
    g = pl.program_id(0)
    h = h_ref[...]
    nb = DEC_BATCH
    z = _dot(h, wc_ref[...].astype(BF16)) * _dot(h, wh_ref[...].astype(BF16))
    bg = _dot(h, wb_ref[...].astype(BF16))
    zs = [pc_ref[0], pc_ref[1]] + [z[t * nb:(t + 1) * nb] for t in range(DEC_SEQ)]
    cw = cw_ref[...]
    for t in range(DEC_SEQ):
        y = cw[0:1] * zs[t] + cw[1:2] * zs[t + 1] + cw[2:3] * zs[t + 2]
        ya_ref[t * nb:(t + 1) * nb, :] = (bg[t * nb:(t + 1) * nb] * y).astype(BF16)
    cst_ref[0] = zs[DEC_SEQ]
    cst_ref[1] = zs[DEC_SEQ + 1]
    p = _dot(h, wp_ref[...].astype(BF16))
    pp = [pp_ref[k] for k in range(POOL_STATE)] + [p[t * nb:(t + 1) * nb] for t in range(DEC_SEQ)]
    window = jnp.left_shift(2, g).astype(F32)
    ds = []
    for t in range(DEC_SEQ):
        e = POOL_STATE + t
        s = pp[e] + pp[e - 1]
        s4 = s + (pp[e - 2] + pp[e - 3])
        s8 = s4 + ((pp[e - 4] + pp[e - 5]) + (pp[e - 6] + pp[e - 7]))
        s16 = s8 + (((pp[e - 8] + pp[e - 9]) + (pp[e - 10] + pp[e - 11]))
                    + ((pp[e - 12] + pp[e - 13]) + (pp[e - 14] + pp[e - 15])))
        s = jnp.where(g >= 1, s4, s)
        s = jnp.where(g >= 2, s8, s)
        s = jnp.where(g >= 3, s16, s)
        ds.append(s / window - pp[e])
    d = jnp.concatenate(ds, axis=0)
    yb_ref[...] = (_dot(d.astype(BF16), pw_ref[...]) * ps_ref[...]).astype(BF16)
    for k in range(POOL_STATE):
        pst_ref[k] = pp[DEC_SEQ + k]


def _mix_s(h, w_in, conv_w, pool_w, pool_scale, conv_t, pool_t, l):
    nc = W // CB
    srow = NP // NS
    return pl.pallas_call(
        _mix_s_kernel,
        out_shape=(jax.ShapeDtypeStruct((NS, W), BF16),
                   jax.ShapeDtypeStruct((NS, W), BF16),
                   jax.ShapeDtypeStruct((CONV_K - 1, DEC_BATCH, W), F32),
                   jax.ShapeDtypeStruct((POOL_STATE, DEC_BATCH, W), F32)),
        grid=(nc,),
        in_specs=[
            pl.BlockSpec((NS, D), lambda c: (srow, 0)),
            pl.BlockSpec((None, D, CB), lambda c: (l, 0, c)),
            pl.BlockSpec((None, D, CB), lambda c: (l, 0, nc + c)),
            pl.BlockSpec((None, D, CB), lambda c: (l, 0, 2 * nc + c)),
            pl.BlockSpec((None, D, CB), lambda c: (l, 0, 3 * nc + c)),
            pl.BlockSpec((None, CONV_K, CB), lambda c: (l, 0, c)),
            pl.BlockSpec((None, CONV_K - 1, DEC_BATCH, CB), lambda c: (l, 0, 0, c)),
            pl.BlockSpec((None, POOL_STATE, DEC_BATCH, CB), lambda c: (l, 0, 0, c)),
            pl.BlockSpec((None, None, POOL_GROUP, POOL_GROUP), lambda c: (l, c, 0, 0)),
            pl.BlockSpec((None, 1, CB), lambda c: (l, 0, c)),
        ],
        out_specs=(pl.BlockSpec((NS, CB), lambda c: (0, c)),
                   pl.BlockSpec((NS, CB), lambda c: (0, c)),
                   pl.BlockSpec((CONV_K - 1, DEC_BATCH, CB), lambda c: (0, 0, c)),
                   pl.BlockSpec((POOL_STATE, DEC_BATCH, CB), lambda c: (0, 0, c))),
        compiler_params=_params(("arbitrary",)),
        name="mix_sample",
    )(h, w_in, w_in, w_in, w_in, conv_w, conv_t, pool_t, pool_w, pool_scale)


def _gmlp_s_kernel(h_ref, wu_ref, wv_ref, gv_ref, wvec_ref, bvec_ref, yc_ref, vn_ref):
    h = h_ref[...]
    nb = DEC_BATCH
    v = _dot(h, wv_ref[...].astype(BF16))
    vn = (v * _rms(v)) * gv_ref[...]
    vn_ref[...] = vn
    u = _dot(h, wu_ref[...].astype(BF16))
    for t in range(DEC_SEQ):
        s = bvec_ref[t:t + 1, :]
        for sp in range(t + 1):
            k = t * DEC_SEQ + sp
            s = s + wvec_ref[k:k + 1, :] * vn[sp * nb:(sp + 1) * nb]
        yc_ref[t * nb:(t + 1) * nb, :] = (u[t * nb:(t + 1) * nb] * s).astype(BF16)


def _gmlp_s(h, w_in, g_v, wvec, bvec, l):
    srow = NP // NS
    return pl.pallas_call(
        _gmlp_s_kernel,
        out_shape=(jax.ShapeDtypeStruct((NS, W), BF16),
                   jax.ShapeDtypeStruct((NS, W), F32)),
        grid=(1,),
        in_specs=[
            pl.BlockSpec((NS, D), lambda i: (srow, 0)),
            pl.BlockSpec((None, D, W), lambda i: (l, 0, 4)),
            pl.BlockSpec((None, D, W), lambda i: (l, 0, 5)),
            pl.BlockSpec((None, 1, W), lambda i: (l, 0, 0)),
            pl.BlockSpec((None, DEC_SEQ * DEC_SEQ, W), lambda i: (l, 0, 0)),
            pl.BlockSpec((None, DEC_SEQ, W), lambda i: (l, 0, 0)),
        ],
        out_specs=(pl.BlockSpec((NS, W), lambda i: (0, 0)),
                   pl.BlockSpec((NS, W), lambda i: (0, 0))),
        compiler_params=_params(("arbitrary",)),
        name="gmlp_sample",
    )(h, w_in, w_in, g_v, wvec, bvec)


def _merge_kernel(h_ref, yap, ybp, ycp, yas, ybs, ycs, wg0, wg1, wg2, wbr_ref, o_ref):
    is_sample = pl.program_id(0) >= NP // TM
    h = h_ref[...]
    acc = None
    for n, (yp, ys, wg) in enumerate(((yap, yas, wg0), (ybp, ybs, wg1), (ycp, ycs, wg2))):
        y = jnp.where(is_sample, ys[...], yp[...])
        term = _sigmoid(_dot(h, wg[...])) * _dot(y, wbr_ref[n])
        acc = term if acc is None else acc + term
    o_ref[...] = acc.astype(BF16)


def _merge(h, y_prompt, y_sample, w_gates, w_br, l):
    db = 512
    g0 = 0
    gstep = D // db
    return pl.pallas_call(
        _merge_kernel,
        out_shape=jax.ShapeDtypeStruct((NT, D), BF16),
        grid=(N_TILES, D // db),
        in_specs=[
            pl.BlockSpec((TM, D), lambda i, d: (i, 0)),
            pl.BlockSpec((TM, W), lambda i, d: (jnp.minimum(i, NP // TM - 1), 0)),
            pl.BlockSpec((TM, W), lambda i, d: (jnp.minimum(i, NP // TM - 1), 0)),
            pl.BlockSpec((TM, W), lambda i, d: (jnp.minimum(i, NP // TM - 1), 0)),
            pl.BlockSpec((NS, W), lambda i, d: (0, 0)),
            pl.BlockSpec((NS, W), lambda i, d: (0, 0)),
            pl.BlockSpec((NS, W), lambda i, d: (0, 0)),
            pl.BlockSpec((None, D, db), lambda i, d: (l, 0, g0 + d)),
            pl.BlockSpec((None, D, db), lambda i, d: (l, 0, g0 + gstep + d)),
            pl.BlockSpec((None, D, db), lambda i, d: (l, 0, g0 + 2 * gstep + d)),
            pl.BlockSpec((None, 3, W, db), lambda i, d: (l, 0, 0, d)),
        ],
        out_specs=pl.BlockSpec((TM, db), lambda i, d: (i, d)),
        compiler_params=_params(("arbitrary", "arbitrary")),
        name="merge",
    )(h, *y_prompt, *y_sample, w_gates, w_gates, w_gates, w_br)


def _route(logits):
    lane = lax.broadcasted_iota(jnp.int32, logits.shape, 1).astype(F32)
    neg = -jnp.inf
    big = float(LANES)
    is_grp = lane < N_GROUPS_MOE
    gl = jnp.where(is_grp, logits, neg)
    gmax = jnp.max(gl, axis=-1, keepdims=True)
    gsel = jnp.min(jnp.where(gl == gmax, lane, big), axis=-1, keepdims=True)
    gp = 1.0 / jnp.sum(jnp.where(is_grp, jnp.exp(logits - gmax), 0.0), axis=-1, keepdims=True)
    lo = N_GROUPS_MOE + gsel * EXP_PER_GROUP
    in_grp = (lane >= lo) & (lane < lo + EXP_PER_GROUP)
    el = jnp.where(in_grp, logits, neg)
    m1 = jnp.max(el, axis=-1, keepdims=True)
    i1 = jnp.min(jnp.where(el == m1, lane, big), axis=-1, keepdims=True)
    el2 = jnp.where(lane == i1, neg, el)
    m2 = jnp.max(el2, axis=-1, keepdims=True)
    i2 = jnp.min(jnp.where(el2 == m2, lane, big), axis=-1, keepdims=True)
    e = jnp.exp(m2 - m1)
    w1 = gp / (1.0 + e)
    w2 = gp * (e / (1.0 + e))
    rw = jnp.where(lane == 0.0, w1, jnp.where(lane == 1.0, w2, 0.0))
    return rw, lane, i1, i2


def _wo_kernel(m_ref, wo_ref, xp_ref, xs_ref, g_ref, gtp, gts, scp, scs, shp, shs, wrc_ref, wrh_ref, br_ref,
               x1_ref, h2_ref, rw_ref, re_ref, cnt_ref, carry):
    i = pl.program_id(0)
    x = jnp.where(i >= NP // TW, xs_ref[...], xp_ref[...])
    gt = _mod_rows(i, TW, gtp, gts)
    x1 = x + gt * _dot(m_ref[...], wo_ref[...])
    x1_ref[...] = x1
    sc = _mod_rows(i, TW, scp, scs)
    sh = _mod_rows(i, TW, shp, shs)
    h2 = ((x1 * _rms(x1)) * g_ref[...]) * (1.0 + sc) + sh
    h2_ref[...] = h2
    hi = h2.astype(BF16)
    lo = (h2 - hi.astype(F32)).astype(BF16)
    both = _dot(hi, wrc_ref[...])
    logits = (both[:, :LANES] + (both[:, LANES:] + _dot(lo, wrh_ref[...]))) + br_ref[...]
    rw, lane, i1, i2 = _route(logits)
    rw_ref[...] = rw

    @pl.when(i == 0)
    def _():
        carry[...] = jnp.zeros_like(carry)

    o1 = (lane == i1).astype(F32)
    o2 = (lane == i2).astype(F32)
    both_hot = o1 + o2
    r = lax.broadcasted_iota(jnp.int32, (TW, TW), 0)
    c = lax.broadcasted_iota(jnp.int32, (TW, TW), 1)
    before = (r > c).astype(BF16)
    seen = _dot(before, both_hot.astype(BF16)) + carry[...]
    rank1 = jnp.sum(seen * o1, axis=-1, keepdims=True)
    rank2 = jnp.sum(seen * o2, axis=-1, keepdims=True)
    total = carry[...] + jnp.sum(both_hot, axis=0, keepdims=True)
    carry[...] = total
    cnt_ref[...] = jnp.broadcast_to(total, cnt_ref.shape).astype(jnp.int32)
    re = jnp.where(lane == 0.0, i1 - N_GROUPS_MOE,
                   jnp.where(lane == 1.0, i2 - N_GROUPS_MOE,
                             jnp.where(lane == 2.0, rank1, jnp.where(lane == 3.0, rank2, 0.0))))
    re_ref[...] = re.astype(jnp.int32)


def _wo(merged, w_o, x_prompt_rows, x_sample_rows, sample_block0, g2, mod, wr_cat, wr_hi, b_r, l):
    row = lambda i: (i, 0)
    n_ptiles = NP // TW
    return pl.pallas_call(
        _wo_kernel,
        out_shape=(jax.ShapeDtypeStruct((NT, D), F32),
                   jax.ShapeDtypeStruct((NT, D), F32),
                   jax.ShapeDtypeStruct((NT, LANES), F32),
                   jax.ShapeDtypeStruct((NT, LANES), jnp.int32),
                   jax.ShapeDtypeStruct((8, LANES), jnp.int32)),
        grid=(NT // TW,),
        in_specs=[
            pl.BlockSpec((TW, D), row),
            pl.BlockSpec((None, D, D), lambda i: (l, 0, 0)),
            pl.BlockSpec((TW, D), lambda i: (jnp.minimum(i, n_ptiles - 1), 0)),
            pl.BlockSpec((TW, D), lambda i: (sample_block0 + jnp.maximum(i - n_ptiles, 0), 0)),
            pl.BlockSpec((None, 1, D), lambda i: (l, 0, 0)),
        ] + _mod_specs(l, 2, TW) + _mod_specs(l, 4, TW) + _mod_specs(l, 3, TW) + [
            pl.BlockSpec((None, D, 2 * LANES), lambda i: (l, 0, 0)),
            pl.BlockSpec((None, D, LANES), lambda i: (l, 0, 0)),
            pl.BlockSpec((None, 1, LANES), lambda i: (l, 0, 0)),
        ],
        out_specs=(pl.BlockSpec((TW, D), row), pl.BlockSpec((TW, D), row),
                   pl.BlockSpec((TW, LANES), row), pl.BlockSpec((TW, LANES), row),
                   pl.BlockSpec((8, LANES), lambda i: (0, 0))),
        scratch_shapes=[pltpu.VMEM((1, LANES), F32)],
        compiler_params=_params(("arbitrary",)),
        name="wo_router",
    )(merged, w_o, x_prompt_rows, x_sample_rows, g2, mod, mod, mod, mod, mod, mod, wr_cat, wr_hi, b_r)


def _row_gather_copy(src_hbm, idx, dst, sem):
    return pltpu.make_async_copy(src_hbm.at[pl.ds(idx, 1)], dst, sem)


def _moe_kernel(te_ref, tf_ref, nu_ref, pad_ref, dest_ref, h2_hbm, wg_ref, wu_ref, wd_ref, o_ref,
                src_ref, xbuf, sem, wg_s, wu_s, wd_s):
    del te_ref
    t = pl.program_id(0)
    n_used = nu_ref[0]

    def issue(tile, slot):
        base = tile * TE

        def body(r, carry):
            _row_gather_copy(h2_hbm, src_ref[base + r], xbuf.at[slot, pl.ds(r, 1)], sem.at[slot]).start()
            return carry

        lax.fori_loop(0, TE, body, 0, unroll=8)

    @pl.when(t == 0)
    def _():
        def clear(p, carry):
            src_ref[p] = 0
            return carry

        for e in range(N_EXPERTS):
            lax.fori_loop(pad_ref[e], pad_ref[N_EXPERTS + e], clear, 0)

        def invert(n, carry):
            for k in range(TOP_K):
                src_ref[dest_ref[TOP_K * n + k]] = n
            return carry

        lax.fori_loop(0, NT, invert, 0, unroll=8)
        for k in range(GATHER_SLOTS - 1):
            issue(jnp.minimum(k, n_used - 1), k)

    @pl.when((t < n_used) & (tf_ref[t] == 1))
    def _():
        wg_s[...] = wg_ref[...].astype(BF16)
        wu_s[...] = wu_ref[...].astype(BF16)
        wd_s[...] = wd_ref[...].astype(BF16)

    def wait_tile(slot):
        pltpu.make_async_copy(h2_hbm.at[pl.ds(0, TE)], xbuf.at[slot], sem.at[slot]).wait()

    @pl.when(t < n_used)
    def _():
        slot = t % GATHER_SLOTS
        wait_tile(slot)
        x = xbuf[slot].astype(BF16)
        ahead = (t + GATHER_SLOTS - 1) % GATHER_SLOTS
        base = jnp.minimum(t + GATHER_SLOTS - 1, n_used - 1) * TE
        for r in range(TE):
            _row_gather_copy(h2_hbm, src_ref[base + r], xbuf.at[ahead, pl.ds(r, 1)], sem.at[ahead]).start()
        a = _dot(x, wg_s[...])
        b = _dot(x, wu_s[...])
        hid = (a * _sigmoid(a)) * b
        o_ref[...] = _dot(hid.astype(BF16), wd_s[...])

    @pl.when(t == n_used - 1)
    def _():
        for k in range(1, GATHER_SLOTS):
            wait_tile((t + k) % GATHER_SLOTS)

    @pl.when(t >= n_used)
    def _():
        o_ref[...] = jnp.zeros_like(o_ref)


def _moe(h2, tile_expert, tile_first, n_used, pads, dest, w_gate, w_up, w_down, l):
    grid_spec = pltpu.PrefetchScalarGridSpec(
        num_scalar_prefetch=5,
        grid=(MAX_TILES,),
        in_specs=[
            pl.BlockSpec(memory_space=pl.ANY),
            pl.BlockSpec((None, None, D, D_FF), lambda t, te, *_: (l, te[t], 0, 0)),
            pl.BlockSpec((None, None, D, D_FF), lambda t, te, *_: (l, te[t], 0, 0)),
            pl.BlockSpec((None, None, D_FF, D), lambda t, te, *_: (l, te[t], 0, 0)),
        ],
        out_specs=pl.BlockSpec((TE, D), lambda t, *_: (t, 0)),
        scratch_shapes=[
            pltpu.SMEM((P_ROWS,), jnp.int32),
            pltpu.VMEM((GATHER_SLOTS, TE, D), F32),
            pltpu.SemaphoreType.DMA((GATHER_SLOTS,)),
            pltpu.VMEM((D, D_FF), BF16),
            pltpu.VMEM((D, D_FF), BF16),
            pltpu.VMEM((D_FF, D), BF16),
        ],
    )
    return pl.pallas_call(
        _moe_kernel,
        out_shape=jax.ShapeDtypeStruct((P_ROWS, D), F32),
        grid_spec=grid_spec,
        compiler_params=_params(("arbitrary",)),
        name="moe_experts",
    )(tile_expert, tile_first, n_used, pads, dest, h2, w_gate, w_up, w_down)


def _dispatch(route_e, counts_lanes):
    counts = counts_lanes[0, N_GROUPS_MOE:N_GROUPS_MOE + N_EXPERTS]
    tiles = (counts + TE - 1) // TE
    eids = jnp.arange(N_EXPERTS, dtype=jnp.int32)
    tile_end = jnp.sum(jnp.where(eids[None, :] <= eids[:, None], tiles[None, :], 0), axis=1)
    tile_start = tile_end - tiles
    hot = route_e[:, :TOP_K, None] == eids[None, None, :]
    row0 = jnp.sum(jnp.where(hot, tile_start[None, None, :] * TE, 0), axis=-1)
    dest = (row0 + route_e[:, TOP_K:2 * TOP_K]).reshape(-1).astype(jnp.int32)
    tids = jnp.arange(MAX_TILES, dtype=jnp.int32)
    tile_expert = jnp.minimum(jnp.sum((tile_end[None, :] <= tids[:, None]).astype(jnp.int32), axis=1),
                              N_EXPERTS - 1)
    first_tile = jnp.sum(jnp.where(tile_expert[:, None] == eids[None, :], tile_start[None, :], 0), axis=1)
    tile_first = (tids == first_tile).astype(jnp.int32)
    n_used = tile_end[-1:].astype(jnp.int32)
    pads = jnp.concatenate([tile_start * TE + counts, tile_end * TE]).astype(jnp.int32)
    return dest, tile_expert, tile_first, n_used, pads


def _combine_kernel(final, tile0, dest_ref, ys_hbm, x_ref, rw_ref, g_ref, gtp, gts, *rest):
    if final:
        y_ref, ybuf, sem = rest
    else:
        scp, scs, shp, shs, x2_ref, h_ref, ybuf, sem = rest
    step = pl.program_id(0)
    nsteps = pl.num_programs(0)
    i = step + tile0

    def issue(tile, slot, r):
        for k in range(TOP_K):
            _row_gather_copy(ys_hbm, dest_ref[tile * (TK * TOP_K) + TOP_K * r + k],
                             ybuf.at[slot, k, pl.ds(r, 1)], sem.at[slot]).start()

    def wait_tile(slot):
        for k in range(TOP_K):
            pltpu.make_async_copy(ys_hbm.at[pl.ds(0, TK)], ybuf.at[slot, k], sem.at[slot]).wait()

    last = tile0 + nsteps - 1

    @pl.when(step == 0)
    def _():
        for k in range(GATHER_SLOTS - 1):
            def body(r, carry, k=k):
                issue(jnp.minimum(i + k, last), k, r)
                return carry

            lax.fori_loop(0, TK, body, 0, unroll=8)

    slot = step % GATHER_SLOTS
    wait_tile(slot)
    ahead = (step + GATHER_SLOTS - 1) % GATHER_SLOTS
    nxt = jnp.minimum(i + GATHER_SLOTS - 1, last)
    for r in range(TK):
        issue(nxt, ahead, r)
    gt = _mod_rows(i, TK, gtp, gts)
    rw = rw_ref[...]
    moe = rw[:, 0:1] * ybuf[slot, 0] + rw[:, 1:2] * ybuf[slot, 1]
    x2 = x_ref[...] + gt * moe
    xn = (x2 * _rms(x2)) * g_ref[...]
    if final:
        y_ref[...] = xn
    else:
        x2_ref[...] = x2
        sc = _mod_rows(i, TK, scp, scs)
        sh = _mod_rows(i, TK, shp, shs)
        h_ref[...] = (xn * (1.0 + sc) + sh).astype(BF16)

    @pl.when(step == nsteps - 1)
    def _():
        for k in range(1, GATHER_SLOTS):
            wait_tile((step + k) % GATHER_SLOTS)


def _combine(dest, ys, x1, route_w, g, mod, l, final, tile0=0, ntiles=NT // TK):
    row_in = lambda s, d: (s + tile0, 0)
    row_out = lambda s, d: (s, 0)
    nmod = lambda q, ll: [
        pl.BlockSpec((None, 8, D), lambda s, d: (ll, MOD_PROMPT_BLOCK, q)),
        pl.BlockSpec((None, DEC_BATCH, D), lambda s, d: (ll, 0, q)),
    ]
    in_specs = [pl.BlockSpec(memory_space=pl.ANY), pl.BlockSpec((TK, D), row_in),
                pl.BlockSpec((TK, LANES), row_in),
                pl.BlockSpec((1, D), lambda s, d: (0, 0))] + nmod(5, l)
    args = [ys, x1, route_w, g, mod, mod]
    rows = ntiles * TK
    if final:
        out_shape = jax.ShapeDtypeStruct((rows, D), F32)
        out_specs = pl.BlockSpec((TK, D), row_out)
    else:
        in_specs += nmod(1, l + 1) + nmod(0, l + 1)
        args += [mod, mod, mod, mod]
        out_shape = (jax.ShapeDtypeStruct((rows, D), F32), jax.ShapeDtypeStruct((rows, D), BF16))
        out_specs = (pl.BlockSpec((TK, D), row_out), pl.BlockSpec((TK, D), row_out))
    grid_spec = pltpu.PrefetchScalarGridSpec(
        num_scalar_prefetch=1,
        grid=(ntiles,),
        in_specs=in_specs,
        out_specs=out_specs,
        scratch_shapes=[pltpu.VMEM((GATHER_SLOTS, TOP_K, TK, D), F32),
                        pltpu.SemaphoreType.DMA((GATHER_SLOTS,))],
    )
    return pl.pallas_call(
        functools.partial(_combine_kernel, final, tile0),
        out_shape=out_shape,
        grid_spec=grid_spec,
        compiler_params=_params(("arbitrary",)),
        name=("combine_final_%d" % tile0) if final else "combine",
    )(dest, *args)


def kernel(x_prompt, x_sample, c_prompt, c_sample, state_conv, state_pool, w_ada, b_ada, g_norm1, w_in,
           conv_w, pool_w, pool_scale, g_v, w_s, b_s, w_br, w_o, g_norm2, w_rg, b_rg, w_re, b_re,
           w_gate, w_up, w_down, g_final):
    xp_rows = x_prompt.reshape(NP, D)
    xs_rows = x_sample.transpose(1, 0, 2).reshape(NS, D)
    sample_block0 = 0
    c_all = jnp.concatenate([c_sample, c_prompt,
                             jnp.zeros((MOD_ROWS - DEC_BATCH - BATCH, D), F32)], axis=0).astype(BF16)
    mod = _ada(c_all, w_ada, b_ada)

    w_gates_b = w_in[:, :, 6 * W:].astype(BF16)
    w_br_b = w_br.astype(BF16)
    w_o_b = w_o.astype(BF16)
    pool_w_b = pool_w.astype(BF16)
    tril = jnp.tril(jnp.ones((CHUNK, CHUNK), dtype=bool))
    ws_tril = jnp.where(tril[None, None], w_s, 0.0).astype(BF16)
    bs_tile = jnp.broadcast_to(b_s[:, :, :, None], (DEPTH, HEADS, CHUNK, CHUNK))
    small = jnp.tril(jnp.ones((DEC_SEQ, DEC_SEQ), dtype=bool))
    ws_small = jnp.where(small[None, None], w_s[:, :, :DEC_SEQ, :DEC_SEQ], 0.0)
    wvec = jnp.repeat(ws_small.transpose(0, 2, 3, 1).reshape(DEPTH, DEC_SEQ * DEC_SEQ, HEADS), CHUNK, axis=-1)
    bvec = jnp.repeat(b_s[:, :, :DEC_SEQ].transpose(0, 2, 1), CHUNK, axis=-1)
    w_r = jnp.concatenate([w_rg, w_re, jnp.zeros((DEPTH, D, LANES - N_GROUPS_MOE - N_EXPERTS), F32)], axis=-1)
    wr_hi = w_r.astype(BF16)
    wr_lo = (w_r - wr_hi.astype(F32)).astype(BF16)
    wr_cat = jnp.concatenate([wr_hi, wr_lo], axis=-1)
    b_r = jnp.concatenate([b_rg, b_re, jnp.zeros((DEPTH, LANES - N_GROUPS_MOE - N_EXPERTS), F32)],
                          axis=-1).reshape(DEPTH, 1, LANES)
    conv_t = state_conv.transpose(0, 2, 1, 3)
    pool_t = state_pool.transpose(0, 2, 1, 3)
    g1 = g_norm1.reshape(DEPTH, 1, D)
    g2 = g_norm2.reshape(DEPTH, 1, D)
    gv = g_v.reshape(DEPTH, 1, W)
    pscale = pool_scale.reshape(DEPTH, 1, W)

    h = _norm(xp_rows, xs_rows, g1, mod, 0)
    conv_p, pool_p, conv_s, pool_s, v_s = [], [], [], [], []
    y_prompt = y_sample = None
    for l in range(DEPTH):
        ya, cst_p = _conv_p(h, w_in, conv_w, l)
        yb, pst_p = _pool_p(h, w_in, pool_w_b, pscale, l)
        yc = _gmlp_p(h, w_in, gv, ws_tril, bs_tile, l)
        ya_s, yb_s, cst_s, pst_s = _mix_s(h, w_in, conv_w, pool_w_b, pscale, conv_t, pool_t, l)
        yc_s, vn_s = _gmlp_s(h, w_in, gv, wvec, bvec, l)
        merged = _merge(h, (ya, yb, yc), (ya_s, yb_s, yc_s), w_gates_b, w_br_b, l)
        x1, h2, route_w, route_e, counts = _wo(merged, w_o_b, xp_rows, xs_rows, sample_block0, g2, mod,
                                               wr_cat, wr_hi, b_r, l)
        dest, tile_expert, tile_first, n_used, pads = _dispatch(route_e, counts)
        ys = _moe(h2, tile_expert, tile_first, n_used, pads, dest, w_gate, w_up, w_down, l)
        if l + 1 < DEPTH:
            x, h = _combine(dest, ys, x1, route_w, g1[l + 1], mod, l, final=False)
            xp_rows = xs_rows = x
            sample_block0 = NP // TW
        else:
            gf = g_final.reshape(1, D)
            y_prompt = _combine(dest, ys, x1, route_w, gf, mod, l, final=True, tile0=0, ntiles=NP // TK)
            y_sample = _combine(dest, ys, x1, route_w, gf, mod, l, final=True, tile0=NP // TK,
                                ntiles=NS // TK)
        conv_p.append(cst_p[:, 8 - (CONV_K - 1):, :])
        pool_p.append(pst_p[:, 16 - POOL_STATE:, :])
        conv_s.append(cst_s.transpose(1, 0, 2))
        pool_s.append(pst_s.transpose(1, 0, 2))
        v_s.append(vn_s.reshape(DEC_SEQ, DEC_BATCH, W).transpose(1, 0, 2))

    y_prompt = y_prompt.reshape(BATCH, SEQ, D)
    y_sample = y_sample.reshape(DEC_SEQ, DEC_BATCH, D).transpose(1, 0, 2)
    return (y_prompt, y_sample, jnp.stack(conv_p), jnp.stack(pool_p), jnp.stack(conv_s),
            jnp.stack(pool_s), jnp.stack(v_s))
```

```python
import functools

import jax
import jax.numpy as jnp
from jax import lax
from jax.experimental import pallas as pl
from jax.experimental.pallas import tpu as pltpu

F32 = jnp.float32
BF16 = jnp.bfloat16

D = 2048
BATCH = 4
SEQ = 2048
DEPTH = 2
DEC_BATCH = 128
DEC_SEQ = 4
W = D // 2
CONV_K = 3
POOL_STATE = 15
N_POOL_GROUPS = 4
POOL_GROUP = W // N_POOL_GROUPS
CHUNK = 128
HEADS = W // CHUNK
IN_COLS = 6 * W + 3 * D
N_GROUPS_MOE = 4
EXP_PER_GROUP = 4
N_EXPERTS = 16
TOP_K = 2
D_FF = D // 4
EPS = 1e-6

NP = BATCH * SEQ
NS = DEC_BATCH * DEC_SEQ
NT = NP + NS
TM = 512
N_TILES = NT // TM
CB = 256
TE = 256
MAX_TILES = (NT * TOP_K) // TE + N_EXPERTS
P_ROWS = MAX_TILES * TE
TK = 256
TW = 256
GATHER_SLOTS = 3
LANES = 128
MOD_ROWS = 144
MOD_PROMPT_BLOCK = DEC_BATCH // 8

VMEM_LIMIT_V7X = 56 * 1024 * 1024


def _dot(a, b):
    return jnp.dot(a, b, preferred_element_type=F32)


def _params(sem, vmem=VMEM_LIMIT_V7X):
    return pltpu.CompilerParams(dimension_semantics=sem, vmem_limit_bytes=vmem)


def _sigmoid(x):
    return 1.0 / (1.0 + jnp.exp(-x))


def _rms(x):
    return lax.rsqrt(jnp.mean(x * x, axis=-1, keepdims=True) + EPS)


def _ada_kernel(c_ref, w_ref, b_ref, o_ref):
    o_ref[...] = _dot(c_ref[...], w_ref[...].astype(BF16)) + b_ref[...]


def _ada(c_all, w_ada, b_ada):
    nb = 1024
    return pl.pallas_call(
        _ada_kernel,
        out_shape=jax.ShapeDtypeStruct((DEPTH, MOD_ROWS, 6 * D), F32),
        grid=(DEPTH, 6 * D // nb),
        in_specs=[
            pl.BlockSpec((MOD_ROWS, D), lambda l, j: (0, 0)),
            pl.BlockSpec((None, D, nb), lambda l, j: (l, 0, j)),
            pl.BlockSpec((None, 1, nb), lambda l, j: (l, 0, j)),
        ],
        out_specs=pl.BlockSpec((None, MOD_ROWS, nb), lambda l, j: (l, 0, j)),
        compiler_params=_params(("arbitrary", "arbitrary")),
        name="ada",
    )(c_all, w_ada, b_ada.reshape(DEPTH, 1, 6 * D))


def _mod_specs(l, q, tile_rows):
    del tile_rows
    return [
        pl.BlockSpec((None, 8, D), lambda i, *_: (l, MOD_PROMPT_BLOCK, q)),
        pl.BlockSpec((None, DEC_BATCH, D), lambda i, *_: (l, 0, q)),
    ]


def _mod_rows(i, tile_rows, mp_ref, ms_ref):
    tiles_per_seq = SEQ // tile_rows
    is_sample = i >= NP // tile_rows
    b = jnp.minimum(i // tiles_per_seq, BATCH - 1)
    mp = mp_ref[pl.ds(b, 1), :]
    ms = jnp.concatenate([ms_ref[...]] * (tile_rows // DEC_BATCH), axis=0)
    return jnp.where(is_sample, ms, mp)


def _cast_kernel(w_ref, o_ref):
    o_ref[...] = w_ref[...].astype(BF16)


def _cast_gate_columns(w_in):
    nb = 1024
    col0 = 6 * W // nb
    return pl.pallas_call(
        _cast_kernel,
        out_shape=jax.ShapeDtypeStruct((DEPTH, D, 3 * D), BF16),
        grid=(DEPTH, 3 * D // nb),
        in_specs=[pl.BlockSpec((None, D, nb), lambda l, j: (l, 0, col0 + j))],
        out_specs=pl.BlockSpec((None, D, nb), lambda l, j: (l, 0, j)),
        compiler_params=_params(("arbitrary", "arbitrary")),
        name="cast_gates",
    )(w_in)


def _norm_kernel(xp_ref, xs_ref, g_ref, scp, scs, shp, shs, h_ref):
    i = pl.program_id(0)
    x = jnp.where(i >= NP // TM, xs_ref[...], xp_ref[...])
    sc = _mod_rows(i, TM, scp, scs)
    sh = _mod_rows(i, TM, shp, shs)
    h_ref[...] = (((x * _rms(x)) * g_ref[...]) * (1.0 + sc) + sh).astype(BF16)


def _norm(x_prompt_rows, x_sample_rows, g, mod, l):
    n_ptiles = NP // TM
    return pl.pallas_call(
        _norm_kernel,
        out_shape=jax.ShapeDtypeStruct((NT, D), BF16),
        grid=(N_TILES,),
        in_specs=[pl.BlockSpec((TM, D), lambda i: (jnp.minimum(i, n_ptiles - 1), 0)),
                  pl.BlockSpec((TM, D), lambda i: (jnp.maximum(i - n_ptiles, 0), 0)),
                  pl.BlockSpec((None, 1, D), lambda i: (l, 0, 0))]
                 + _mod_specs(l, 1, TM) + _mod_specs(l, 0, TM),
        out_specs=pl.BlockSpec((TM, D), lambda i: (i, 0)),
        compiler_params=_params(("arbitrary",)),
        name="norm1",
    )(x_prompt_rows, x_sample_rows, g, mod, mod, mod, mod)


def _shift_rows(a, s, row):
    return jnp.where(row >= s, pltpu.roll(a, s, 0), 0.0)


def _conv_p_kernel(h_ref, wb_ref, wc_ref, wh_ref, cw_ref, y_ref, st_ref):
    h = h_ref[...]
    z = _dot(h, wc_ref[...].astype(BF16)) * _dot(h, wh_ref[...].astype(BF16))
    row = lax.broadcasted_iota(jnp.int32, (SEQ, 1), 0)
    cw = cw_ref[...]
    y = cw[0:1] * _shift_rows(z, 2, row) + cw[1:2] * _shift_rows(z, 1, row) + cw[2:3] * z
    y_ref[...] = (_dot(h, wb_ref[...].astype(BF16)) * y).astype(BF16)
    st_ref[...] = z[SEQ - 8:, :]


def _conv_p(h, w_in, conv_w, l):
    nc = W // CB
    return pl.pallas_call(
        _conv_p_kernel,
        out_shape=(jax.ShapeDtypeStruct((NP, W), BF16),
                   jax.ShapeDtypeStruct((BATCH, 8, W), F32)),
        grid=(BATCH, nc),
        in_specs=[
            pl.BlockSpec((SEQ, D), lambda b, c: (b, 0)),
            pl.BlockSpec((None, D, CB), lambda b, c: (l, 0, c)),
            pl.BlockSpec((None, D, CB), lambda b, c: (l, 0, nc + c)),
            pl.BlockSpec((None, D, CB), lambda b, c: (l, 0, 2 * nc + c)),
            pl.BlockSpec((None, CONV_K, CB), lambda b, c: (l, 0, c)),
        ],
        out_specs=(pl.BlockSpec((SEQ, CB), lambda b, c: (b, c)),
                   pl.BlockSpec((None, 8, CB), lambda b, c: (b, 0, c))),
        compiler_params=_params(("arbitrary", "arbitrary")),
        name="conv_prompt",
    )(h, w_in, w_in, w_in, conv_w)


def _pool_p_kernel(h_ref, wp_ref, pw_ref, ps_ref, y_ref, st_ref):
    g = pl.program_id(1)
    p = _dot(h_ref[...], wp_ref[...].astype(BF16))
    row = lax.broadcasted_iota(jnp.int32, (SEQ, 1), 0)
    s = p + _shift_rows(p, 1, row)
    s = jnp.where(g >= 1, s + _shift_rows(s, 2, row), s)
    s = jnp.where(g >= 2, s + _shift_rows(s, 4, row), s)
    s = jnp.where(g >= 3, s + _shift_rows(s, 8, row), s)
    window = jnp.left_shift(2, g)
    cnt = jnp.minimum(row + 1, window).astype(F32)
    d = s / cnt - p
    y_ref[...] = (_dot(d.astype(BF16), pw_ref[...]) * ps_ref[...]).astype(BF16)
    st_ref[...] = p[SEQ - 16:, :]


def _pool_p(h, w_in, pool_w, pool_scale, l):
    col0 = 3 * W // CB
    return pl.pallas_call(
        _pool_p_kernel,
        out_shape=(jax.ShapeDtypeStruct((NP, W), BF16),
                   jax.ShapeDtypeStruct((BATCH, 16, W), F32)),
        grid=(BATCH, N_POOL_GROUPS),
        in_specs=[
            pl.BlockSpec((SEQ, D), lambda b, g: (b, 0)),
            pl.BlockSpec((None, D, CB), lambda b, g: (l, 0, col0 + g)),
            pl.BlockSpec((None, None, POOL_GROUP, POOL_GROUP), lambda b, g: (l, g, 0, 0)),
            pl.BlockSpec((None, 1, CB), lambda b, g: (l, 0, g)),
        ],
        out_specs=(pl.BlockSpec((SEQ, CB), lambda b, g: (b, g)),
                   pl.BlockSpec((None, 16, CB), lambda b, g: (b, 0, g))),
        compiler_params=_params(("arbitrary", "arbitrary")),
        name="pool_prompt",
    )(h, w_in, pool_w, pool_scale)


def _gmlp_p_kernel(h_ref, wv_ref, wu_ref, gv_ref, ws_ref, bs_ref, y_ref, vn_s):
    j = pl.program_id(1)

    @pl.when(j == 0)
    def _():
        wv = wv_ref[...].astype(BF16)
        for r0 in range(0, SEQ, TM):
            v = _dot(h_ref[r0:r0 + TM, :], wv)
            vn = (v * _rms(v)) * gv_ref[...]
            for hd in range(HEADS):
                vn_s[hd, r0:r0 + TM, :] = vn[:, hd * CHUNK:(hd + 1) * CHUNK].astype(BF16)

    u = _dot(h_ref[...], wu_ref[...].astype(BF16))
    heads_per_block = CB // CHUNK
    for hh in range(heads_per_block):
        head = heads_per_block * j + hh
        wsh = ws_ref[head]
        bsh = bs_ref[head]
        for n in range(SEQ // CHUNK):
            rows = slice(n * CHUNK, (n + 1) * CHUNK)
            cols = slice(hh * CHUNK, (hh + 1) * CHUNK)
            s = _dot(wsh, vn_s[head, pl.ds(n * CHUNK, CHUNK), :]) + bsh
            y_ref[rows, cols] = (u[rows, cols] * s).astype(BF16)


def _gmlp_p(h, w_in, g_v, ws_tril, bs_tile, l):
    ucol0 = 4 * W // CB
    return pl.pallas_call(
        _gmlp_p_kernel,
        out_shape=jax.ShapeDtypeStruct((NP, W), BF16),
        grid=(BATCH, W // CB),
        in_specs=[
            pl.BlockSpec((SEQ, D), lambda b, j: (b, 0)),
            pl.BlockSpec((None, D, W), lambda b, j: (l, 0, 5), pipeline_mode=pl.Buffered(1)),
            pl.BlockSpec((None, D, CB), lambda b, j: (l, 0, ucol0 + j)),
            pl.BlockSpec((None, 1, W), lambda b, j: (l, 0, 0)),
            pl.BlockSpec((None, HEADS, CHUNK, CHUNK), lambda b, j: (l, 0, 0, 0)),
            pl.BlockSpec((None, HEADS, CHUNK, CHUNK), lambda b, j: (l, 0, 0, 0)),
        ],
        out_specs=pl.BlockSpec((SEQ, CB), lambda b, j: (b, j)),
        scratch_shapes=[pltpu.VMEM((HEADS, SEQ, CHUNK), BF16)],
        compiler_params=_params(("arbitrary", "arbitrary")),
        name="gmlp_prompt",
    )(h, w_in, w_in, g_v, ws_tril, bs_tile)


def _mix_s_kernel(h_ref, wb_ref, wc_ref, wh_ref, wp_ref, cw_ref, pc_ref, pp_ref,
                  pw_ref, ps_ref, ya_ref, yb_ref, cst_ref, pst_ref):
    g = pl.program_id(0)
    h = h_ref[...]
    nb = DEC_BATCH
    z = _dot(h, wc_ref[...].astype(BF16)) * _dot(h, wh_ref[...].astype(BF16))
    bg = _dot(h, wb_ref[...].astype(BF16))
    zs = [pc_ref[0], pc_ref[1]] + [z[t * nb:(t + 1) * nb] for t in range(DEC_SEQ)]
    cw = cw_ref[...]
    for t in range(DEC_SEQ):
        y = cw[0:1] * zs[t] + cw[1:2] * zs[t + 1] + cw[2:3] * zs[t + 2]
        ya_ref[t * nb:(t + 1) * nb, :] = (bg[t * nb:(t + 1) * nb] * y).astype(BF16)
    cst_ref[0] = zs[DEC_SEQ]
    cst_ref[1] = zs[DEC_SEQ + 1]
    p = _dot(h, wp_ref[...].astype(BF16))
    pp = [pp_ref[k] for k in range(POOL_STATE)] + [p[t * nb:(t + 1) * nb] for t in range(DEC_SEQ)]
    window = jnp.left_shift(2, g).astype(F32)
    ds = []
    for t in range(DEC_SEQ):
        e = POOL_STATE + t
        s = pp[e] + pp[e - 1]
        s4 = s + (pp[e - 2] + pp[e - 3])
        s8 = s4 + ((pp[e - 4] + pp[e - 5]) + (pp[e - 6] + pp[e - 7]))
        s16 = s8 + (((pp[e - 8] + pp[e - 9]) + (pp[e - 10] + pp[e - 11]))
                    + ((pp[e - 12] + pp[e - 13]) + (pp[e - 14] + pp[e - 15])))
        s = jnp.where(g >= 1, s4, s)
        s = jnp.where(g >= 2, s8, s)
        s = jnp.where(g >= 3, s16, s)
        ds.append(s / window - pp[e])
    d = jnp.concatenate(ds, axis=0)
    yb_ref[...] = (_dot(d.astype(BF16), pw_ref[...]) * ps_ref[...]).astype(BF16)
    for k in range(POOL_STATE):
        pst_ref[k] = pp[DEC_SEQ + k]


def _mix_s(h, w_in, conv_w, pool_w, pool_scale, conv_t, pool_t, l):
    nc = W // CB
    srow = NP // NS
    return pl.pallas_call(
        _mix_s_kernel,
        out_shape=(jax.ShapeDtypeStruct((NS, W), BF16),
                   jax.ShapeDtypeStruct((NS, W), BF16),
                   jax.ShapeDtypeStruct((CONV_K - 1, DEC_BATCH, W), F32),
                   jax.ShapeDtypeStruct((POOL_STATE, DEC_BATCH, W), F32)),
        grid=(nc,),
        in_specs=[
            pl.BlockSpec((NS, D), lambda c: (srow, 0)),
            pl.BlockSpec((None, D, CB), lambda c: (l, 0, c)),
            pl.BlockSpec((None, D, CB), lambda c: (l, 0, nc + c)),
            pl.BlockSpec((None, D, CB), lambda c: (l, 0, 2 * nc + c)),
            pl.BlockSpec((None, D, CB), lambda c: (l, 0, 3 * nc + c)),
            pl.BlockSpec((None, CONV_K, CB), lambda c: (l, 0, c)),
            pl.BlockSpec((None, CONV_K - 1, DEC_BATCH, CB), lambda c: (l, 0, 0, c)),
            pl.BlockSpec((None, POOL_STATE, DEC_BATCH, CB), lambda c: (l, 0, 0, c)),
            pl.BlockSpec((None, None, POOL_GROUP, POOL_GROUP), lambda c: (l, c, 0, 0)),
            pl.BlockSpec((None, 1, CB), lambda c: (l, 0, c)),
        ],
        out_specs=(pl.BlockSpec((NS, CB), lambda c: (0, c)),
                   pl.BlockSpec((NS, CB), lambda c: (0, c)),
                   pl.BlockSpec((CONV_K - 1, DEC_BATCH, CB), lambda c: (0, 0, c)),
                   pl.BlockSpec((POOL_STATE, DEC_BATCH, CB), lambda c: (0, 0, c))),
        compiler_params=_params(("arbitrary",)),
        name="mix_sample",
    )(h, w_in, w_in, w_in, w_in, conv_w, conv_t, pool_t, pool_w, pool_scale)


def _gmlp_s_kernel(h_ref, wu_ref, wv_ref, gv_ref, wvec_ref, bvec_ref, yc_ref, vn_ref):
    h = h_ref[...]
    nb = DEC_BATCH
    v = _dot(h, wv_ref[...].astype(BF16))
    vn = (v * _rms(v)) * gv_ref[...]
    vn_ref[...] = vn
    u = _dot(h, wu_ref[...].astype(BF16))
    for t in range(DEC_SEQ):
        s = bvec_ref[t:t + 1, :]
        for sp in range(t + 1):
            k = t * DEC_SEQ + sp
            s = s + wvec_ref[k:k + 1, :] * vn[sp * nb:(sp + 1) * nb]
        yc_ref[t * nb:(t + 1) * nb, :] = (u[t * nb:(t + 1) * nb] * s).astype(BF16)


def _gmlp_s(h, w_in, g_v, wvec, bvec, l):
    srow = NP // NS
    return pl.pallas_call(
        _gmlp_s_kernel,
        out_shape=(jax.ShapeDtypeStruct((NS, W), BF16),
                   jax.ShapeDtypeStruct((NS, W), F32)),
        grid=(1,),
        in_specs=[
            pl.BlockSpec((NS, D), lambda i: (srow, 0)),
            pl.BlockSpec((None, D, W), lambda i: (l, 0, 4)),
            pl.BlockSpec((None, D, W), lambda i: (l, 0, 5)),
            pl.BlockSpec((None, 1, W), lambda i: (l, 0, 0)),
            pl.BlockSpec((None, DEC_SEQ * DEC_SEQ, W), lambda i: (l, 0, 0)),
            pl.BlockSpec((None, DEC_SEQ, W), lambda i: (l, 0, 0)),
        ],
        out_specs=(pl.BlockSpec((NS, W), lambda i: (0, 0)),
                   pl.BlockSpec((NS, W), lambda i: (0, 0))),
        compiler_params=_params(("arbitrary",)),
        name="gmlp_sample",
    )(h, w_in, w_in, g_v, wvec, bvec)


def _merge_kernel(h_ref, yap, ybp, ycp, yas, ybs, ycs, wg0, wg1, wg2, wbr_ref, o_ref):
    is_sample = pl.program_id(0) >= NP // TM
    h = h_ref[...]
    acc = None
    for n, (yp, ys, wg) in enumerate(((yap, yas, wg0), (ybp, ybs, wg1), (ycp, ycs, wg2))):
        y = jnp.where(is_sample, ys[...], yp[...])
        term = _sigmoid(_dot(h, wg[...])) * _dot(y, wbr_ref[n])
        acc = term if acc is None else acc + term
    o_ref[...] = acc.astype(BF16)


def _merge(h, y_prompt, y_sample, w_gates, w_br, l):
    db = 512
    g0 = 0
    gstep = D // db
    return pl.pallas_call(
        _merge_kernel,
        out_shape=jax.ShapeDtypeStruct((NT, D), BF16),
        grid=(N_TILES, D // db),
        in_specs=[
            pl.BlockSpec((TM, D), lambda i, d: (i, 0)),
            pl.BlockSpec((TM, W), lambda i, d: (jnp.minimum(i, NP // TM - 1), 0)),
            pl.BlockSpec((TM, W), lambda i, d: (jnp.minimum(i, NP // TM - 1), 0)),
            pl.BlockSpec((TM, W), lambda i, d: (jnp.minimum(i, NP // TM - 1), 0)),
            pl.BlockSpec((NS, W), lambda i, d: (0, 0)),
            pl.BlockSpec((NS, W), lambda i, d: (0, 0)),
            pl.BlockSpec((NS, W), lambda i, d: (0, 0)),
            pl.BlockSpec((None, D, db), lambda i, d: (l, 0, g0 + d)),
            pl.BlockSpec((None, D, db), lambda i, d: (l, 0, g0 + gstep + d)),
            pl.BlockSpec((None, D, db), lambda i, d: (l, 0, g0 + 2 * gstep + d)),
            pl.BlockSpec((None, 3, W, db), lambda i, d: (l, 0, 0, d)),
        ],
        out_specs=pl.BlockSpec((TM, db), lambda i, d: (i, d)),
        compiler_params=_params(("arbitrary", "arbitrary")),
        name="merge",
    )(h, *y_prompt, *y_sample, w_gates, w_gates, w_gates, w_br)


def _route(logits):
    lane = lax.broadcasted_iota(jnp.int32, logits.shape, 1).astype(F32)
    neg = -jnp.inf
    big = float(LANES)
    is_grp = lane < N_GROUPS_MOE
    gl = jnp.where(is_grp, logits, neg)
    gmax = jnp.max(gl, axis=-1, keepdims=True)
    gsel = jnp.min(jnp.where(gl == gmax, lane, big), axis=-1, keepdims=True)
    gp = 1.0 / jnp.sum(jnp.where(is_grp, jnp.exp(logits - gmax), 0.0), axis=-1, keepdims=True)
    lo = N_GROUPS_MOE + gsel * EXP_PER_GROUP
    in_grp = (lane >= lo) & (lane < lo + EXP_PER_GROUP)
    el = jnp.where(in_grp, logits, neg)
    m1 = jnp.max(el, axis=-1, keepdims=True)
    i1 = jnp.min(jnp.where(el == m1, lane, big), axis=-1, keepdims=True)
    el2 = jnp.where(lane == i1, neg, el)
    m2 = jnp.max(el2, axis=-1, keepdims=True)
    i2 = jnp.min(jnp.where(el2 == m2, lane, big), axis=-1, keepdims=True)
    e = jnp.exp(m2 - m1)
    w1 = gp / (1.0 + e)
    w2 = gp * (e / (1.0 + e))
    rw = jnp.where(lane == 0.0, w1, jnp.where(lane == 1.0, w2, 0.0))
    return rw, lane, i1, i2


def _wo_kernel(m_ref, wo_ref, xp_ref, xs_ref, g_ref, gtp, gts, scp, scs, shp, shs, wrc_ref, wrh_ref, br_ref,
               x1_ref, h2_ref, rw_ref, re_ref, cnt_ref, carry):
    i = pl.program_id(0)
    x = jnp.where(i >= NP // TW, xs_ref[...], xp_ref[...])
    gt = _mod_rows(i, TW, gtp, gts)
    x1 = x + gt * _dot(m_ref[...], wo_ref[...])
    x1_ref[...] = x1
    sc = _mod_rows(i, TW, scp, scs)
    sh = _mod_rows(i, TW, shp, shs)
    h2 = ((x1 * _rms(x1)) * g_ref[...]) * (1.0 + sc) + sh
    h2_ref[...] = h2.reshape(TW, D // LANES, LANES).astype(BF16)
    hi = h2.astype(BF16)
    lo = (h2 - hi.astype(F32)).astype(BF16)
    both = _dot(hi, wrc_ref[...])
    logits = (both[:, :LANES] + (both[:, LANES:] + _dot(lo, wrh_ref[...]))) + br_ref[...]
    rw, lane, i1, i2 = _route(logits)
    rw_ref[...] = rw

    @pl.when(i == 0)
    def _():
        carry[...] = jnp.zeros_like(carry)

    o1 = (lane == i1).astype(F32)
    o2 = (lane == i2).astype(F32)
    both_hot = o1 + o2
    r = lax.broadcasted_iota(jnp.int32, (TW, TW), 0)
    c = lax.broadcasted_iota(jnp.int32, (TW, TW), 1)
    before = (r > c).astype(BF16)
    seen = _dot(before, both_hot.astype(BF16)) + carry[...]
    rank1 = jnp.sum(seen * o1, axis=-1, keepdims=True)
    rank2 = jnp.sum(seen * o2, axis=-1, keepdims=True)
    total = carry[...] + jnp.sum(both_hot, axis=0, keepdims=True)
    carry[...] = total
    cnt_ref[...] = jnp.broadcast_to(total, cnt_ref.shape).astype(jnp.int32)
    re = jnp.where(lane == 0.0, i1 - N_GROUPS_MOE,
                   jnp.where(lane == 1.0, i2 - N_GROUPS_MOE,
                             jnp.where(lane == 2.0, rank1, jnp.where(lane == 3.0, rank2, 0.0))))
    re_ref[...] = re.astype(jnp.int32)


def _wo(merged, w_o, x_prompt_rows, x_sample_rows, sample_block0, g2, mod, wr_cat, wr_hi, b_r, l):
    row = lambda i: (i, 0)
    n_ptiles = NP // TW
    return pl.pallas_call(
        _wo_kernel,
        out_shape=(jax.ShapeDtypeStruct((NT, D), F32),
                   jax.ShapeDtypeStruct((NT, D // LANES, LANES), BF16),
                   jax.ShapeDtypeStruct((NT, LANES), F32),
                   jax.ShapeDtypeStruct((NT, LANES), jnp.int32),
                   jax.ShapeDtypeStruct((8, LANES), jnp.int32)),
        grid=(NT // TW,),
        in_specs=[
            pl.BlockSpec((TW, D), row),
            pl.BlockSpec((None, D, D), lambda i: (l, 0, 0)),
            pl.BlockSpec((TW, D), lambda i: (jnp.minimum(i, n_ptiles - 1), 0)),
            pl.BlockSpec((TW, D), lambda i: (sample_block0 + jnp.maximum(i - n_ptiles, 0), 0)),
            pl.BlockSpec((None, 1, D), lambda i: (l, 0, 0)),
        ] + _mod_specs(l, 2, TW) + _mod_specs(l, 4, TW) + _mod_specs(l, 3, TW) + [
            pl.BlockSpec((None, D, 2 * LANES), lambda i: (l, 0, 0)),
            pl.BlockSpec((None, D, LANES), lambda i: (l, 0, 0)),
            pl.BlockSpec((None, 1, LANES), lambda i: (l, 0, 0)),
        ],
        out_specs=(pl.BlockSpec((TW, D), row), pl.BlockSpec((TW, D // LANES, LANES), lambda i: (i, 0, 0)),
                   pl.BlockSpec((TW, LANES), row), pl.BlockSpec((TW, LANES), row),
                   pl.BlockSpec((8, LANES), lambda i: (0, 0))),
        scratch_shapes=[pltpu.VMEM((1, LANES), F32)],
        compiler_params=_params(("arbitrary",)),
        name="wo_router",
    )(merged, w_o, x_prompt_rows, x_sample_rows, g2, mod, mod, mod, mod, mod, mod, wr_cat, wr_hi, b_r)


def _row_gather_copy(src_hbm, idx, dst, sem):
    return pltpu.make_async_copy(src_hbm.at[pl.ds(idx, 1)], dst, sem)


def _moe_kernel(te_ref, tf_ref, nu_ref, pad_ref, dest_ref, h2_hbm, wg_ref, wu_ref, wd_ref, o_ref,
                src_ref, xbuf, sem, wg_s, wu_s, wd_s):
    del te_ref
    t = pl.program_id(0)
    n_used = nu_ref[0]

    def issue(tile, slot):
        base = tile * TE

        def body(r, carry):
            _row_gather_copy(h2_hbm, src_ref[base + r], xbuf.at[slot, pl.ds(r, 1)], sem.at[slot]).start()
            return carry

        lax.fori_loop(0, TE, body, 0, unroll=8)

    @pl.when(t == 0)
    def _():
        def clear(p, carry):
            src_ref[p] = 0
            return carry

        for e in range(N_EXPERTS):
            lax.fori_loop(pad_ref[e], pad_ref[N_EXPERTS + e], clear, 0)

        def invert(n, carry):
            for k in range(TOP_K):
                src_ref[dest_ref[TOP_K * n + k]] = n
            return carry

        lax.fori_loop(0, NT, invert, 0, unroll=8)
        for k in range(GATHER_SLOTS - 1):
            issue(jnp.minimum(k, n_used - 1), k)

    @pl.when((t < n_used) & (tf_ref[t] == 1))
    def _():
        wg_s[...] = wg_ref[...].astype(BF16)
        wu_s[...] = wu_ref[...].astype(BF16)
        wd_s[...] = wd_ref[...].astype(BF16)

    def wait_tile(slot):
        pltpu.make_async_copy(h2_hbm.at[pl.ds(0, TE)], xbuf.at[slot], sem.at[slot]).wait()

    @pl.when(t < n_used)
    def _():
        slot = t % GATHER_SLOTS
        wait_tile(slot)
        x = xbuf[slot].astype(F32).reshape(TE, D).astype(BF16)
        ahead = (t + GATHER_SLOTS - 1) % GATHER_SLOTS
        base = jnp.minimum(t + GATHER_SLOTS - 1, n_used - 1) * TE
        for r in range(TE):
            _row_gather_copy(h2_hbm, src_ref[base + r], xbuf.at[ahead, pl.ds(r, 1)], sem.at[ahead]).start()
        a = _dot(x, wg_s[...])
        b = _dot(x, wu_s[...])
        hid = (a * _sigmoid(a)) * b
        o_ref[...] = _dot(hid.astype(BF16), wd_s[...])

    @pl.when(t == n_used - 1)
    def _():
        for k in range(1, GATHER_SLOTS):
            wait_tile((t + k) % GATHER_SLOTS)

    @pl.when(t >= n_used)
    def _():
        o_ref[...] = jnp.zeros_like(o_ref)


def _moe(h2, tile_expert, tile_first, n_used, pads, dest, w_gate, w_up, w_down, l):
    grid_spec = pltpu.PrefetchScalarGridSpec(
        num_scalar_prefetch=5,
        grid=(MAX_TILES,),
        in_specs=[
            pl.BlockSpec(memory_space=pl.ANY),
            pl.BlockSpec((None, None, D, D_FF), lambda t, te, *_: (l, te[t], 0, 0)),
            pl.BlockSpec((None, None, D, D_FF), lambda t, te, *_: (l, te[t], 0, 0)),
            pl.BlockSpec((None, None, D_FF, D), lambda t, te, *_: (l, te[t], 0, 0)),
        ],
        out_specs=pl.BlockSpec((TE, D), lambda t, *_: (t, 0)),
        scratch_shapes=[
            pltpu.SMEM((P_ROWS,), jnp.int32),
            pltpu.VMEM((GATHER_SLOTS, TE, D // LANES, LANES), BF16),
            pltpu.SemaphoreType.DMA((GATHER_SLOTS,)),
            pltpu.VMEM((D, D_FF), BF16),
            pltpu.VMEM((D, D_FF), BF16),
            pltpu.VMEM((D_FF, D), BF16),
        ],
    )
    return pl.pallas_call(
        _moe_kernel,
        out_shape=jax.ShapeDtypeStruct((P_ROWS, D), F32),
        grid_spec=grid_spec,
        compiler_params=_params(("arbitrary",)),
        name="moe_experts",
    )(tile_expert, tile_first, n_used, pads, dest, h2, w_gate, w_up, w_down)


def _dispatch(route_e, counts_lanes):
    counts = counts_lanes[0, N_GROUPS_MOE:N_GROUPS_MOE + N_EXPERTS]
    tiles = (counts + TE - 1) // TE
    eids = jnp.arange(N_EXPERTS, dtype=jnp.int32)
    tile_end = jnp.sum(jnp.where(eids[None, :] <= eids[:, None], tiles[None, :], 0), axis=1)
    tile_start = tile_end - tiles
    hot = route_e[:, :TOP_K, None] == eids[None, None, :]
    row0 = jnp.sum(jnp.where(hot, tile_start[None, None, :] * TE, 0), axis=-1)
    dest = (row0 + route_e[:, TOP_K:2 * TOP_K]).reshape(-1).astype(jnp.int32)
    tids = jnp.arange(MAX_TILES, dtype=jnp.int32)
    tile_expert = jnp.minimum(jnp.sum((tile_end[None, :] <= tids[:, None]).astype(jnp.int32), axis=1),
                              N_EXPERTS - 1)
    first_tile = jnp.sum(jnp.where(tile_expert[:, None] == eids[None, :], tile_start[None, :], 0), axis=1)
    tile_first = (tids == first_tile).astype(jnp.int32)
    n_used = tile_end[-1:].astype(jnp.int32)
    pads = jnp.concatenate([tile_start * TE + counts, tile_end * TE]).astype(jnp.int32)
    return dest, tile_expert, tile_first, n_used, pads


def _combine_kernel(final, tile0, dest_ref, ys_hbm, x_ref, rw_ref, g_ref, gtp, gts, *rest):
    if final:
        y_ref, ybuf, sem = rest
    else:
        scp, scs, shp, shs, x2_ref, h_ref, ybuf, sem = rest
    step = pl.program_id(0)
    nsteps = pl.num_programs(0)
    i = step + tile0

    def issue(tile, slot, r):
        for k in range(TOP_K):
            _row_gather_copy(ys_hbm, dest_ref[tile * (TK * TOP_K) + TOP_K * r + k],
                             ybuf.at[slot, k, pl.ds(r, 1)], sem.at[slot]).start()

    def wait_tile(slot):
        for k in range(TOP_K):
            pltpu.make_async_copy(ys_hbm.at[pl.ds(0, TK)], ybuf.at[slot, k], sem.at[slot]).wait()

    last = tile0 + nsteps - 1

    @pl.when(step == 0)
    def _():
        for k in range(GATHER_SLOTS - 1):
            def body(r, carry, k=k):
                issue(jnp.minimum(i + k, last), k, r)
                return carry

            lax.fori_loop(0, TK, body, 0, unroll=8)

    slot = step % GATHER_SLOTS
    wait_tile(slot)
    ahead = (step + GATHER_SLOTS - 1) % GATHER_SLOTS
    nxt = jnp.minimum(i + GATHER_SLOTS - 1, last)
    for r in range(TK):
        issue(nxt, ahead, r)
    gt = _mod_rows(i, TK, gtp, gts)
    rw = rw_ref[...]
    moe = rw[:, 0:1] * ybuf[slot, 0] + rw[:, 1:2] * ybuf[slot, 1]
    x2 = x_ref[...] + gt * moe
    xn = (x2 * _rms(x2)) * g_ref[...]
    if final:
        y_ref[...] = xn
    else:
        x2_ref[...] = x2
        sc = _mod_rows(i, TK, scp, scs)
        sh = _mod_rows(i, TK, shp, shs)
        h_ref[...] = (xn * (1.0 + sc) + sh).astype(BF16)

    @pl.when(step == nsteps - 1)
    def _():
        for k in range(1, GATHER_SLOTS):
            wait_tile((step + k) % GATHER_SLOTS)


def _combine(dest, ys, x1, route_w, g, mod, l, final, tile0=0, ntiles=NT // TK):
    row_in = lambda s, d: (s + tile0, 0)
    row_out = lambda s, d: (s, 0)
    nmod = lambda q, ll: [
        pl.BlockSpec((None, 8, D), lambda s, d: (ll, MOD_PROMPT_BLOCK, q)),
        pl.BlockSpec((None, DEC_BATCH, D), lambda s, d: (ll, 0, q)),
    ]
    in_specs = [pl.BlockSpec(memory_space=pl.ANY), pl.BlockSpec((TK, D), row_in),
                pl.BlockSpec((TK, LANES), row_in),
                pl.BlockSpec((1, D), lambda s, d: (0, 0))] + nmod(5, l)
    args = [ys, x1, route_w, g, mod, mod]
    rows = ntiles * TK
    if final:
        out_shape = jax.ShapeDtypeStruct((rows, D), F32)
        out_specs = pl.BlockSpec((TK, D), row_out)
    else:
        in_specs += nmod(1, l + 1) + nmod(0, l + 1)
        args += [mod, mod, mod, mod]
        out_shape = (jax.ShapeDtypeStruct((rows, D), F32), jax.ShapeDtypeStruct((rows, D), BF16))
        out_specs = (pl.BlockSpec((TK, D), row_out), pl.BlockSpec((TK, D), row_out))
    grid_spec = pltpu.PrefetchScalarGridSpec(
        num_scalar_prefetch=1,
        grid=(ntiles,),
        in_specs=in_specs,
        out_specs=out_specs,
        scratch_shapes=[pltpu.VMEM((GATHER_SLOTS, TOP_K, TK, D), F32),
                        pltpu.SemaphoreType.DMA((GATHER_SLOTS,))],
    )
    return pl.pallas_call(
        functools.partial(_combine_kernel, final, tile0),
        out_shape=out_shape,
        grid_spec=grid_spec,
        compiler_params=_params(("arbitrary",)),
        name=("combine_final_%d" % tile0) if final else "combine",
    )(dest, *args)


def kernel(x_prompt, x_sample, c_prompt, c_sample, state_conv, state_pool, w_ada, b_ada, g_norm1, w_in,
           conv_w, pool_w, pool_scale, g_v, w_s, b_s, w_br, w_o, g_norm2, w_rg, b_rg, w_re, b_re,
           w_gate, w_up, w_down, g_final):
    xp_rows = x_prompt.reshape(NP, D)
    xs_rows = x_sample.transpose(1, 0, 2).reshape(NS, D)
    sample_block0 = 0
    c_all = jnp.concatenate([c_sample, c_prompt,
                             jnp.zeros((MOD_ROWS - DEC_BATCH - BATCH, D), F32)], axis=0).astype(BF16)
    mod = _ada(c_all, w_ada, b_ada)

    w_gates_b = _cast_gate_columns(w_in)
    w_br_b = w_br.astype(BF16)
    w_o_b = w_o.astype(BF16)
    pool_w_b = pool_w.astype(BF16)
    tril = jnp.tril(jnp.ones((CHUNK, CHUNK), dtype=bool))
    ws_tril = jnp.where(tril[None, None], w_s, 0.0).astype(BF16)
    bs_tile = jnp.broadcast_to(b_s[:, :, :, None], (DEPTH, HEADS, CHUNK, CHUNK))
    small = jnp.tril(jnp.ones((DEC_SEQ, DEC_SEQ), dtype=bool))
    ws_small = jnp.where(small[None, None], w_s[:, :, :DEC_SEQ, :DEC_SEQ], 0.0)
    wvec = jnp.repeat(ws_small.transpose(0, 2, 3, 1).reshape(DEPTH, DEC_SEQ * DEC_SEQ, HEADS), CHUNK, axis=-1)
    bvec = jnp.repeat(b_s[:, :, :DEC_SEQ].transpose(0, 2, 1), CHUNK, axis=-1)
    w_r = jnp.concatenate([w_rg, w_re, jnp.zeros((DEPTH, D, LANES - N_GROUPS_MOE - N_EXPERTS), F32)], axis=-1)
    wr_hi = w_r.astype(BF16)
    wr_lo = (w_r - wr_hi.astype(F32)).astype(BF16)
    wr_cat = jnp.concatenate([wr_hi, wr_lo], axis=-1)
    b_r = jnp.concatenate([b_rg, b_re, jnp.zeros((DEPTH, LANES - N_GROUPS_MOE - N_EXPERTS), F32)],
                          axis=-1).reshape(DEPTH, 1, LANES)
    conv_t = state_conv.transpose(0, 2, 1, 3)
    pool_t = state_pool.transpose(0, 2, 1, 3)
    g1 = g_norm1.reshape(DEPTH, 1, D)
    g2 = g_norm2.reshape(DEPTH, 1, D)
    gv = g_v.reshape(DEPTH, 1, W)
    pscale = pool_scale.reshape(DEPTH, 1, W)

    h = _norm(xp_rows, xs_rows, g1, mod, 0)
    conv_p, pool_p, conv_s, pool_s, v_s = [], [], [], [], []
    y_prompt = y_sample = None
    for l in range(DEPTH):
        ya, cst_p = _conv_p(h, w_in, conv_w, l)
        yb, pst_p = _pool_p(h, w_in, pool_w_b, pscale, l)
        yc = _gmlp_p(h, w_in, gv, ws_tril, bs_tile, l)
        ya_s, yb_s, cst_s, pst_s = _mix_s(h, w_in, conv_w, pool_w_b, pscale, conv_t, pool_t, l)
        yc_s, vn_s = _gmlp_s(h, w_in, gv, wvec, bvec, l)
        merged = _merge(h, (ya, yb, yc), (ya_s, yb_s, yc_s), w_gates_b, w_br_b, l)
        x1, h2, route_w, route_e, counts = _wo(merged, w_o_b, xp_rows, xs_rows, sample_block0, g2, mod,
                                               wr_cat, wr_hi, b_r, l)
        dest, tile_expert, tile_first, n_used, pads = _dispatch(route_e, counts)
        ys = _moe(h2, tile_expert, tile_first, n_used, pads, dest, w_gate, w_up, w_down, l)
        if l + 1 < DEPTH:
            x, h = _combine(dest, ys, x1, route_w, g1[l + 1], mod, l, final=False)
            xp_rows = xs_rows = x
            sample_block0 = NP // TW
        else:
            gf = g_final.reshape(1, D)
            y_prompt = _combine(dest, ys, x1, route_w, gf, mod, l, final=True, tile0=0, ntiles=NP // TK)
            y_sample = _combine(dest, ys, x1, route_w, gf, mod, l, final=True, tile0=NP // TK,
                                ntiles=NS // TK)
        conv_p.append(cst_p[:, 8 - (CONV_K - 1):, :])
        pool_p.append(pst_p[:, 16 - POOL_STATE:, :])
        conv_s.append(cst_s.transpose(1, 0, 2))
        pool_s.append(pst_s.transpose(1, 0, 2))
        v_s.append(vn_s.reshape(DEC_SEQ, DEC_BATCH, W).transpose(1, 0, 2))

    y_prompt = y_prompt.reshape(BATCH, SEQ, D)
    y_sample = y_sample.reshape(DEC_SEQ, DEC_BATCH, D).transpose(1, 0, 2)
    return (y_prompt, y_sample, jnp.stack(conv_p), jnp.stack(pool_p), jnp.stack(conv_s),
            jnp.stack(pool_s), jnp.stack(v_s))
```

```python
import functools

import jax
import jax.numpy as jnp
from jax import lax
from jax.experimental import pallas as pl
from jax.experimental.pallas import tpu as pltpu

F32 = jnp.float32
BF16 = jnp.bfloat16

D = 2048
BATCH = 4
SEQ = 2048
DEPTH = 2
DEC_BATCH = 128
DEC_SEQ = 4
W = D // 2
CONV_K = 3
POOL_STATE = 15
N_POOL_GROUPS = 4
POOL_GROUP = W // N_POOL_GROUPS
CHUNK = 128
HEADS = W // CHUNK
IN_COLS = 6 * W + 3 * D
N_GROUPS_MOE = 4
EXP_PER_GROUP = 4
N_EXPERTS = 16
TOP_K = 2
D_FF = D // 4
EPS = 1e-6

NP = BATCH * SEQ
NS = DEC_BATCH * DEC_SEQ
NT = NP + NS
TM = 512
N_TILES = NT // TM
CB = 256
TE = 256
MAX_TILES = (NT * TOP_K) // TE + N_EXPERTS
P_ROWS = MAX_TILES * TE
TK = 256
TW = 256
GATHER_SLOTS = 3
LANES = 128
MOD_ROWS = 144
MOD_PROMPT_BLOCK = DEC_BATCH // 8

VMEM_LIMIT_V7X = 56 * 1024 * 1024


def _dot(a, b):
    return jnp.dot(a, b, preferred_element_type=F32)


def _params(sem, vmem=VMEM_LIMIT_V7X):
    return pltpu.CompilerParams(dimension_semantics=sem, vmem_limit_bytes=vmem)


def _sigmoid(x):
    return 1.0 / (1.0 + jnp.exp(-x))


def _rms(x):
    return lax.rsqrt(jnp.mean(x * x, axis=-1, keepdims=True) + EPS)


def _ada_kernel(c_ref, w_ref, b_ref, o_ref):
    o_ref[...] = _dot(c_ref[...], w_ref[...].astype(BF16)) + b_ref[...]


def _ada(c_all, w_ada, b_ada):
    nb = 1024
    return pl.pallas_call(
        _ada_kernel,
        out_shape=jax.ShapeDtypeStruct((DEPTH, MOD_ROWS, 6 * D), F32),
        grid=(DEPTH, 6 * D // nb),
        in_specs=[
            pl.BlockSpec((MOD_ROWS, D), lambda l, j: (0, 0)),
            pl.BlockSpec((None, D, nb), lambda l, j: (l, 0, j)),
            pl.BlockSpec((None, 1, nb), lambda l, j: (l, 0, j)),
        ],
        out_specs=pl.BlockSpec((None, MOD_ROWS, nb), lambda l, j: (l, 0, j)),
        compiler_params=_params(("arbitrary", "arbitrary")),
        name="ada",
    )(c_all, w_ada, b_ada.reshape(DEPTH, 1, 6 * D))


def _mod_specs(l, q, tile_rows):
    del tile_rows
    return [
        pl.BlockSpec((None, 8, D), lambda i, *_: (l, MOD_PROMPT_BLOCK, q)),
        pl.BlockSpec((None, DEC_BATCH, D), lambda i, *_: (l, 0, q)),
    ]


def _mod_rows(i, tile_rows, mp_ref, ms_ref):
    tiles_per_seq = SEQ // tile_rows
    is_sample = i >= NP // tile_rows
    b = jnp.minimum(i // tiles_per_seq, BATCH - 1)
    mp = mp_ref[pl.ds(b, 1), :]
    ms = jnp.concatenate([ms_ref[...]] * (tile_rows // DEC_BATCH), axis=0)
    return jnp.where(is_sample, ms, mp)


def _norm_kernel(xp_ref, xs_ref, g_ref, scp, scs, shp, shs, h_ref):
    i = pl.program_id(0)
    x = jnp.where(i >= NP // TM, xs_ref[...], xp_ref[...])
    sc = _mod_rows(i, TM, scp, scs)
    sh = _mod_rows(i, TM, shp, shs)
    h_ref[...] = (((x * _rms(x)) * g_ref[...]) * (1.0 + sc) + sh).astype(BF16)


def _norm(x_prompt_rows, x_sample_rows, g, mod, l):
    n_ptiles = NP // TM
    return pl.pallas_call(
        _norm_kernel,
        out_shape=jax.ShapeDtypeStruct((NT, D), BF16),
        grid=(N_TILES,),
        in_specs=[pl.BlockSpec((TM, D), lambda i: (jnp.minimum(i, n_ptiles - 1), 0)),
                  pl.BlockSpec((TM, D), lambda i: (jnp.maximum(i - n_ptiles, 0), 0)),
                  pl.BlockSpec((None, 1, D), lambda i: (l, 0, 0))]
                 + _mod_specs(l, 1, TM) + _mod_specs(l, 0, TM),
        out_specs=pl.BlockSpec((TM, D), lambda i: (i, 0)),
        compiler_params=_params(("arbitrary",)),
        name="norm1",
    )(x_prompt_rows, x_sample_rows, g, mod, mod, mod, mod)


def _shift_rows(a, s, row):
    return jnp.where(row >= s, pltpu.roll(a, s, 0), 0.0)


def _conv_p_kernel(h_ref, wb_ref, wc_ref, wh_ref, cw_ref, wcast_ref, y_ref, st_ref, wcast_out):
    wcast_out[...] = wcast_ref[...].astype(BF16)
    h = h_ref[...]
    z = _dot(h, wc_ref[...].astype(BF16)) * _dot(h, wh_ref[...].astype(BF16))
    row = lax.broadcasted_iota(jnp.int32, (SEQ, 1), 0)
    cw = cw_ref[...]
    y = cw[0:1] * _shift_rows(z, 2, row) + cw[1:2] * _shift_rows(z, 1, row) + cw[2:3] * z
    y_ref[...] = (_dot(h, wb_ref[...].astype(BF16)) * y).astype(BF16)
    st_ref[...] = z[SEQ - 8:, :]


def _conv_p(h, w_in, conv_w, l):
    nc = W // CB
    steps = BATCH * nc
    gcols = 3 * D // steps
    g0 = 6 * W // gcols
    return pl.pallas_call(
        _conv_p_kernel,
        out_shape=(jax.ShapeDtypeStruct((NP, W), BF16),
                   jax.ShapeDtypeStruct((BATCH, 8, W), F32),
                   jax.ShapeDtypeStruct((D, 3 * D), BF16)),
        grid=(BATCH, nc),
        in_specs=[
            pl.BlockSpec((SEQ, D), lambda b, c: (b, 0)),
            pl.BlockSpec((None, D, CB), lambda b, c: (l, 0, c)),
            pl.BlockSpec((None, D, CB), lambda b, c: (l, 0, nc + c)),
            pl.BlockSpec((None, D, CB), lambda b, c: (l, 0, 2 * nc + c)),
            pl.BlockSpec((None, CONV_K, CB), lambda b, c: (l, 0, c)),
            pl.BlockSpec((None, D, gcols), lambda b, c: (l, 0, g0 + b * nc + c)),
        ],
        out_specs=(pl.BlockSpec((SEQ, CB), lambda b, c: (b, c)),
                   pl.BlockSpec((None, 8, CB), lambda b, c: (b, 0, c)),
                   pl.BlockSpec((D, gcols), lambda b, c: (0, b * nc + c))),
        compiler_params=_params(("arbitrary", "arbitrary")),
        name="conv_prompt",
    )(h, w_in, w_in, w_in, conv_w, w_in)


def _pool_p_kernel(h_ref, wp_ref, pw_ref, ps_ref, wcast_ref, y_ref, st_ref, wcast_out):
    wcast_out[...] = wcast_ref[...].astype(BF16)
    g = pl.program_id(1)
    p = _dot(h_ref[...], wp_ref[...].astype(BF16))
    row = lax.broadcasted_iota(jnp.int32, (SEQ, 1), 0)
    s = p + _shift_rows(p, 1, row)
    s = jnp.where(g >= 1, s + _shift_rows(s, 2, row), s)
    s = jnp.where(g >= 2, s + _shift_rows(s, 4, row), s)
    s = jnp.where(g >= 3, s + _shift_rows(s, 8, row), s)
    window = jnp.left_shift(2, g)
    cnt = jnp.minimum(row + 1, window).astype(F32)
    d = s / cnt - p
    y_ref[...] = (_dot(d.astype(BF16), pw_ref[...]) * ps_ref[...]).astype(BF16)
    st_ref[...] = p[SEQ - 16:, :]


def _pool_p(h, w_in, pool_w, pool_scale, w_br, l):
    col0 = 3 * W // CB
    bcols = D // (BATCH * N_POOL_GROUPS)
    return pl.pallas_call(
        _pool_p_kernel,
        out_shape=(jax.ShapeDtypeStruct((NP, W), BF16),
                   jax.ShapeDtypeStruct((BATCH, 16, W), F32),
                   jax.ShapeDtypeStruct((3, W, D), BF16)),
        grid=(BATCH, N_POOL_GROUPS),
        in_specs=[
            pl.BlockSpec((SEQ, D), lambda b, g: (b, 0)),
            pl.BlockSpec((None, D, CB), lambda b, g: (l, 0, col0 + g)),
            pl.BlockSpec((None, None, POOL_GROUP, POOL_GROUP), lambda b, g: (l, g, 0, 0)),
            pl.BlockSpec((None, 1, CB), lambda b, g: (l, 0, g)),
            pl.BlockSpec((None, 3, W, bcols), lambda b, g: (l, 0, 0, b * N_POOL_GROUPS + g)),
        ],
        out_specs=(pl.BlockSpec((SEQ, CB), lambda b, g: (b, g)),
                   pl.BlockSpec((None, 16, CB), lambda b, g: (b, 0, g)),
                   pl.BlockSpec((3, W, bcols), lambda b, g: (0, 0, b * N_POOL_GROUPS + g))),
        compiler_params=_params(("arbitrary", "arbitrary")),
        name="pool_prompt",
    )(h, w_in, pool_w, pool_scale, w_br)


def _gmlp_p_kernel(h_ref, wv_ref, wu_ref, gv_ref, ws_ref, bs_ref, wcast_ref, y_ref, wcast_out, vn_s):
    wcast_out[...] = wcast_ref[...].astype(BF16)
    j = pl.program_id(1)

    @pl.when(j == 0)
    def _():
        wv = wv_ref[...].astype(BF16)
        for r0 in range(0, SEQ, TM):
            v = _dot(h_ref[r0:r0 + TM, :], wv)
            vn = (v * _rms(v)) * gv_ref[...]
            for hd in range(HEADS):
                vn_s[hd, r0:r0 + TM, :] = vn[:, hd * CHUNK:(hd + 1) * CHUNK].astype(BF16)

    u = _dot(h_ref[...], wu_ref[...].astype(BF16))
    heads_per_block = CB // CHUNK
    for hh in range(heads_per_block):
        head = heads_per_block * j + hh
        wsh = ws_ref[head]
        bsh = bs_ref[head]
        for n in range(SEQ // CHUNK):
            rows = slice(n * CHUNK, (n + 1) * CHUNK)
            cols = slice(hh * CHUNK, (hh + 1) * CHUNK)
            s = _dot(wsh, vn_s[head, pl.ds(n * CHUNK, CHUNK), :]) + bsh
            y_ref[rows, cols] = (u[rows, cols] * s).astype(BF16)


def _gmlp_p(h, w_in, g_v, ws_tril, bs_tile, w_o, l):
    ucol0 = 4 * W // CB
    nj = W // CB
    ocols = D // (BATCH * nj)
    return pl.pallas_call(
        _gmlp_p_kernel,
        out_shape=(jax.ShapeDtypeStruct((NP, W), BF16), jax.ShapeDtypeStruct((D, D), BF16)),
        grid=(BATCH, nj),
        in_specs=[
            pl.BlockSpec((SEQ, D), lambda b, j: (b, 0)),
            pl.BlockSpec((None, D, W), lambda b, j: (l, 0, 5), pipeline_mode=pl.Buffered(1)),
            pl.BlockSpec((None, D, CB), lambda b, j: (l, 0, ucol0 + j)),
            pl.BlockSpec((None, 1, W), lambda b, j: (l, 0, 0)),
            pl.BlockSpec((None, HEADS, CHUNK, CHUNK), lambda b, j: (l, 0, 0, 0)),
            pl.BlockSpec((None, HEADS, CHUNK, CHUNK), lambda b, j: (l, 0, 0, 0)),
            pl.BlockSpec((None, D, ocols), lambda b, j: (l, 0, b * nj + j)),
        ],
        out_specs=(pl.BlockSpec((SEQ, CB), lambda b, j: (b, j)),
                   pl.BlockSpec((D, ocols), lambda b, j: (0, b * nj + j))),
        scratch_shapes=[pltpu.VMEM((HEADS, SEQ, CHUNK), BF16)],
        compiler_params=_params(("arbitrary", "arbitrary")),
        name="gmlp_prompt",
    )(h, w_in, w_in, g_v, ws_tril, bs_tile, w_o)


def _mix_s_kernel(h_ref, wb_ref, wc_ref, wh_ref, wp_ref, cw_ref, pc_ref, pp_ref,
                  pw_ref, ps_ref, ya_ref, yb_ref, cst_ref, pst_ref):
    g = pl.program_id(0)
    h = h_ref[...]
    nb = DEC_BATCH
    z = _dot(h, wc_ref[...].astype(BF16)) * _dot(h, wh_ref[...].astype(BF16))
    bg = _dot(h, wb_ref[...].astype(BF16))
    zs = [pc_ref[0], pc_ref[1]] + [z[t * nb:(t + 1) * nb] for t in range(DEC_SEQ)]
    cw = cw_ref[...]
    for t in range(DEC_SEQ):
        y = cw[0:1] * zs[t] + cw[1:2] * zs[t + 1] + cw[2:3] * zs[t + 2]
        ya_ref[t * nb:(t + 1) * nb, :] = (bg[t * nb:(t + 1) * nb] * y).astype(BF16)
    cst_ref[0] = zs[DEC_SEQ]
    cst_ref[1] = zs[DEC_SEQ + 1]
    p = _dot(h, wp_ref[...].astype(BF16))
    pp = [pp_ref[k] for k in range(POOL_STATE)] + [p[t * nb:(t + 1) * nb] for t in range(DEC_SEQ)]
    window = jnp.left_shift(2, g).astype(F32)
    ds = []
    for t in range(DEC_SEQ):
        e = POOL_STATE + t
        s = pp[e] + pp[e - 1]
        s4 = s + (pp[e - 2] + pp[e - 3])
        s8 = s4 + ((pp[e - 4] + pp[e - 5]) + (pp[e - 6] + pp[e - 7]))
        s16 = s8 + (((pp[e - 8] + pp[e - 9]) + (pp[e - 10] + pp[e - 11]))
                    + ((pp[e - 12] + pp[e - 13]) + (pp[e - 14] + pp[e - 15])))
        s = jnp.where(g >= 1, s4, s)
        s = jnp.where(g >= 2, s8, s)
        s = jnp.where(g >= 3, s16, s)
        ds.append(s / window - pp[e])
    d = jnp.concatenate(ds, axis=0)
    yb_ref[...] = (_dot(d.astype(BF16), pw_ref[...]) * ps_ref[...]).astype(BF16)
    for k in range(POOL_STATE):
        pst_ref[k] = pp[DEC_SEQ + k]


def _mix_s(h, w_in, conv_w, pool_w, pool_scale, conv_t, pool_t, l):
    nc = W // CB
    srow = NP // NS
    return pl.pallas_call(
        _mix_s_kernel,
        out_shape=(jax.ShapeDtypeStruct((NS, W), BF16),
                   jax.ShapeDtypeStruct((NS, W), BF16),
                   jax.ShapeDtypeStruct((CONV_K - 1, DEC_BATCH, W), F32),
                   jax.ShapeDtypeStruct((POOL_STATE, DEC_BATCH, W), F32)),
        grid=(nc,),
        in_specs=[
            pl.BlockSpec((NS, D), lambda c: (srow, 0)),
            pl.BlockSpec((None, D, CB), lambda c: (l, 0, c)),
            pl.BlockSpec((None, D, CB), lambda c: (l, 0, nc + c)),
            pl.BlockSpec((None, D, CB), lambda c: (l, 0, 2 * nc + c)),
            pl.BlockSpec((None, D, CB), lambda c: (l, 0, 3 * nc + c)),
            pl.BlockSpec((None, CONV_K, CB), lambda c: (l, 0, c)),
            pl.BlockSpec((None, CONV_K - 1, DEC_BATCH, CB), lambda c: (l, 0, 0, c)),
            pl.BlockSpec((None, POOL_STATE, DEC_BATCH, CB), lambda c: (l, 0, 0, c)),
            pl.BlockSpec((None, None, POOL_GROUP, POOL_GROUP), lambda c: (l, c, 0, 0)),
            pl.BlockSpec((None, 1, CB), lambda c: (l, 0, c)),
        ],
        out_specs=(pl.BlockSpec((NS, CB), lambda c: (0, c)),
                   pl.BlockSpec((NS, CB), lambda c: (0, c)),
                   pl.BlockSpec((CONV_K - 1, DEC_BATCH, CB), lambda c: (0, 0, c)),
                   pl.BlockSpec((POOL_STATE, DEC_BATCH, CB), lambda c: (0, 0, c))),
        compiler_params=_params(("arbitrary",)),
        name="mix_sample",
    )(h, w_in, w_in, w_in, w_in, conv_w, conv_t, pool_t, pool_w, pool_scale)


def _gmlp_s_kernel(h_ref, wu_ref, wv_ref, gv_ref, wvec_ref, bvec_ref, yc_ref, vn_ref):
    h = h_ref[...]
    nb = DEC_BATCH
    v = _dot(h, wv_ref[...].astype(BF16))
    vn = (v * _rms(v)) * gv_ref[...]
    vn_ref[...] = vn
    u = _dot(h, wu_ref[...].astype(BF16))
    for t in range(DEC_SEQ):
        s = bvec_ref[t:t + 1, :]
        for sp in range(t + 1):
            k = t * DEC_SEQ + sp
            s = s + wvec_ref[k:k + 1, :] * vn[sp * nb:(sp + 1) * nb]
        yc_ref[t * nb:(t + 1) * nb, :] = (u[t * nb:(t + 1) * nb] * s).astype(BF16)


def _gmlp_s(h, w_in, g_v, wvec, bvec, l):
    srow = NP // NS
    return pl.pallas_call(
        _gmlp_s_kernel,
        out_shape=(jax.ShapeDtypeStruct((NS, W), BF16),
                   jax.ShapeDtypeStruct((NS, W), F32)),
        grid=(1,),
        in_specs=[
            pl.BlockSpec((NS, D), lambda i: (srow, 0)),
            pl.BlockSpec((None, D, W), lambda i: (l, 0, 4)),
            pl.BlockSpec((None, D, W), lambda i: (l, 0, 5)),
            pl.BlockSpec((None, 1, W), lambda i: (l, 0, 0)),
            pl.BlockSpec((None, DEC_SEQ * DEC_SEQ, W), lambda i: (l, 0, 0)),
            pl.BlockSpec((None, DEC_SEQ, W), lambda i: (l, 0, 0)),
        ],
        out_specs=(pl.BlockSpec((NS, W), lambda i: (0, 0)),
                   pl.BlockSpec((NS, W), lambda i: (0, 0))),
        compiler_params=_params(("arbitrary",)),
        name="gmlp_sample",
    )(h, w_in, w_in, g_v, wvec, bvec)


def _merge_kernel(h_ref, yap, ybp, ycp, yas, ybs, ycs, wg0, wg1, wg2, wbr_ref, o_ref):
    is_sample = pl.program_id(0) >= NP // TM
    h = h_ref[...]
    acc = None
    for n, (yp, ys, wg) in enumerate(((yap, yas, wg0), (ybp, ybs, wg1), (ycp, ycs, wg2))):
        y = jnp.where(is_sample, ys[...], yp[...])
        term = _sigmoid(_dot(h, wg[...])) * _dot(y, wbr_ref[n])
        acc = term if acc is None else acc + term
    o_ref[...] = acc.astype(BF16)


def _merge(h, y_prompt, y_sample, w_gates, w_br):
    db = 512
    g0 = 0
    gstep = D // db
    return pl.pallas_call(
        _merge_kernel,
        out_shape=jax.ShapeDtypeStruct((NT, D), BF16),
        grid=(N_TILES, D // db),
        in_specs=[
            pl.BlockSpec((TM, D), lambda i, d: (i, 0)),
            pl.BlockSpec((TM, W), lambda i, d: (jnp.minimum(i, NP // TM - 1), 0)),
            pl.BlockSpec((TM, W), lambda i, d: (jnp.minimum(i, NP // TM - 1), 0)),
            pl.BlockSpec((TM, W), lambda i, d: (jnp.minimum(i, NP // TM - 1), 0)),
            pl.BlockSpec((NS, W), lambda i, d: (0, 0)),
            pl.BlockSpec((NS, W), lambda i, d: (0, 0)),
            pl.BlockSpec((NS, W), lambda i, d: (0, 0)),
            pl.BlockSpec((D, db), lambda i, d: (0, g0 + d)),
            pl.BlockSpec((D, db), lambda i, d: (0, g0 + gstep + d)),
            pl.BlockSpec((D, db), lambda i, d: (0, g0 + 2 * gstep + d)),
            pl.BlockSpec((3, W, db), lambda i, d: (0, 0, d)),
        ],
        out_specs=pl.BlockSpec((TM, db), lambda i, d: (i, d)),
        compiler_params=_params(("arbitrary", "arbitrary")),
        name="merge",
    )(h, *y_prompt, *y_sample, w_gates, w_gates, w_gates, w_br)


def _route(logits):
    lane = lax.broadcasted_iota(jnp.int32, logits.shape, 1).astype(F32)
    neg = -jnp.inf
    big = float(LANES)
    is_grp = lane < N_GROUPS_MOE
    gl = jnp.where(is_grp, logits, neg)
    gmax = jnp.max(gl, axis=-1, keepdims=True)
    gsel = jnp.min(jnp.where(gl == gmax, lane, big), axis=-1, keepdims=True)
    gp = 1.0 / jnp.sum(jnp.where(is_grp, jnp.exp(logits - gmax), 0.0), axis=-1, keepdims=True)
    lo = N_GROUPS_MOE + gsel * EXP_PER_GROUP
    in_grp = (lane >= lo) & (lane < lo + EXP_PER_GROUP)
    el = jnp.where(in_grp, logits, neg)
    m1 = jnp.max(el, axis=-1, keepdims=True)
    i1 = jnp.min(jnp.where(el == m1, lane, big), axis=-1, keepdims=True)
    el2 = jnp.where(lane == i1, neg, el)
    m2 = jnp.max(el2, axis=-1, keepdims=True)
    i2 = jnp.min(jnp.where(el2 == m2, lane, big), axis=-1, keepdims=True)
    e = jnp.exp(m2 - m1)
    w1 = gp / (1.0 + e)
    w2 = gp * (e / (1.0 + e))
    rw = jnp.where(lane == 0.0, w1, jnp.where(lane == 1.0, w2, 0.0))
    return rw, lane, i1, i2


def _wo_kernel(m_ref, wo_ref, xp_ref, xs_ref, g_ref, gtp, gts, scp, scs, shp, shs, wrc_ref, wrh_ref, br_ref,
               x1_ref, h2_ref, rw_ref, re_ref, cnt_ref, carry):
    i = pl.program_id(0)
    x = jnp.where(i >= NP // TW, xs_ref[...], xp_ref[...])
    gt = _mod_rows(i, TW, gtp, gts)
    x1 = x + gt * _dot(m_ref[...], wo_ref[...])
    x1_ref[...] = x1
    sc = _mod_rows(i, TW, scp, scs)
    sh = _mod_rows(i, TW, shp, shs)
    h2 = ((x1 * _rms(x1)) * g_ref[...]) * (1.0 + sc) + sh
    h2_ref[...] = h2.reshape(TW, D // LANES, LANES).astype(BF16)
    hi = h2.astype(BF16)
    lo = (h2 - hi.astype(F32)).astype(BF16)
    both = _dot(hi, wrc_ref[...])
    logits = (both[:, :LANES] + (both[:, LANES:] + _dot(lo, wrh_ref[...]))) + br_ref[...]
    rw, lane, i1, i2 = _route(logits)
    rw_ref[...] = rw

    @pl.when(i == 0)
    def _():
        carry[...] = jnp.zeros_like(carry)

    o1 = (lane == i1).astype(F32)
    o2 = (lane == i2).astype(F32)
    both_hot = o1 + o2
    r = lax.broadcasted_iota(jnp.int32, (TW, TW), 0)
    c = lax.broadcasted_iota(jnp.int32, (TW, TW), 1)
    before = (r > c).astype(BF16)
    seen = _dot(before, both_hot.astype(BF16)) + carry[...]
    rank1 = jnp.sum(seen * o1, axis=-1, keepdims=True)
    rank2 = jnp.sum(seen * o2, axis=-1, keepdims=True)
    total = carry[...] + jnp.sum(both_hot, axis=0, keepdims=True)
    carry[...] = total
    cnt_ref[...] = jnp.broadcast_to(total, cnt_ref.shape).astype(jnp.int32)
    re = jnp.where(lane == 0.0, i1 - N_GROUPS_MOE,
                   jnp.where(lane == 1.0, i2 - N_GROUPS_MOE,
                             jnp.where(lane == 2.0, rank1, jnp.where(lane == 3.0, rank2, 0.0))))
    re_ref[...] = re.astype(jnp.int32)


def _wo(merged, w_o, x_prompt_rows, x_sample_rows, sample_block0, g2, mod, wr_cat, wr_hi, b_r, l):
    row = lambda i: (i, 0)
    n_ptiles = NP // TW
    return pl.pallas_call(
        _wo_kernel,
        out_shape=(jax.ShapeDtypeStruct((NT, D), F32),
                   jax.ShapeDtypeStruct((NT, D // LANES, LANES), BF16),
                   jax.ShapeDtypeStruct((NT, LANES), F32),
                   jax.ShapeDtypeStruct((NT, LANES), jnp.int32),
                   jax.ShapeDtypeStruct((8, LANES), jnp.int32)),
        grid=(NT // TW,),
        in_specs=[
            pl.BlockSpec((TW, D), row),
            pl.BlockSpec((D, D), lambda i: (0, 0)),
            pl.BlockSpec((TW, D), lambda i: (jnp.minimum(i, n_ptiles - 1), 0)),
            pl.BlockSpec((TW, D), lambda i: (sample_block0 + jnp.maximum(i - n_ptiles, 0), 0)),
            pl.BlockSpec((None, 1, D), lambda i: (l, 0, 0)),
        ] + _mod_specs(l, 2, TW) + _mod_specs(l, 4, TW) + _mod_specs(l, 3, TW) + [
            pl.BlockSpec((None, D, 2 * LANES), lambda i: (l, 0, 0)),
            pl.BlockSpec((None, D, LANES), lambda i: (l, 0, 0)),
            pl.BlockSpec((None, 1, LANES), lambda i: (l, 0, 0)),
        ],
        out_specs=(pl.BlockSpec((TW, D), row), pl.BlockSpec((TW, D // LANES, LANES), lambda i: (i, 0, 0)),
                   pl.BlockSpec((TW, LANES), row), pl.BlockSpec((TW, LANES), row),
                   pl.BlockSpec((8, LANES), lambda i: (0, 0))),
        scratch_shapes=[pltpu.VMEM((1, LANES), F32)],
        compiler_params=_params(("arbitrary",)),
        name="wo_router",
    )(merged, w_o, x_prompt_rows, x_sample_rows, g2, mod, mod, mod, mod, mod, mod, wr_cat, wr_hi, b_r)


def _row_gather_copy(src_hbm, idx, dst, sem):
    return pltpu.make_async_copy(src_hbm.at[pl.ds(idx, 1)], dst, sem)


def _moe_kernel(te_ref, tf_ref, nu_ref, pad_ref, dest_ref, h2_hbm, wg_ref, wu_ref, wd_ref, o_ref,
                src_ref, xbuf, sem, wg_s, wu_s, wd_s):
    del te_ref
    t = pl.program_id(0)
    n_used = nu_ref[0]

    def issue(tile, slot):
        base = tile * TE

        def body(r, carry):
            _row_gather_copy(h2_hbm, src_ref[base + r], xbuf.at[slot, pl.ds(r, 1)], sem.at[slot]).start()
            return carry

        lax.fori_loop(0, TE, body, 0, unroll=8)

    @pl.when(t == 0)
    def _():
        def clear(p, carry):
            src_ref[p] = 0
            return carry

        for e in range(N_EXPERTS):
            lax.fori_loop(pad_ref[e], pad_ref[N_EXPERTS + e], clear, 0)

        def invert(n, carry):
            for k in range(TOP_K):
                src_ref[dest_ref[TOP_K * n + k]] = n
            return carry

        lax.fori_loop(0, NT, invert, 0, unroll=8)
        for k in range(GATHER_SLOTS - 1):
            issue(jnp.minimum(k, n_used - 1), k)

    @pl.when((t < n_used) & (tf_ref[t] == 1))
    def _():
        wg_s[...] = wg_ref[...].astype(BF16)
        wu_s[...] = wu_ref[...].astype(BF16)
        wd_s[...] = wd_ref[...].astype(BF16)

    def wait_tile(slot):
        pltpu.make_async_copy(h2_hbm.at[pl.ds(0, TE)], xbuf.at[slot], sem.at[slot]).wait()

    @pl.when(t < n_used)
    def _():
        slot = t % GATHER_SLOTS
        wait_tile(slot)
        x = xbuf[slot].astype(F32).reshape(TE, D).astype(BF16)
        ahead = (t + GATHER_SLOTS - 1) % GATHER_SLOTS
        base = jnp.minimum(t + GATHER_SLOTS - 1, n_used - 1) * TE
        for r in range(TE):
            _row_gather_copy(h2_hbm, src_ref[base + r], xbuf.at[ahead, pl.ds(r, 1)], sem.at[ahead]).start()
        a = _dot(x, wg_s[...])
        b = _dot(x, wu_s[...])
        hid = (a * _sigmoid(a)) * b
        o_ref[...] = _dot(hid.astype(BF16), wd_s[...])

    @pl.when(t == n_used - 1)
    def _():
        for k in range(1, GATHER_SLOTS):
            wait_tile((t + k) % GATHER_SLOTS)

    @pl.when(t >= n_used)
    def _():
        o_ref[...] = jnp.zeros_like(o_ref)


def _moe(h2, tile_expert, tile_first, n_used, pads, dest, w_gate, w_up, w_down, l):
    grid_spec = pltpu.PrefetchScalarGridSpec(
        num_scalar_prefetch=5,
        grid=(MAX_TILES,),
        in_specs=[
            pl.BlockSpec(memory_space=pl.ANY),
            pl.BlockSpec((None, None, D, D_FF), lambda t, te, *_: (l, te[t], 0, 0)),
            pl.BlockSpec((None, None, D, D_FF), lambda t, te, *_: (l, te[t], 0, 0)),
            pl.BlockSpec((None, None, D_FF, D), lambda t, te, *_: (l, te[t], 0, 0)),
        ],
        out_specs=pl.BlockSpec((TE, D), lambda t, *_: (t, 0)),
        scratch_shapes=[
            pltpu.SMEM((P_ROWS,), jnp.int32),
            pltpu.VMEM((GATHER_SLOTS, TE, D // LANES, LANES), BF16),
            pltpu.SemaphoreType.DMA((GATHER_SLOTS,)),
            pltpu.VMEM((D, D_FF), BF16),
            pltpu.VMEM((D, D_FF), BF16),
            pltpu.VMEM((D_FF, D), BF16),
        ],
    )
    return pl.pallas_call(
        _moe_kernel,
        out_shape=jax.ShapeDtypeStruct((P_ROWS, D), F32),
        grid_spec=grid_spec,
        compiler_params=_params(("arbitrary",)),
        name="moe_experts",
    )(tile_expert, tile_first, n_used, pads, dest, h2, w_gate, w_up, w_down)


def _dispatch(route_e, counts_lanes):
    counts = counts_lanes[0, N_GROUPS_MOE:N_GROUPS_MOE + N_EXPERTS]
    tiles = (counts + TE - 1) // TE
    eids = jnp.arange(N_EXPERTS, dtype=jnp.int32)
    tile_end = jnp.sum(jnp.where(eids[None, :] <= eids[:, None], tiles[None, :], 0), axis=1)
    tile_start = tile_end - tiles
    hot = route_e[:, :TOP_K, None] == eids[None, None, :]
    row0 = jnp.sum(jnp.where(hot, tile_start[None, None, :] * TE, 0), axis=-1)
    dest = (row0 + route_e[:, TOP_K:2 * TOP_K]).reshape(-1).astype(jnp.int32)
    tids = jnp.arange(MAX_TILES, dtype=jnp.int32)
    tile_expert = jnp.minimum(jnp.sum((tile_end[None, :] <= tids[:, None]).astype(jnp.int32), axis=1),
                              N_EXPERTS - 1)
    first_tile = jnp.sum(jnp.where(tile_expert[:, None] == eids[None, :], tile_start[None, :], 0), axis=1)
    tile_first = (tids == first_tile).astype(jnp.int32)
    n_used = tile_end[-1:].astype(jnp.int32)
    pads = jnp.concatenate([tile_start * TE + counts, tile_end * TE]).astype(jnp.int32)
    return dest, tile_expert, tile_first, n_used, pads


def _combine_kernel(final, tile0, dest_ref, ys_hbm, x_ref, rw_ref, g_ref, gtp, gts, *rest):
    if final:
        y_ref, ybuf, sem = rest
    else:
        scp, scs, shp, shs, x2_ref, h_ref, ybuf, sem = rest
    step = pl.program_id(0)
    nsteps = pl.num_programs(0)
    i = step + tile0

    def issue(tile, slot, r):
        for k in range(TOP_K):
            _row_gather_copy(ys_hbm, dest_ref[tile * (TK * TOP_K) + TOP_K * r + k],
                             ybuf.at[slot, k, pl.ds(r, 1)], sem.at[slot]).start()

    def wait_tile(slot):
        for k in range(TOP_K):
            pltpu.make_async_copy(ys_hbm.at[pl.ds(0, TK)], ybuf.at[slot, k], sem.at[slot]).wait()

    last = tile0 + nsteps - 1

    @pl.when(step == 0)
    def _():
        for k in range(GATHER_SLOTS - 1):
            def body(r, carry, k=k):
                issue(jnp.minimum(i + k, last), k, r)
                return carry

            lax.fori_loop(0, TK, body, 0, unroll=8)

    slot = step % GATHER_SLOTS
    wait_tile(slot)
    ahead = (step + GATHER_SLOTS - 1) % GATHER_SLOTS
    nxt = jnp.minimum(i + GATHER_SLOTS - 1, last)
    for r in range(TK):
        issue(nxt, ahead, r)
    gt = _mod_rows(i, TK, gtp, gts)
    rw = rw_ref[...]
    moe = rw[:, 0:1] * ybuf[slot, 0] + rw[:, 1:2] * ybuf[slot, 1]
    x2 = x_ref[...] + gt * moe
    xn = (x2 * _rms(x2)) * g_ref[...]
    if final:
        y_ref[...] = xn
    else:
        x2_ref[...] = x2
        sc = _mod_rows(i, TK, scp, scs)
        sh = _mod_rows(i, TK, shp, shs)
        h_ref[...] = (xn * (1.0 + sc) + sh).astype(BF16)

    @pl.when(step == nsteps - 1)
    def _():
        for k in range(1, GATHER_SLOTS):
            wait_tile((step + k) % GATHER_SLOTS)


def _combine(dest, ys, x1, route_w, g, mod, l, final, tile0=0, ntiles=NT // TK):
    row_in = lambda s, d: (s + tile0, 0)
    row_out = lambda s, d: (s, 0)
    nmod = lambda q, ll: [
        pl.BlockSpec((None, 8, D), lambda s, d: (ll, MOD_PROMPT_BLOCK, q)),
        pl.BlockSpec((None, DEC_BATCH, D), lambda s, d: (ll, 0, q)),
    ]
    in_specs = [pl.BlockSpec(memory_space=pl.ANY), pl.BlockSpec((TK, D), row_in),
                pl.BlockSpec((TK, LANES), row_in),
                pl.BlockSpec((1, D), lambda s, d: (0, 0))] + nmod(5, l)
    args = [ys, x1, route_w, g, mod, mod]
    rows = ntiles * TK
    if final:
        out_shape = jax.ShapeDtypeStruct((rows, D), F32)
        out_specs = pl.BlockSpec((TK, D), row_out)
    else:
        in_specs += nmod(1, l + 1) + nmod(0, l + 1)
        args += [mod, mod, mod, mod]
        out_shape = (jax.ShapeDtypeStruct((rows, D), F32), jax.ShapeDtypeStruct((rows, D), BF16))
        out_specs = (pl.BlockSpec((TK, D), row_out), pl.BlockSpec((TK, D), row_out))
    grid_spec = pltpu.PrefetchScalarGridSpec(
        num_scalar_prefetch=1,
        grid=(ntiles,),
        in_specs=in_specs,
        out_specs=out_specs,
        scratch_shapes=[pltpu.VMEM((GATHER_SLOTS, TOP_K, TK, D), F32),
                        pltpu.SemaphoreType.DMA((GATHER_SLOTS,))],
    )
    return pl.pallas_call(
        functools.partial(_combine_kernel, final, tile0),
        out_shape=out_shape,
        grid_spec=grid_spec,
        compiler_params=_params(("arbitrary",)),
        name=("combine_final_%d" % tile0) if final else "combine",
    )(dest, *args)


def kernel(x_prompt, x_sample, c_prompt, c_sample, state_conv, state_pool, w_ada, b_ada, g_norm1, w_in,
           conv_w, pool_w, pool_scale, g_v, w_s, b_s, w_br, w_o, g_norm2, w_rg, b_rg, w_re, b_re,
           w_gate, w_up, w_down, g_final):
    xp_rows = x_prompt.reshape(NP, D)
    xs_rows = x_sample.transpose(1, 0, 2).reshape(NS, D)
    sample_block0 = 0
    c_all = jnp.concatenate([c_sample, c_prompt,
                             jnp.zeros((MOD_ROWS - DEC_BATCH - BATCH, D), F32)], axis=0).astype(BF16)
    mod = _ada(c_all, w_ada, b_ada)

    pool_w_b = pool_w.astype(BF16)
    tril = jnp.tril(jnp.ones((CHUNK, CHUNK), dtype=bool))
    ws_tril = jnp.where(tril[None, None], w_s, 0.0).astype(BF16)
    bs_tile = jnp.broadcast_to(b_s[:, :, :, None], (DEPTH, HEADS, CHUNK, CHUNK))
    small = jnp.tril(jnp.ones((DEC_SEQ, DEC_SEQ), dtype=bool))
    ws_small = jnp.where(small[None, None], w_s[:, :, :DEC_SEQ, :DEC_SEQ], 0.0)
    wvec = jnp.repeat(ws_small.transpose(0, 2, 3, 1).reshape(DEPTH, DEC_SEQ * DEC_SEQ, HEADS), CHUNK, axis=-1)
    bvec = jnp.repeat(b_s[:, :, :DEC_SEQ].transpose(0, 2, 1), CHUNK, axis=-1)
    w_r = jnp.concatenate([w_rg, w_re, jnp.zeros((DEPTH, D, LANES - N_GROUPS_MOE - N_EXPERTS), F32)], axis=-1)
    wr_hi = w_r.astype(BF16)
    wr_lo = (w_r - wr_hi.astype(F32)).astype(BF16)
    wr_cat = jnp.concatenate([wr_hi, wr_lo], axis=-1)
    b_r = jnp.concatenate([b_rg, b_re, jnp.zeros((DEPTH, LANES - N_GROUPS_MOE - N_EXPERTS), F32)],
                          axis=-1).reshape(DEPTH, 1, LANES)
    conv_t = state_conv.transpose(0, 2, 1, 3)
    pool_t = state_pool.transpose(0, 2, 1, 3)
    g1 = g_norm1.reshape(DEPTH, 1, D)
    g2 = g_norm2.reshape(DEPTH, 1, D)
    gv = g_v.reshape(DEPTH, 1, W)
    pscale = pool_scale.reshape(DEPTH, 1, W)

    h = _norm(xp_rows, xs_rows, g1, mod, 0)
    conv_p, pool_p, conv_s, pool_s, v_s = [], [], [], [], []
    y_prompt = y_sample = None
    for l in range(DEPTH):
        ya, cst_p, w_gates_b = _conv_p(h, w_in, conv_w, l)
        yb, pst_p, w_br_b = _pool_p(h, w_in, pool_w_b, pscale, w_br, l)
        yc, w_o_b = _gmlp_p(h, w_in, gv, ws_tril, bs_tile, w_o, l)
        ya_s, yb_s, cst_s, pst_s = _mix_s(h, w_in, conv_w, pool_w_b, pscale, conv_t, pool_t, l)
        yc_s, vn_s = _gmlp_s(h, w_in, gv, wvec, bvec, l)
        merged = _merge(h, (ya, yb, yc), (ya_s, yb_s, yc_s), w_gates_b, w_br_b)
        x1, h2, route_w, route_e, counts = _wo(merged, w_o_b, xp_rows, xs_rows, sample_block0, g2, mod,
                                               wr_cat, wr_hi, b_r, l)
        dest, tile_expert, tile_first, n_used, pads = _dispatch(route_e, counts)
        ys = _moe(h2, tile_expert, tile_first, n_used, pads, dest, w_gate, w_up, w_down, l)
        if l + 1 < DEPTH:
            x, h = _combine(dest, ys, x1, route_w, g1[l + 1], mod, l, final=False)
            xp_rows = xs_rows = x
            sample_block0 = NP // TW
        else:
            gf = g_final.reshape(1, D)
            y_prompt = _combine(dest, ys, x1, route_w, gf, mod, l, final=True, tile0=0, ntiles=NP // TK)
            y_sample = _combine(dest, ys, x1, route_w, gf, mod, l, final=True, tile0=NP // TK,
                                ntiles=NS // TK)
        conv_p.append(cst_p[:, 8 - (CONV_K - 1):, :])
        pool_p.append(pst_p[:, 16 - POOL_STATE:, :])
        conv_s.append(cst_s.transpose(1, 0, 2))
        pool_s.append(pst_s.transpose(1, 0, 2))
        v_s.append(vn_s.reshape(DEC_SEQ, DEC_BATCH, W).transpose(1, 0, 2))

    y_prompt = y_prompt.reshape(BATCH, SEQ, D)
    y_sample = y_sample.reshape(DEC_SEQ, DEC_BATCH, D).transpose(1, 0, 2)
    return (y_prompt, y_sample, jnp.stack(conv_p), jnp.stack(pool_p), jnp.stack(conv_s),
            jnp.stack(pool_s), jnp.stack(v_s))
```

```python
import functools

import jax
import jax.numpy as jnp
from jax import lax
from jax.experimental import pallas as pl
from jax.experimental.pallas import tpu as pltpu

F32 = jnp.float32
BF16 = jnp.bfloat16

D = 2048
BATCH = 4
SEQ = 2048
DEPTH = 2
DEC_BATCH = 128
DEC_SEQ = 4
W = D // 2
CONV_K = 3
POOL_STATE = 15
N_POOL_GROUPS = 4
POOL_GROUP = W // N_POOL_GROUPS
CHUNK = 128
HEADS = W // CHUNK
IN_COLS = 6 * W + 3 * D
N_GROUPS_MOE = 4
EXP_PER_GROUP = 4
N_EXPERTS = 16
TOP_K = 2
D_FF = D // 4
EPS = 1e-6

NP = BATCH * SEQ
NS = DEC_BATCH * DEC_SEQ
NT = NP + NS
TM = 512
N_TILES = NT // TM
CB = 256
TE = 256
MAX_TILES = (NT * TOP_K) // TE + N_EXPERTS
P_ROWS = MAX_TILES * TE
TK = 256
TW = 256
GATHER_SLOTS = 3
MOE_SLOTS = 4
LANES = 128
MOD_ROWS = 144
MOD_PROMPT_BLOCK = DEC_BATCH // 8

VMEM_LIMIT_V7X = 56 * 1024 * 1024


def _dot(a, b):
    return jnp.dot(a, b, preferred_element_type=F32)


def _params(sem, vmem=VMEM_LIMIT_V7X):
    return pltpu.CompilerParams(dimension_semantics=sem, vmem_limit_bytes=vmem)


def _sigmoid(x):
    return 1.0 / (1.0 + jnp.exp(-x))


def _rms(x):
    return lax.rsqrt(jnp.mean(x * x, axis=-1, keepdims=True) + EPS)


def _ada_kernel(c_ref, w_ref, b_ref, o_ref):
    o_ref[...] = _dot(c_ref[...], w_ref[...].astype(BF16)) + b_ref[...]


def _ada(c_all, w_ada, b_ada):
    nb = 1024
    return pl.pallas_call(
        _ada_kernel,
        out_shape=jax.ShapeDtypeStruct((DEPTH, MOD_ROWS, 6 * D), F32),
        grid=(DEPTH, 6 * D // nb),
        in_specs=[
            pl.BlockSpec((MOD_ROWS, D), lambda l, j: (0, 0)),
            pl.BlockSpec((None, D, nb), lambda l, j: (l, 0, j)),
            pl.BlockSpec((None, 1, nb), lambda l, j: (l, 0, j)),
        ],
        out_specs=pl.BlockSpec((None, MOD_ROWS, nb), lambda l, j: (l, 0, j)),
        compiler_params=_params(("arbitrary", "arbitrary")),
        name="ada",
    )(c_all, w_ada, b_ada.reshape(DEPTH, 1, 6 * D))


def _mod_specs(l, q, tile_rows):
    del tile_rows
    return [
        pl.BlockSpec((None, 8, D), lambda i, *_: (l, MOD_PROMPT_BLOCK, q)),
        pl.BlockSpec((None, DEC_BATCH, D), lambda i, *_: (l, 0, q)),
    ]


def _mod_rows(i, tile_rows, mp_ref, ms_ref):
    tiles_per_seq = SEQ // tile_rows
    is_sample = i >= NP // tile_rows
    b = jnp.minimum(i // tiles_per_seq, BATCH - 1)
    mp = mp_ref[pl.ds(b, 1), :]
    ms = jnp.concatenate([ms_ref[...]] * (tile_rows // DEC_BATCH), axis=0)
    return jnp.where(is_sample, ms, mp)


def _norm_kernel(xp_ref, xs_ref, g_ref, scp, scs, shp, shs, h_ref):
    i = pl.program_id(0)
    x = jnp.where(i >= NP // TM, xs_ref[...], xp_ref[...])
    sc = _mod_rows(i, TM, scp, scs)
    sh = _mod_rows(i, TM, shp, shs)
    h_ref[...] = (((x * _rms(x)) * g_ref[...]) * (1.0 + sc) + sh).astype(BF16)


def _norm(x_prompt_rows, x_sample_rows, g, mod, l):
    n_ptiles = NP // TM
    return pl.pallas_call(
        _norm_kernel,
        out_shape=jax.ShapeDtypeStruct((NT, D), BF16),
        grid=(N_TILES,),
        in_specs=[pl.BlockSpec((TM, D), lambda i: (jnp.minimum(i, n_ptiles - 1), 0)),
                  pl.BlockSpec((TM, D), lambda i: (jnp.maximum(i - n_ptiles, 0), 0)),
                  pl.BlockSpec((None, 1, D), lambda i: (l, 0, 0))]
                 + _mod_specs(l, 1, TM) + _mod_specs(l, 0, TM),
        out_specs=pl.BlockSpec((TM, D), lambda i: (i, 0)),
        compiler_params=_params(("arbitrary",)),
        name="norm1",
    )(x_prompt_rows, x_sample_rows, g, mod, mod, mod, mod)


def _shift_rows(a, s, row):
    return jnp.where(row >= s, pltpu.roll(a, s, 0), 0.0)


def _conv_p_kernel(h_ref, wb_ref, wc_ref, wh_ref, cw_ref, wcast_ref, y_ref, st_ref, wcast_out):
    wcast_out[...] = wcast_ref[...].astype(BF16)
    h = h_ref[...]
    z = _dot(h, wc_ref[...].astype(BF16)) * _dot(h, wh_ref[...].astype(BF16))
    row = lax.broadcasted_iota(jnp.int32, (SEQ, 1), 0)
    cw = cw_ref[...]
    y = cw[0:1] * _shift_rows(z, 2, row) + cw[1:2] * _shift_rows(z, 1, row) + cw[2:3] * z
    y_ref[...] = (_dot(h, wb_ref[...].astype(BF16)) * y).astype(BF16)
    st_ref[...] = z[SEQ - 8:, :]


def _conv_p(h, w_in, conv_w, l):
    nc = W // CB
    steps = BATCH * nc
    gcols = 3 * D // steps
    g0 = 6 * W // gcols
    return pl.pallas_call(
        _conv_p_kernel,
        out_shape=(jax.ShapeDtypeStruct((NP, W), BF16),
                   jax.ShapeDtypeStruct((BATCH, 8, W), F32),
                   jax.ShapeDtypeStruct((D, 3 * D), BF16)),
        grid=(BATCH, nc),
        in_specs=[
            pl.BlockSpec((SEQ, D), lambda b, c: (b, 0)),
            pl.BlockSpec((None, D, CB), lambda b, c: (l, 0, c)),
            pl.BlockSpec((None, D, CB), lambda b, c: (l, 0, nc + c)),
            pl.BlockSpec((None, D, CB), lambda b, c: (l, 0, 2 * nc + c)),
            pl.BlockSpec((None, CONV_K, CB), lambda b, c: (l, 0, c)),
            pl.BlockSpec((None, D, gcols), lambda b, c: (l, 0, g0 + b * nc + c)),
        ],
        out_specs=(pl.BlockSpec((SEQ, CB), lambda b, c: (b, c)),
                   pl.BlockSpec((None, 8, CB), lambda b, c: (b, 0, c)),
                   pl.BlockSpec((D, gcols), lambda b, c: (0, b * nc + c))),
        compiler_params=_params(("arbitrary", "arbitrary")),
        name="conv_prompt",
    )(h, w_in, w_in, w_in, conv_w, w_in)


def _pool_p_kernel(h_ref, wp_ref, pw_ref, ps_ref, wcast_ref, y_ref, st_ref, wcast_out):
    wcast_out[...] = wcast_ref[...].astype(BF16)
    g = pl.program_id(1)
    p = _dot(h_ref[...], wp_ref[...].astype(BF16))
    row = lax.broadcasted_iota(jnp.int32, (SEQ, 1), 0)
    s = p + _shift_rows(p, 1, row)
    s = jnp.where(g >= 1, s + _shift_rows(s, 2, row), s)
    s = jnp.where(g >= 2, s + _shift_rows(s, 4, row), s)
    s = jnp.where(g >= 3, s + _shift_rows(s, 8, row), s)
    window = jnp.left_shift(2, g)
    cnt = jnp.minimum(row + 1, window).astype(F32)
    d = s / cnt - p
    y_ref[...] = (_dot(d.astype(BF16), pw_ref[...]) * ps_ref[...]).astype(BF16)
    st_ref[...] = p[SEQ - 16:, :]


def _pool_p(h, w_in, pool_w, pool_scale, w_br, l):
    col0 = 3 * W // CB
    bcols = D // (BATCH * N_POOL_GROUPS)
    return pl.pallas_call(
        _pool_p_kernel,
        out_shape=(jax.ShapeDtypeStruct((NP, W), BF16),
                   jax.ShapeDtypeStruct((BATCH, 16, W), F32),
                   jax.ShapeDtypeStruct((3, W, D), BF16)),
        grid=(BATCH, N_POOL_GROUPS),
        in_specs=[
            pl.BlockSpec((SEQ, D), lambda b, g: (b, 0)),
            pl.BlockSpec((None, D, CB), lambda b, g: (l, 0, col0 + g)),
            pl.BlockSpec((None, None, POOL_GROUP, POOL_GROUP), lambda b, g: (l, g, 0, 0)),
            pl.BlockSpec((None, 1, CB), lambda b, g: (l, 0, g)),
            pl.BlockSpec((None, 3, W, bcols), lambda b, g: (l, 0, 0, b * N_POOL_GROUPS + g)),
        ],
        out_specs=(pl.BlockSpec((SEQ, CB), lambda b, g: (b, g)),
                   pl.BlockSpec((None, 16, CB), lambda b, g: (b, 0, g)),
                   pl.BlockSpec((3, W, bcols), lambda b, g: (0, 0, b * N_POOL_GROUPS + g))),
        compiler_params=_params(("arbitrary", "arbitrary")),
        name="pool_prompt",
    )(h, w_in, pool_w, pool_scale, w_br)


def _gmlp_p_kernel(h_ref, wv_ref, wu_ref, gv_ref, ws_ref, bs_ref, wcast_ref, y_ref, wcast_out, vn_s):
    wcast_out[...] = wcast_ref[...].astype(BF16)
    j = pl.program_id(1)

    @pl.when(j == 0)
    def _():
        wv = wv_ref[...].astype(BF16)
        for r0 in range(0, SEQ, TM):
            v = _dot(h_ref[r0:r0 + TM, :], wv)
            vn = (v * _rms(v)) * gv_ref[...]
            for hd in range(HEADS):
                vn_s[hd, r0:r0 + TM, :] = vn[:, hd * CHUNK:(hd + 1) * CHUNK].astype(BF16)

    u = _dot(h_ref[...], wu_ref[...].astype(BF16))
    heads_per_block = CB // CHUNK
    for hh in range(heads_per_block):
        head = heads_per_block * j + hh
        wsh = ws_ref[head]
        bsh = bs_ref[head]
        for n in range(SEQ // CHUNK):
            rows = slice(n * CHUNK, (n + 1) * CHUNK)
            cols = slice(hh * CHUNK, (hh + 1) * CHUNK)
            s = _dot(wsh, vn_s[head, pl.ds(n * CHUNK, CHUNK), :]) + bsh
            y_ref[rows, cols] = (u[rows, cols] * s).astype(BF16)


def _gmlp_p(h, w_in, g_v, ws_tril, bs_tile, w_o, l):
    ucol0 = 4 * W // CB
    nj = W // CB
    ocols = D // (BATCH * nj)
    return pl.pallas_call(
        _gmlp_p_kernel,
        out_shape=(jax.ShapeDtypeStruct((NP, W), BF16), jax.ShapeDtypeStruct((D, D), BF16)),
        grid=(BATCH, nj),
        in_specs=[
            pl.BlockSpec((SEQ, D), lambda b, j: (b, 0)),
            pl.BlockSpec((None, D, W), lambda b, j: (l, 0, 5), pipeline_mode=pl.Buffered(1)),
            pl.BlockSpec((None, D, CB), lambda b, j: (l, 0, ucol0 + j)),
            pl.BlockSpec((None, 1, W), lambda b, j: (l, 0, 0)),
            pl.BlockSpec((None, HEADS, CHUNK, CHUNK), lambda b, j: (l, 0, 0, 0)),
            pl.BlockSpec((None, HEADS, CHUNK, CHUNK), lambda b, j: (l, 0, 0, 0)),
            pl.BlockSpec((None, D, ocols), lambda b, j: (l, 0, b * nj + j)),
        ],
        out_specs=(pl.BlockSpec((SEQ, CB), lambda b, j: (b, j)),
                   pl.BlockSpec((D, ocols), lambda b, j: (0, b * nj + j))),
        scratch_shapes=[pltpu.VMEM((HEADS, SEQ, CHUNK), BF16)],
        compiler_params=_params(("arbitrary", "arbitrary")),
        name="gmlp_prompt",
    )(h, w_in, w_in, g_v, ws_tril, bs_tile, w_o)


def _mix_s_kernel(h_ref, wb_ref, wc_ref, wh_ref, wp_ref, cw_ref, pc_ref, pp_ref,
                  pw_ref, ps_ref, ya_ref, yb_ref, cst_ref, pst_ref):
    g = pl.program_id(0)
    h = h_ref[...]
    nb = DEC_BATCH
    z = _dot(h, wc_ref[...].astype(BF16)) * _dot(h, wh_ref[...].astype(BF16))
    bg = _dot(h, wb_ref[...].astype(BF16))
    zs = [pc_ref[0], pc_ref[1]] + [z[t * nb:(t + 1) * nb] for t in range(DEC_SEQ)]
    cw = cw_ref[...]
    for t in range(DEC_SEQ):
        y = cw[0:1] * zs[t] + cw[1:2] * zs[t + 1] + cw[2:3] * zs[t + 2]
        ya_ref[t * nb:(t + 1) * nb, :] = (bg[t * nb:(t + 1) * nb] * y).astype(BF16)
    cst_ref[0] = zs[DEC_SEQ]
    cst_ref[1] = zs[DEC_SEQ + 1]
    p = _dot(h, wp_ref[...].astype(BF16))
    pp = [pp_ref[k] for k in range(POOL_STATE)] + [p[t * nb:(t + 1) * nb] for t in range(DEC_SEQ)]
    window = jnp.left_shift(2, g).astype(F32)
    ds = []
    for t in range(DEC_SEQ):
        e = POOL_STATE + t
        s = pp[e] + pp[e - 1]
        s4 = s + (pp[e - 2] + pp[e - 3])
        s8 = s4 + ((pp[e - 4] + pp[e - 5]) + (pp[e - 6] + pp[e - 7]))
        s16 = s8 + (((pp[e - 8] + pp[e - 9]) + (pp[e - 10] + pp[e - 11]))
                    + ((pp[e - 12] + pp[e - 13]) + (pp[e - 14] + pp[e - 15])))
        s = jnp.where(g >= 1, s4, s)
        s = jnp.where(g >= 2, s8, s)
        s = jnp.where(g >= 3, s16, s)
        ds.append(s / window - pp[e])
    d = jnp.concatenate(ds, axis=0)
    yb_ref[...] = (_dot(d.astype(BF16), pw_ref[...]) * ps_ref[...]).astype(BF16)
    for k in range(POOL_STATE):
        pst_ref[k] = pp[DEC_SEQ + k]


def _mix_s(h, w_in, conv_w, pool_w, pool_scale, conv_t, pool_t, l):
    nc = W // CB
    srow = NP // NS
    return pl.pallas_call(
        _mix_s_kernel,
        out_shape=(jax.ShapeDtypeStruct((NS, W), BF16),
                   jax.ShapeDtypeStruct((NS, W), BF16),
                   jax.ShapeDtypeStruct((CONV_K - 1, DEC_BATCH, W), F32),
                   jax.ShapeDtypeStruct((POOL_STATE, DEC_BATCH, W), F32)),
        grid=(nc,),
        in_specs=[
            pl.BlockSpec((NS, D), lambda c: (srow, 0)),
            pl.BlockSpec((None, D, CB), lambda c: (l, 0, c)),
            pl.BlockSpec((None, D, CB), lambda c: (l, 0, nc + c)),
            pl.BlockSpec((None, D, CB), lambda c: (l, 0, 2 * nc + c)),
            pl.BlockSpec((None, D, CB), lambda c: (l, 0, 3 * nc + c)),
            pl.BlockSpec((None, CONV_K, CB), lambda c: (l, 0, c)),
            pl.BlockSpec((None, CONV_K - 1, DEC_BATCH, CB), lambda c: (l, 0, 0, c)),
            pl.BlockSpec((None, POOL_STATE, DEC_BATCH, CB), lambda c: (l, 0, 0, c)),
            pl.BlockSpec((None, None, POOL_GROUP, POOL_GROUP), lambda c: (l, c, 0, 0)),
            pl.BlockSpec((None, 1, CB), lambda c: (l, 0, c)),
        ],
        out_specs=(pl.BlockSpec((NS, CB), lambda c: (0, c)),
                   pl.BlockSpec((NS, CB), lambda c: (0, c)),
                   pl.BlockSpec((CONV_K - 1, DEC_BATCH, CB), lambda c: (0, 0, c)),
                   pl.BlockSpec((POOL_STATE, DEC_BATCH, CB), lambda c: (0, 0, c))),
        compiler_params=_params(("arbitrary",)),
        name="mix_sample",
    )(h, w_in, w_in, w_in, w_in, conv_w, conv_t, pool_t, pool_w, pool_scale)


def _gmlp_s_kernel(h_ref, wu_ref, wv_ref, gv_ref, wvec_ref, bvec_ref, yc_ref, vn_ref):
    h = h_ref[...]
    nb = DEC_BATCH
    v = _dot(h, wv_ref[...].astype(BF16))
    vn = (v * _rms(v)) * gv_ref[...]
    vn_ref[...] = vn
    u = _dot(h, wu_ref[...].astype(BF16))
    for t in range(DEC_SEQ):
        s = bvec_ref[t:t + 1, :]
        for sp in range(t + 1):
            k = t * DEC_SEQ + sp
            s = s + wvec_ref[k:k + 1, :] * vn[sp * nb:(sp + 1) * nb]
        yc_ref[t * nb:(t + 1) * nb, :] = (u[t * nb:(t + 1) * nb] * s).astype(BF16)


def _gmlp_s(h, w_in, g_v, wvec, bvec, l):
    srow = NP // NS
    return pl.pallas_call(
        _gmlp_s_kernel,
        out_shape=(jax.ShapeDtypeStruct((NS, W), BF16),
                   jax.ShapeDtypeStruct((NS, W), F32)),
        grid=(1,),
        in_specs=[
            pl.BlockSpec((NS, D), lambda i: (srow, 0)),
            pl.BlockSpec((None, D, W), lambda i: (l, 0, 4)),
            pl.BlockSpec((None, D, W), lambda i: (l, 0, 5)),
            pl.BlockSpec((None, 1, W), lambda i: (l, 0, 0)),
            pl.BlockSpec((None, DEC_SEQ * DEC_SEQ, W), lambda i: (l, 0, 0)),
            pl.BlockSpec((None, DEC_SEQ, W), lambda i: (l, 0, 0)),
        ],
        out_specs=(pl.BlockSpec((NS, W), lambda i: (0, 0)),
                   pl.BlockSpec((NS, W), lambda i: (0, 0))),
        compiler_params=_params(("arbitrary",)),
        name="gmlp_sample",
    )(h, w_in, w_in, g_v, wvec, bvec)


def _merge_kernel(h_ref, yap, ybp, ycp, yas, ybs, ycs, wg0, wg1, wg2, wbr_ref, o_ref):
    is_sample = pl.program_id(0) >= NP // TM
    h = h_ref[...]
    acc = None
    for n, (yp, ys, wg) in enumerate(((yap, yas, wg0), (ybp, ybs, wg1), (ycp, ycs, wg2))):
        y = jnp.where(is_sample, ys[...], yp[...])
        term = _sigmoid(_dot(h, wg[...])) * _dot(y, wbr_ref[n])
        acc = term if acc is None else acc + term
    o_ref[...] = acc.astype(BF16)


def _merge(h, y_prompt, y_sample, w_gates, w_br):
    db = 512
    g0 = 0
    gstep = D // db
    return pl.pallas_call(
        _merge_kernel,
        out_shape=jax.ShapeDtypeStruct((NT, D), BF16),
        grid=(N_TILES, D // db),
        in_specs=[
            pl.BlockSpec((TM, D), lambda i, d: (i, 0)),
            pl.BlockSpec((TM, W), lambda i, d: (jnp.minimum(i, NP // TM - 1), 0)),
            pl.BlockSpec((TM, W), lambda i, d: (jnp.minimum(i, NP // TM - 1), 0)),
            pl.BlockSpec((TM, W), lambda i, d: (jnp.minimum(i, NP // TM - 1), 0)),
            pl.BlockSpec((NS, W), lambda i, d: (0, 0)),
            pl.BlockSpec((NS, W), lambda i, d: (0, 0)),
            pl.BlockSpec((NS, W), lambda i, d: (0, 0)),
            pl.BlockSpec((D, db), lambda i, d: (0, g0 + d)),
            pl.BlockSpec((D, db), lambda i, d: (0, g0 + gstep + d)),
            pl.BlockSpec((D, db), lambda i, d: (0, g0 + 2 * gstep + d)),
            pl.BlockSpec((3, W, db), lambda i, d: (0, 0, d)),
        ],
        out_specs=pl.BlockSpec((TM, db), lambda i, d: (i, d)),
        compiler_params=_params(("arbitrary", "arbitrary")),
        name="merge",
    )(h, *y_prompt, *y_sample, w_gates, w_gates, w_gates, w_br)


def _route(logits):
    lane = lax.broadcasted_iota(jnp.int32, logits.shape, 1).astype(F32)
    neg = -jnp.inf
    big = float(LANES)
    is_grp = lane < N_GROUPS_MOE
    gl = jnp.where(is_grp, logits, neg)
    gmax = jnp.max(gl, axis=-1, keepdims=True)
    gsel = jnp.min(jnp.where(gl == gmax, lane, big), axis=-1, keepdims=True)
    gp = 1.0 / jnp.sum(jnp.where(is_grp, jnp.exp(logits - gmax), 0.0), axis=-1, keepdims=True)
    lo = N_GROUPS_MOE + gsel * EXP_PER_GROUP
    in_grp = (lane >= lo) & (lane < lo + EXP_PER_GROUP)
    el = jnp.where(in_grp, logits, neg)
    m1 = jnp.max(el, axis=-1, keepdims=True)
    i1 = jnp.min(jnp.where(el == m1, lane, big), axis=-1, keepdims=True)
    el2 = jnp.where(lane == i1, neg, el)
    m2 = jnp.max(el2, axis=-1, keepdims=True)
    i2 = jnp.min(jnp.where(el2 == m2, lane, big), axis=-1, keepdims=True)
    e = jnp.exp(m2 - m1)
    w1 = gp / (1.0 + e)
    w2 = gp * (e / (1.0 + e))
    rw = jnp.where(lane == 0.0, w1, jnp.where(lane == 1.0, w2, 0.0))
    return rw, lane, i1, i2


def _wo_kernel(m_ref, wo_ref, xp_ref, xs_ref, g_ref, gtp, gts, scp, scs, shp, shs, wrc_ref, wrh_ref, br_ref,
               x1_ref, h2_ref, rw_ref, re_ref, cnt_ref, carry):
    i = pl.program_id(0)
    x = jnp.where(i >= NP // TW, xs_ref[...], xp_ref[...])
    gt = _mod_rows(i, TW, gtp, gts)
    x1 = x + gt * _dot(m_ref[...], wo_ref[...])
    x1_ref[...] = x1
    sc = _mod_rows(i, TW, scp, scs)
    sh = _mod_rows(i, TW, shp, shs)
    h2 = ((x1 * _rms(x1)) * g_ref[...]) * (1.0 + sc) + sh
    h2_ref[...] = h2.astype(BF16).reshape(TW, D // LANES, LANES)
    hi = h2.astype(BF16)
    lo = (h2 - hi.astype(F32)).astype(BF16)
    both = _dot(hi, wrc_ref[...])
    logits = (both[:, :LANES] + (both[:, LANES:] + _dot(lo, wrh_ref[...]))) + br_ref[...]
    rw, lane, i1, i2 = _route(logits)
    rw_ref[...] = rw

    @pl.when(i == 0)
    def _():
        carry[...] = jnp.zeros_like(carry)

    o1 = (lane == i1).astype(F32)
    o2 = (lane == i2).astype(F32)
    both_hot = o1 + o2
    r = lax.broadcasted_iota(jnp.int32, (TW, TW), 0)
    c = lax.broadcasted_iota(jnp.int32, (TW, TW), 1)
    before = (r > c).astype(BF16)
    seen = _dot(before, both_hot.astype(BF16)) + carry[...]
    rank1 = jnp.sum(seen * o1, axis=-1, keepdims=True)
    rank2 = jnp.sum(seen * o2, axis=-1, keepdims=True)
    total = carry[...] + jnp.sum(both_hot, axis=0, keepdims=True)
    carry[...] = total
    cnt_ref[...] = jnp.broadcast_to(total, cnt_ref.shape).astype(jnp.int32)
    re = jnp.where(lane == 0.0, i1 - N_GROUPS_MOE,
                   jnp.where(lane == 1.0, i2 - N_GROUPS_MOE,
                             jnp.where(lane == 2.0, rank1, jnp.where(lane == 3.0, rank2, 0.0))))
    re_ref[...] = re.astype(jnp.int32)


def _wo(merged, w_o, x_prompt_rows, x_sample_rows, sample_block0, g2, mod, wr_cat, wr_hi, b_r, l):
    row = lambda i: (i, 0)
    n_ptiles = NP // TW
    return pl.pallas_call(
        _wo_kernel,
        out_shape=(jax.ShapeDtypeStruct((NT, D), F32),
                   jax.ShapeDtypeStruct((NT, D // LANES, LANES), BF16),
                   jax.ShapeDtypeStruct((NT, LANES), F32),
                   jax.ShapeDtypeStruct((NT, LANES), jnp.int32),
                   jax.ShapeDtypeStruct((8, LANES), jnp.int32)),
        grid=(NT // TW,),
        in_specs=[
            pl.BlockSpec((TW, D), row),
            pl.BlockSpec((D, D), lambda i: (0, 0)),
            pl.BlockSpec((TW, D), lambda i: (jnp.minimum(i, n_ptiles - 1), 0)),
            pl.BlockSpec((TW, D), lambda i: (sample_block0 + jnp.maximum(i - n_ptiles, 0), 0)),
            pl.BlockSpec((None, 1, D), lambda i: (l, 0, 0)),
        ] + _mod_specs(l, 2, TW) + _mod_specs(l, 4, TW) + _mod_specs(l, 3, TW) + [
            pl.BlockSpec((None, D, 2 * LANES), lambda i: (l, 0, 0)),
            pl.BlockSpec((None, D, LANES), lambda i: (l, 0, 0)),
            pl.BlockSpec((None, 1, LANES), lambda i: (l, 0, 0)),
        ],
        out_specs=(pl.BlockSpec((TW, D), row), pl.BlockSpec((TW, D // LANES, LANES), lambda i: (i, 0, 0)),
                   pl.BlockSpec((TW, LANES), row), pl.BlockSpec((TW, LANES), row),
                   pl.BlockSpec((8, LANES), lambda i: (0, 0))),
        scratch_shapes=[pltpu.VMEM((1, LANES), F32)],
        compiler_params=_params(("arbitrary",)),
        name="wo_router",
    )(merged, w_o, x_prompt_rows, x_sample_rows, g2, mod, mod, mod, mod, mod, mod, wr_cat, wr_hi, b_r)


def _row_gather_copy(src_hbm, idx, dst, sem):
    return pltpu.make_async_copy(src_hbm.at[pl.ds(idx, 1)], dst, sem)


def _moe_kernel(te_ref, tf_ref, nu_ref, pad_ref, dest_ref, h2_hbm, wg_ref, wu_ref, wd_ref, o_ref,
                src_ref, xbuf, sem, wg_s, wu_s, wd_s):
    del te_ref
    t = pl.program_id(0)
    n_used = nu_ref[0]

    def issue(tile, slot):
        base = tile * TE

        def body(r, carry):
            _row_gather_copy(h2_hbm, src_ref[base + r], xbuf.at[slot, pl.ds(r, 1)], sem.at[slot]).start()
            return carry

        lax.fori_loop(0, TE, body, 0, unroll=8)

    @pl.when(t == 0)
    def _():
        def clear(p, carry):
            src_ref[p] = 0
            return carry

        for e in range(N_EXPERTS):
            lax.fori_loop(pad_ref[e], pad_ref[N_EXPERTS + e], clear, 0)

        def invert(n, carry):
            for k in range(TOP_K):
                src_ref[dest_ref[TOP_K * n + k]] = n
            return carry

        lax.fori_loop(0, NT, invert, 0, unroll=8)
        for k in range(MOE_SLOTS - 1):
            issue(jnp.minimum(k, n_used - 1), k)

    @pl.when((t < n_used) & (tf_ref[t] == 1))
    def _():
        wg_s[...] = wg_ref[...].astype(BF16)
        wu_s[...] = wu_ref[...].astype(BF16)
        wd_s[...] = wd_ref[...].astype(BF16)

    def wait_tile(slot):
        pltpu.make_async_copy(h2_hbm.at[pl.ds(0, TE)], xbuf.at[slot], sem.at[slot]).wait()

    @pl.when(t < n_used)
    def _():
        slot = t % MOE_SLOTS
        wait_tile(slot)
        x = xbuf[slot].reshape(TE, D)
        ahead = (t + MOE_SLOTS - 1) % MOE_SLOTS
        base = jnp.minimum(t + MOE_SLOTS - 1, n_used - 1) * TE
        for r in range(TE):
            _row_gather_copy(h2_hbm, src_ref[base + r], xbuf.at[ahead, pl.ds(r, 1)], sem.at[ahead]).start()
        a = _dot(x, wg_s[...])
        b = _dot(x, wu_s[...])
        hid = (a * _sigmoid(a)) * b
        o_ref[...] = _dot(hid.astype(BF16), wd_s[...])

    @pl.when(t == n_used - 1)
    def _():
        for k in range(1, MOE_SLOTS):
            wait_tile((t + k) % MOE_SLOTS)

    @pl.when(t >= n_used)
    def _():
        o_ref[...] = jnp.zeros_like(o_ref)


def _moe(h2, tile_expert, tile_first, n_used, pads, dest, w_gate, w_up, w_down, l):
    grid_spec = pltpu.PrefetchScalarGridSpec(
        num_scalar_prefetch=5,
        grid=(MAX_TILES,),
        in_specs=[
            pl.BlockSpec(memory_space=pl.ANY),
            pl.BlockSpec((None, None, D, D_FF), lambda t, te, *_: (l, te[t], 0, 0)),
            pl.BlockSpec((None, None, D, D_FF), lambda t, te, *_: (l, te[t], 0, 0)),
            pl.BlockSpec((None, None, D_FF, D), lambda t, te, *_: (l, te[t], 0, 0)),
        ],
        out_specs=pl.BlockSpec((TE, D), lambda t, *_: (t, 0)),
        scratch_shapes=[
            pltpu.SMEM((P_ROWS,), jnp.int32),
            pltpu.VMEM((MOE_SLOTS, TE, D // LANES, LANES), BF16),
            pltpu.SemaphoreType.DMA((MOE_SLOTS,)),
            pltpu.VMEM((D, D_FF), BF16),
            pltpu.VMEM((D, D_FF), BF16),
            pltpu.VMEM((D_FF, D), BF16),
        ],
    )
    return pl.pallas_call(
        _moe_kernel,
        out_shape=jax.ShapeDtypeStruct((P_ROWS, D), F32),
        grid_spec=grid_spec,
        compiler_params=_params(("arbitrary",)),
        name="moe_experts",
    )(tile_expert, tile_first, n_used, pads, dest, h2, w_gate, w_up, w_down)


def _dispatch(route_e, counts_lanes):
    counts = counts_lanes[0, N_GROUPS_MOE:N_GROUPS_MOE + N_EXPERTS]
    tiles = (counts + TE - 1) // TE
    eids = jnp.arange(N_EXPERTS, dtype=jnp.int32)
    tile_end = jnp.sum(jnp.where(eids[None, :] <= eids[:, None], tiles[None, :], 0), axis=1)
    tile_start = tile_end - tiles
    hot = route_e[:, :TOP_K, None] == eids[None, None, :]
    row0 = jnp.sum(jnp.where(hot, tile_start[None, None, :] * TE, 0), axis=-1)
    dest = (row0 + route_e[:, TOP_K:2 * TOP_K]).reshape(-1).astype(jnp.int32)
    tids = jnp.arange(MAX_TILES, dtype=jnp.int32)
    tile_expert = jnp.minimum(jnp.sum((tile_end[None, :] <= tids[:, None]).astype(jnp.int32), axis=1),
                              N_EXPERTS - 1)
    first_tile = jnp.sum(jnp.where(tile_expert[:, None] == eids[None, :], tile_start[None, :], 0), axis=1)
    tile_first = (tids == first_tile).astype(jnp.int32)
    n_used = tile_end[-1:].astype(jnp.int32)
    pads = jnp.concatenate([tile_start * TE + counts, tile_end * TE]).astype(jnp.int32)
    return dest, tile_expert, tile_first, n_used, pads


def _combine_kernel(final, tile0, dest_ref, ys_hbm, x_ref, rw_ref, g_ref, gtp, gts, *rest):
    if final:
        y_ref, ybuf, sem = rest
    else:
        scp, scs, shp, shs, x2_ref, h_ref, ybuf, sem = rest
    step = pl.program_id(0)
    nsteps = pl.num_programs(0)
    i = step + tile0

    def issue(tile, slot, r):
        for k in range(TOP_K):
            _row_gather_copy(ys_hbm, dest_ref[tile * (TK * TOP_K) + TOP_K * r + k],
                             ybuf.at[slot, k, pl.ds(r, 1)], sem.at[slot]).start()

    def wait_tile(slot):
        for k in range(TOP_K):
            pltpu.make_async_copy(ys_hbm.at[pl.ds(0, TK)], ybuf.at[slot, k], sem.at[slot]).wait()

    last = tile0 + nsteps - 1

    @pl.when(step == 0)
    def _():
        for k in range(GATHER_SLOTS - 1):
            def body(r, carry, k=k):
                issue(jnp.minimum(i + k, last), k, r)
                return carry

            lax.fori_loop(0, TK, body, 0, unroll=8)

    slot = step % GATHER_SLOTS
    wait_tile(slot)
    ahead = (step + GATHER_SLOTS - 1) % GATHER_SLOTS
    nxt = jnp.minimum(i + GATHER_SLOTS - 1, last)
    for r in range(TK):
        issue(nxt, ahead, r)
    gt = _mod_rows(i, TK, gtp, gts)
    rw = rw_ref[...]
    moe = rw[:, 0:1] * ybuf[slot, 0] + rw[:, 1:2] * ybuf[slot, 1]
    x2 = x_ref[...] + gt * moe
    xn = (x2 * _rms(x2)) * g_ref[...]
    if final:
        y_ref[...] = xn
    else:
        x2_ref[...] = x2
        sc = _mod_rows(i, TK, scp, scs)
        sh = _mod_rows(i, TK, shp, shs)
        h_ref[...] = (xn * (1.0 + sc) + sh).astype(BF16)

    @pl.when(step == nsteps - 1)
    def _():
        for k in range(1, GATHER_SLOTS):
            wait_tile((step + k) % GATHER_SLOTS)


def _combine(dest, ys, x1, route_w, g, mod, l, final, tile0=0, ntiles=NT // TK):
    row_in = lambda s, d: (s + tile0, 0)
    row_out = lambda s, d: (s, 0)
    nmod = lambda q, ll: [
        pl.BlockSpec((None, 8, D), lambda s, d: (ll, MOD_PROMPT_BLOCK, q)),
        pl.BlockSpec((None, DEC_BATCH, D), lambda s, d: (ll, 0, q)),
    ]
    in_specs = [pl.BlockSpec(memory_space=pl.ANY), pl.BlockSpec((TK, D), row_in),
                pl.BlockSpec((TK, LANES), row_in),
                pl.BlockSpec((1, D), lambda s, d: (0, 0))] + nmod(5, l)
    args = [ys, x1, route_w, g, mod, mod]
    rows = ntiles * TK
    if final:
        out_shape = jax.ShapeDtypeStruct((rows, D), F32)
        out_specs = pl.BlockSpec((TK, D), row_out)
    else:
        in_specs += nmod(1, l + 1) + nmod(0, l + 1)
        args += [mod, mod, mod, mod]
        out_shape = (jax.ShapeDtypeStruct((rows, D), F32), jax.ShapeDtypeStruct((rows, D), BF16))
        out_specs = (pl.BlockSpec((TK, D), row_out), pl.BlockSpec((TK, D), row_out))
    grid_spec = pltpu.PrefetchScalarGridSpec(
        num_scalar_prefetch=1,
        grid=(ntiles,),
        in_specs=in_specs,
        out_specs=out_specs,
        scratch_shapes=[pltpu.VMEM((GATHER_SLOTS, TOP_K, TK, D), F32),
                        pltpu.SemaphoreType.DMA((GATHER_SLOTS,))],
    )
    return pl.pallas_call(
        functools.partial(_combine_kernel, final, tile0),
        out_shape=out_shape,
        grid_spec=grid_spec,
        compiler_params=_params(("arbitrary",)),
        name=("combine_final_%d" % tile0) if final else "combine",
    )(dest, *args)


def kernel(x_prompt, x_sample, c_prompt, c_sample, state_conv, state_pool, w_ada, b_ada, g_norm1, w_in,
           conv_w, pool_w, pool_scale, g_v, w_s, b_s, w_br, w_o, g_norm2, w_rg, b_rg, w_re, b_re,
           w_gate, w_up, w_down, g_final):
    xp_rows = x_prompt.reshape(NP, D)
    xs_rows = x_sample.transpose(1, 0, 2).reshape(NS, D)
    sample_block0 = 0
    c_all = jnp.concatenate([c_sample, c_prompt,
                             jnp.zeros((MOD_ROWS - DEC_BATCH - BATCH, D), F32)], axis=0).astype(BF16)
    mod = _ada(c_all, w_ada, b_ada)

    pool_w_b = pool_w.astype(BF16)
    tril = jnp.tril(jnp.ones((CHUNK, CHUNK), dtype=bool))
    ws_tril = jnp.where(tril[None, None], w_s, 0.0).astype(BF16)
    bs_tile = jnp.broadcast_to(b_s[:, :, :, None], (DEPTH, HEADS, CHUNK, CHUNK))
    small = jnp.tril(jnp.ones((DEC_SEQ, DEC_SEQ), dtype=bool))
    ws_small = jnp.where(small[None, None], w_s[:, :, :DEC_SEQ, :DEC_SEQ], 0.0)
    wvec = jnp.repeat(ws_small.transpose(0, 2, 3, 1).reshape(DEPTH, DEC_SEQ * DEC_SEQ, HEADS), CHUNK, axis=-1)
    bvec = jnp.repeat(b_s[:, :, :DEC_SEQ].transpose(0, 2, 1), CHUNK, axis=-1)
    w_r = jnp.concatenate([w_rg, w_re, jnp.zeros((DEPTH, D, LANES - N_GROUPS_MOE - N_EXPERTS), F32)], axis=-1)
    wr_hi = w_r.astype(BF16)
    wr_lo = (w_r - wr_hi.astype(F32)).astype(BF16)
    wr_cat = jnp.concatenate([wr_hi, wr_lo], axis=-1)
    b_r = jnp.concatenate([b_rg, b_re, jnp.zeros((DEPTH, LANES - N_GROUPS_MOE - N_EXPERTS), F32)],
                          axis=-1).reshape(DEPTH, 1, LANES)
    conv_t = state_conv.transpose(0, 2, 1, 3)
    pool_t = state_pool.transpose(0, 2, 1, 3)
    g1 = g_norm1.reshape(DEPTH, 1, D)
    g2 = g_norm2.reshape(DEPTH, 1, D)
    gv = g_v.reshape(DEPTH, 1, W)
    pscale = pool_scale.reshape(DEPTH, 1, W)

    h = _norm(xp_rows, xs_rows, g1, mod, 0)
    conv_p, pool_p, conv_s, pool_s, v_s = [], [], [], [], []
    y_prompt = y_sample = None
    for l in range(DEPTH):
        ya, cst_p, w_gates_b = _conv_p(h, w_in, conv_w, l)
        yb, pst_p, w_br_b = _pool_p(h, w_in, pool_w_b, pscale, w_br, l)
        yc, w_o_b = _gmlp_p(h, w_in, gv, ws_tril, bs_tile, w_o, l)
        ya_s, yb_s, cst_s, pst_s = _mix_s(h, w_in, conv_w, pool_w_b, pscale, conv_t, pool_t, l)
        yc_s, vn_s = _gmlp_s(h, w_in, gv, wvec, bvec, l)
        merged = _merge(h, (ya, yb, yc), (ya_s, yb_s, yc_s), w_gates_b, w_br_b)
        x1, h2, route_w, route_e, counts = _wo(merged, w_o_b, xp_rows, xs_rows, sample_block0, g2, mod,
                                               wr_cat, wr_hi, b_r, l)
        dest, tile_expert, tile_first, n_used, pads = _dispatch(route_e, counts)
        ys = _moe(h2, tile_expert, tile_first, n_used, pads, dest, w_gate, w_up, w_down, l)
        if l + 1 < DEPTH:
            x, h = _combine(dest, ys, x1, route_w, g1[l + 1], mod, l, final=False)
            xp_rows = xs_rows = x
            sample_block0 = NP // TW
        else:
            gf = g_final.reshape(1, D)
            y_prompt = _combine(dest, ys, x1, route_w, gf, mod, l, final=True, tile0=0, ntiles=NP // TK)
            y_sample = _combine(dest, ys, x1, route_w, gf, mod, l, final=True, tile0=NP // TK,
                                ntiles=NS // TK)
        conv_p.append(cst_p[:, 8 - (CONV_K - 1):, :])
        pool_p.append(pst_p[:, 16 - POOL_STATE:, :])
        conv_s.append(cst_s.transpose(1, 0, 2))
        pool_s.append(pst_s.transpose(1, 0, 2))
        v_s.append(vn_s.reshape(DEC_SEQ, DEC_BATCH, W).transpose(1, 0, 2))

    y_prompt = y_prompt.reshape(BATCH, SEQ, D)
    y_sample = y_sample.reshape(DEC_SEQ, DEC_BATCH, D).transpose(1, 0, 2)
    return (y_prompt, y_sample, jnp.stack(conv_p), jnp.stack(pool_p), jnp.stack(conv_s),
            jnp.stack(pool_s), jnp.stack(v_s))
```

```python
import functools

import jax
import jax.numpy as jnp
from jax import lax
from jax.experimental import pallas as pl
from jax.experimental.pallas import tpu as pltpu

F32 = jnp.float32
BF16 = jnp.bfloat16

D = 2048
BATCH = 4
SEQ = 2048
DEPTH = 2
DEC_BATCH = 128
DEC_SEQ = 4
W = D // 2
CONV_K = 3
POOL_STATE = 15
N_POOL_GROUPS = 4
POOL_GROUP = W // N_POOL_GROUPS
CHUNK = 128
HEADS = W // CHUNK
N_GROUPS_MOE = 4
EXP_PER_GROUP = 4
N_EXPERTS = 16
TOP_K = 2
D_FF = D // 4
EPS = 1e-6

NP = BATCH * SEQ
NS = DEC_BATCH * DEC_SEQ
NT = NP + NS
TM = 512
N_TILES = NT // TM
CB = 256
TE = 256
MAX_TILES = (NT * TOP_K) // TE + N_EXPERTS
P_ROWS = MAX_TILES * TE
TK = 256
TW = 256
GATHER_SLOTS = 3
MOE_SLOTS = 4
LANES = 128
SUBLANES = 8
CONV_TAIL = SUBLANES
POOL_TAIL = 2 * SUBLANES
MOD_ROWS = 144
MOD_PROMPT_BLOCK = DEC_BATCH // SUBLANES

VMEM_LIMIT_V7X = 56 * 1024 * 1024


def _dot(a, b):
    return jnp.dot(a, b, preferred_element_type=F32)


def _params(sem, vmem=VMEM_LIMIT_V7X):
    return pltpu.CompilerParams(dimension_semantics=sem, vmem_limit_bytes=vmem)


def _sigmoid(x):
    return 1.0 / (1.0 + jnp.exp(-x))


def _rms(x):
    return lax.rsqrt(jnp.mean(x * x, axis=-1, keepdims=True) + EPS)


def _ada_kernel(c_ref, w_ref, b_ref, o_ref):
    o_ref[...] = _dot(c_ref[...], w_ref[...].astype(BF16)) + b_ref[...]


def _ada(c_all, w_ada, b_ada):
    nb = 1024
    return pl.pallas_call(
        _ada_kernel,
        out_shape=jax.ShapeDtypeStruct((DEPTH, MOD_ROWS, 6 * D), F32),
        grid=(DEPTH, 6 * D // nb),
        in_specs=[
            pl.BlockSpec((MOD_ROWS, D), lambda l, j: (0, 0)),
            pl.BlockSpec((None, D, nb), lambda l, j: (l, 0, j)),
            pl.BlockSpec((None, 1, nb), lambda l, j: (l, 0, j)),
        ],
        out_specs=pl.BlockSpec((None, MOD_ROWS, nb), lambda l, j: (l, 0, j)),
        compiler_params=_params(("arbitrary", "arbitrary")),
        name="ada",
    )(c_all, w_ada, b_ada.reshape(DEPTH, 1, 6 * D))


def _mod_specs(l, q):
    return [
        pl.BlockSpec((None, SUBLANES, D), lambda i, *_: (l, MOD_PROMPT_BLOCK, q)),
        pl.BlockSpec((None, DEC_BATCH, D), lambda i, *_: (l, 0, q)),
    ]


def _mod_rows(i, tile_rows, mp_ref, ms_ref):
    tiles_per_seq = SEQ // tile_rows
    is_sample = i >= NP // tile_rows
    b = jnp.minimum(i // tiles_per_seq, BATCH - 1)
    mp = mp_ref[pl.ds(b, 1), :]
    ms = jnp.concatenate([ms_ref[...]] * (tile_rows // DEC_BATCH), axis=0)
    return jnp.where(is_sample, ms, mp)


def _norm_kernel(xp_ref, xs_ref, g_ref, scp, scs, shp, shs, h_ref):
    i = pl.program_id(0)
    x = jnp.where(i >= NP // TM, xs_ref[...], xp_ref[...])
    sc = _mod_rows(i, TM, scp, scs)
    sh = _mod_rows(i, TM, shp, shs)
    h_ref[...] = (((x * _rms(x)) * g_ref[...]) * (1.0 + sc) + sh).astype(BF16)


def _norm(x_prompt_rows, x_sample_rows, g, mod, l):
    n_ptiles = NP // TM
    return pl.pallas_call(
        _norm_kernel,
        out_shape=jax.ShapeDtypeStruct((NT, D), BF16),
        grid=(N_TILES,),
        in_specs=[pl.BlockSpec((TM, D), lambda i: (jnp.minimum(i, n_ptiles - 1), 0)),
                  pl.BlockSpec((TM, D), lambda i: (jnp.maximum(i - n_ptiles, 0), 0)),
                  pl.BlockSpec((None, 1, D), lambda i: (l, 0, 0))]
                 + _mod_specs(l, 1) + _mod_specs(l, 0),
        out_specs=pl.BlockSpec((TM, D), lambda i: (i, 0)),
        compiler_params=_params(("arbitrary",)),
        name="norm1",
    )(x_prompt_rows, x_sample_rows, g, mod, mod, mod, mod)


def _shift_rows(a, s, row):
    return jnp.where(row >= s, pltpu.roll(a, s, 0), 0.0)


def _conv_p_kernel(h_ref, wb_ref, wc_ref, wh_ref, cw_ref, wcast_ref, y_ref, st_ref, wcast_out):
    wcast_out[...] = wcast_ref[...].astype(BF16)
    h = h_ref[...]
    z = _dot(h, wc_ref[...].astype(BF16)) * _dot(h, wh_ref[...].astype(BF16))
    row = lax.broadcasted_iota(jnp.int32, (SEQ, 1), 0)
    cw = cw_ref[...]
    y = cw[0:1] * _shift_rows(z, 2, row) + cw[1:2] * _shift_rows(z, 1, row) + cw[2:3] * z
    y_ref[...] = (_dot(h, wb_ref[...].astype(BF16)) * y).astype(BF16)
    st_ref[...] = z[SEQ - CONV_TAIL:, :]


def _conv_p(h, w_in, conv_w, l):
    nc = W // CB
    steps = BATCH * nc
    gcols = 3 * D // steps
    g0 = 6 * W // gcols
    return pl.pallas_call(
        _conv_p_kernel,
        out_shape=(jax.ShapeDtypeStruct((NP, W), BF16),
                   jax.ShapeDtypeStruct((BATCH, CONV_TAIL, W), F32),
                   jax.ShapeDtypeStruct((D, 3 * D), BF16)),
        grid=(BATCH, nc),
        in_specs=[
            pl.BlockSpec((SEQ, D), lambda b, c: (b, 0)),
            pl.BlockSpec((None, D, CB), lambda b, c: (l, 0, c)),
            pl.BlockSpec((None, D, CB), lambda b, c: (l, 0, nc + c)),
            pl.BlockSpec((None, D, CB), lambda b, c: (l, 0, 2 * nc + c)),
            pl.BlockSpec((None, CONV_K, CB), lambda b, c: (l, 0, c)),
            pl.BlockSpec((None, D, gcols), lambda b, c: (l, 0, g0 + b * nc + c)),
        ],
        out_specs=(pl.BlockSpec((SEQ, CB), lambda b, c: (b, c)),
                   pl.BlockSpec((None, CONV_TAIL, CB), lambda b, c: (b, 0, c)),
                   pl.BlockSpec((D, gcols), lambda b, c: (0, b * nc + c))),
        compiler_params=_params(("arbitrary", "arbitrary")),
        name="conv_prompt",
    )(h, w_in, w_in, w_in, conv_w, w_in)


def _pool_p_kernel(h_ref, wp_ref, pw_ref, ps_ref, wcast_ref, y_ref, st_ref, wcast_out):
    wcast_out[...] = wcast_ref[...].astype(BF16)
    g = pl.program_id(1)
    p = _dot(h_ref[...], wp_ref[...].astype(BF16))
    row = lax.broadcasted_iota(jnp.int32, (SEQ, 1), 0)
    s = p + _shift_rows(p, 1, row)
    s = jnp.where(g >= 1, s + _shift_rows(s, 2, row), s)
    s = jnp.where(g >= 2, s + _shift_rows(s, 4, row), s)
    s = jnp.where(g >= 3, s + _shift_rows(s, 8, row), s)
    window = jnp.left_shift(2, g)
    cnt = jnp.minimum(row + 1, window).astype(F32)
    d = s / cnt - p
    y_ref[...] = (_dot(d.astype(BF16), pw_ref[...]) * ps_ref[...]).astype(BF16)
    st_ref[...] = p[SEQ - POOL_TAIL:, :]


def _pool_p(h, w_in, pool_w, pool_scale, w_br, l):
    col0 = 3 * W // CB
    bcols = D // (BATCH * N_POOL_GROUPS)
    return pl.pallas_call(
        _pool_p_kernel,
        out_shape=(jax.ShapeDtypeStruct((NP, W), BF16),
                   jax.ShapeDtypeStruct((BATCH, POOL_TAIL, W), F32),
                   jax.ShapeDtypeStruct((3, W, D), BF16)),
        grid=(BATCH, N_POOL_GROUPS),
        in_specs=[
            pl.BlockSpec((SEQ, D), lambda b, g: (b, 0)),
            pl.BlockSpec((None, D, CB), lambda b, g: (l, 0, col0 + g)),
            pl.BlockSpec((None, None, POOL_GROUP, POOL_GROUP), lambda b, g: (l, g, 0, 0)),
            pl.BlockSpec((None, 1, CB), lambda b, g: (l, 0, g)),
            pl.BlockSpec((None, 3, W, bcols), lambda b, g: (l, 0, 0, b * N_POOL_GROUPS + g)),
        ],
        out_specs=(pl.BlockSpec((SEQ, CB), lambda b, g: (b, g)),
                   pl.BlockSpec((None, POOL_TAIL, CB), lambda b, g: (b, 0, g)),
                   pl.BlockSpec((3, W, bcols), lambda b, g: (0, 0, b * N_POOL_GROUPS + g))),
        compiler_params=_params(("arbitrary", "arbitrary")),
        name="pool_prompt",
    )(h, w_in, pool_w, pool_scale, w_br)


def _gmlp_p_kernel(h_ref, wv_ref, wu_ref, gv_ref, ws_ref, bs_ref, wcast_ref, y_ref, wcast_out, vn_s):
    wcast_out[...] = wcast_ref[...].astype(BF16)
    j = pl.program_id(1)

    @pl.when(j == 0)
    def _():
        wv = wv_ref[...].astype(BF16)
        for r0 in range(0, SEQ, TM):
            v = _dot(h_ref[r0:r0 + TM, :], wv)
            vn = (v * _rms(v)) * gv_ref[...]
            for hd in range(HEADS):
                vn_s[hd, r0:r0 + TM, :] = vn[:, hd * CHUNK:(hd + 1) * CHUNK].astype(BF16)

    u = _dot(h_ref[...], wu_ref[...].astype(BF16))
    heads_per_block = CB // CHUNK
    for hh in range(heads_per_block):
        head = heads_per_block * j + hh
        wsh = ws_ref[head]
        bsh = bs_ref[head]
        for n in range(SEQ // CHUNK):
            rows = slice(n * CHUNK, (n + 1) * CHUNK)
            cols = slice(hh * CHUNK, (hh + 1) * CHUNK)
            s = _dot(wsh, vn_s[head, pl.ds(n * CHUNK, CHUNK), :]) + bsh
            y_ref[rows, cols] = (u[rows, cols] * s).astype(BF16)


def _gmlp_p(h, w_in, g_v, ws_tril, bs_tile, w_o, l):
    ucol0 = 4 * W // CB
    nj = W // CB
    ocols = D // (BATCH * nj)
    return pl.pallas_call(
        _gmlp_p_kernel,
        out_shape=(jax.ShapeDtypeStruct((NP, W), BF16), jax.ShapeDtypeStruct((D, D), BF16)),
        grid=(BATCH, nj),
        in_specs=[
            pl.BlockSpec((SEQ, D), lambda b, j: (b, 0)),
            pl.BlockSpec((None, D, W), lambda b, j: (l, 0, 5), pipeline_mode=pl.Buffered(1)),
            pl.BlockSpec((None, D, CB), lambda b, j: (l, 0, ucol0 + j)),
            pl.BlockSpec((None, 1, W), lambda b, j: (l, 0, 0)),
            pl.BlockSpec((None, HEADS, CHUNK, CHUNK), lambda b, j: (l, 0, 0, 0)),
            pl.BlockSpec((None, HEADS, CHUNK, CHUNK), lambda b, j: (l, 0, 0, 0)),
            pl.BlockSpec((None, D, ocols), lambda b, j: (l, 0, b * nj + j)),
        ],
        out_specs=(pl.BlockSpec((SEQ, CB), lambda b, j: (b, j)),
                   pl.BlockSpec((D, ocols), lambda b, j: (0, b * nj + j))),
        scratch_shapes=[pltpu.VMEM((HEADS, SEQ, CHUNK), BF16)],
        compiler_params=_params(("arbitrary", "arbitrary")),
        name="gmlp_prompt",
    )(h, w_in, w_in, g_v, ws_tril, bs_tile, w_o)


def _mix_s_kernel(h_ref, wb_ref, wc_ref, wh_ref, wp_ref, cw_ref, pc_ref, pp_ref,
                  pw_ref, ps_ref, ya_ref, yb_ref, cst_ref, pst_ref):
    g = pl.program_id(0)
    h = h_ref[...]
    nb = DEC_BATCH
    z = _dot(h, wc_ref[...].astype(BF16)) * _dot(h, wh_ref[...].astype(BF16))
    bg = _dot(h, wb_ref[...].astype(BF16))
    zs = [pc_ref[0], pc_ref[1]] + [z[t * nb:(t + 1) * nb] for t in range(DEC_SEQ)]
    cw = cw_ref[...]
    for t in range(DEC_SEQ):
        y = cw[0:1] * zs[t] + cw[1:2] * zs[t + 1] + cw[2:3] * zs[t + 2]
        ya_ref[t * nb:(t + 1) * nb, :] = (bg[t * nb:(t + 1) * nb] * y).astype(BF16)
    cst_ref[0] = zs[DEC_SEQ]
    cst_ref[1] = zs[DEC_SEQ + 1]
    p = _dot(h, wp_ref[...].astype(BF16))
    pp = [pp_ref[k] for k in range(POOL_STATE)] + [p[t * nb:(t + 1) * nb] for t in range(DEC_SEQ)]
    window = jnp.left_shift(2, g).astype(F32)
    ds = []
    for t in range(DEC_SEQ):
        e = POOL_STATE + t
        s = pp[e] + pp[e - 1]
        s4 = s + (pp[e - 2] + pp[e - 3])
        s8 = s4 + ((pp[e - 4] + pp[e - 5]) + (pp[e - 6] + pp[e - 7]))
        s16 = s8 + (((pp[e - 8] + pp[e - 9]) + (pp[e - 10] + pp[e - 11]))
                    + ((pp[e - 12] + pp[e - 13]) + (pp[e - 14] + pp[e - 15])))
        s = jnp.where(g >= 1, s4, s)
        s = jnp.where(g >= 2, s8, s)
        s = jnp.where(g >= 3, s16, s)
        ds.append(s / window - pp[e])
    d = jnp.concatenate(ds, axis=0)
    yb_ref[...] = (_dot(d.astype(BF16), pw_ref[...]) * ps_ref[...]).astype(BF16)
    for k in range(POOL_STATE):
        pst_ref[k] = pp[DEC_SEQ + k]


def _mix_s(h, w_in, conv_w, pool_w, pool_scale, conv_t, pool_t, l):
    nc = W // CB
    srow = NP // NS
    return pl.pallas_call(
        _mix_s_kernel,
        out_shape=(jax.ShapeDtypeStruct((NS, W), BF16),
                   jax.ShapeDtypeStruct((NS, W), BF16),
                   jax.ShapeDtypeStruct((CONV_K - 1, DEC_BATCH, W), F32),
                   jax.ShapeDtypeStruct((POOL_STATE, DEC_BATCH, W), F32)),
        grid=(nc,),
        in_specs=[
            pl.BlockSpec((NS, D), lambda c: (srow, 0)),
            pl.BlockSpec((None, D, CB), lambda c: (l, 0, c)),
            pl.BlockSpec((None, D, CB), lambda c: (l, 0, nc + c)),
            pl.BlockSpec((None, D, CB), lambda c: (l, 0, 2 * nc + c)),
            pl.BlockSpec((None, D, CB), lambda c: (l, 0, 3 * nc + c)),
            pl.BlockSpec((None, CONV_K, CB), lambda c: (l, 0, c)),
            pl.BlockSpec((None, CONV_K - 1, DEC_BATCH, CB), lambda c: (l, 0, 0, c)),
            pl.BlockSpec((None, POOL_STATE, DEC_BATCH, CB), lambda c: (l, 0, 0, c)),
            pl.BlockSpec((None, None, POOL_GROUP, POOL_GROUP), lambda c: (l, c, 0, 0)),
            pl.BlockSpec((None, 1, CB), lambda c: (l, 0, c)),
        ],
        out_specs=(pl.BlockSpec((NS, CB), lambda c: (0, c)),
                   pl.BlockSpec((NS, CB), lambda c: (0, c)),
                   pl.BlockSpec((CONV_K - 1, DEC_BATCH, CB), lambda c: (0, 0, c)),
                   pl.BlockSpec((POOL_STATE, DEC_BATCH, CB), lambda c: (0, 0, c))),
        compiler_params=_params(("arbitrary",)),
        name="mix_sample",
    )(h, w_in, w_in, w_in, w_in, conv_w, conv_t, pool_t, pool_w, pool_scale)


def _gmlp_s_kernel(h_ref, wu_ref, wv_ref, gv_ref, wvec_ref, bvec_ref, yc_ref, vn_ref):
    h = h_ref[...]
    nb = DEC_BATCH
    v = _dot(h, wv_ref[...].astype(BF16))
    vn = (v * _rms(v)) * gv_ref[...]
    vn_ref[...] = vn
    u = _dot(h, wu_ref[...].astype(BF16))
    for t in range(DEC_SEQ):
        s = bvec_ref[t:t + 1, :]
        for sp in range(t + 1):
            k = t * DEC_SEQ + sp
            s = s + wvec_ref[k:k + 1, :] * vn[sp * nb:(sp + 1) * nb]
        yc_ref[t * nb:(t + 1) * nb, :] = (u[t * nb:(t + 1) * nb] * s).astype(BF16)


def _gmlp_s(h, w_in, g_v, wvec, bvec, l):
    srow = NP // NS
    return pl.pallas_call(
        _gmlp_s_kernel,
        out_shape=(jax.ShapeDtypeStruct((NS, W), BF16),
                   jax.ShapeDtypeStruct((NS, W), F32)),
        grid=(1,),
        in_specs=[
            pl.BlockSpec((NS, D), lambda i: (srow, 0)),
            pl.BlockSpec((None, D, W), lambda i: (l, 0, 4)),
            pl.BlockSpec((None, D, W), lambda i: (l, 0, 5)),
            pl.BlockSpec((None, 1, W), lambda i: (l, 0, 0)),
            pl.BlockSpec((None, DEC_SEQ * DEC_SEQ, W), lambda i: (l, 0, 0)),
            pl.BlockSpec((None, DEC_SEQ, W), lambda i: (l, 0, 0)),
        ],
        out_specs=(pl.BlockSpec((NS, W), lambda i: (0, 0)),
                   pl.BlockSpec((NS, W), lambda i: (0, 0))),
        compiler_params=_params(("arbitrary",)),
        name="gmlp_sample",
    )(h, w_in, w_in, g_v, wvec, bvec)


def _merge_kernel(h_ref, yap, ybp, ycp, yas, ybs, ycs, wg0, wg1, wg2, wbr_ref, o_ref):
    is_sample = pl.program_id(0) >= NP // TM
    h = h_ref[...]
    acc = None
    for n, (yp, ys, wg) in enumerate(((yap, yas, wg0), (ybp, ybs, wg1), (ycp, ycs, wg2))):
        y = jnp.where(is_sample, ys[...], yp[...])
        term = _sigmoid(_dot(h, wg[...])) * _dot(y, wbr_ref[n])
        acc = term if acc is None else acc + term
    o_ref[...] = acc.astype(BF16)


def _merge(h, y_prompt, y_sample, w_gates, w_br):
    db = 512
    g0 = 0
    gstep = D // db
    return pl.pallas_call(
        _merge_kernel,
        out_shape=jax.ShapeDtypeStruct((NT, D), BF16),
        grid=(N_TILES, D // db),
        in_specs=[
            pl.BlockSpec((TM, D), lambda i, d: (i, 0)),
            pl.BlockSpec((TM, W), lambda i, d: (jnp.minimum(i, NP // TM - 1), 0)),
            pl.BlockSpec((TM, W), lambda i, d: (jnp.minimum(i, NP // TM - 1), 0)),
            pl.BlockSpec((TM, W), lambda i, d: (jnp.minimum(i, NP // TM - 1), 0)),
            pl.BlockSpec((NS, W), lambda i, d: (0, 0)),
            pl.BlockSpec((NS, W), lambda i, d: (0, 0)),
            pl.BlockSpec((NS, W), lambda i, d: (0, 0)),
            pl.BlockSpec((D, db), lambda i, d: (0, g0 + d)),
            pl.BlockSpec((D, db), lambda i, d: (0, g0 + gstep + d)),
            pl.BlockSpec((D, db), lambda i, d: (0, g0 + 2 * gstep + d)),
            pl.BlockSpec((3, W, db), lambda i, d: (0, 0, d)),
        ],
        out_specs=pl.BlockSpec((TM, db), lambda i, d: (i, d)),
        compiler_params=_params(("arbitrary", "arbitrary")),
        name="merge",
    )(h, *y_prompt, *y_sample, w_gates, w_gates, w_gates, w_br)


def _route(logits):
    lane = lax.broadcasted_iota(jnp.int32, logits.shape, 1).astype(F32)
    neg = -jnp.inf
    big = float(LANES)
    is_grp = lane < N_GROUPS_MOE
    gl = jnp.where(is_grp, logits, neg)
    gmax = jnp.max(gl, axis=-1, keepdims=True)
    gsel = jnp.min(jnp.where(gl == gmax, lane, big), axis=-1, keepdims=True)
    gp = 1.0 / jnp.sum(jnp.where(is_grp, jnp.exp(logits - gmax), 0.0), axis=-1, keepdims=True)
    lo = N_GROUPS_MOE + gsel * EXP_PER_GROUP
    in_grp = (lane >= lo) & (lane < lo + EXP_PER_GROUP)
    el = jnp.where(in_grp, logits, neg)
    m1 = jnp.max(el, axis=-1, keepdims=True)
    i1 = jnp.min(jnp.where(el == m1, lane, big), axis=-1, keepdims=True)
    el2 = jnp.where(lane == i1, neg, el)
    m2 = jnp.max(el2, axis=-1, keepdims=True)
    i2 = jnp.min(jnp.where(el2 == m2, lane, big), axis=-1, keepdims=True)
    e = jnp.exp(m2 - m1)
    w1 = gp / (1.0 + e)
    w2 = gp * (e / (1.0 + e))
    rw = jnp.where(lane == 0.0, w1, jnp.where(lane == 1.0, w2, 0.0))
    return rw, lane, i1, i2


def _wo_kernel(m_ref, wo_ref, xp_ref, xs_ref, g_ref, gtp, gts, scp, scs, shp, shs, wrc_ref, wrh_ref, br_ref,
               x1_ref, h2_ref, rw_ref, re_ref, cnt_ref, carry):
    i = pl.program_id(0)
    x = jnp.where(i >= NP // TW, xs_ref[...], xp_ref[...])
    gt = _mod_rows(i, TW, gtp, gts)
    x1 = x + gt * _dot(m_ref[...], wo_ref[...])
    x1_ref[...] = x1
    sc = _mod_rows(i, TW, scp, scs)
    sh = _mod_rows(i, TW, shp, shs)
    h2 = ((x1 * _rms(x1)) * g_ref[...]) * (1.0 + sc) + sh
    h2_ref[...] = h2.astype(BF16).reshape(TW, D // LANES, LANES)
    hi = h2.astype(BF16)
    lo = (h2 - hi.astype(F32)).astype(BF16)
    both = _dot(hi, wrc_ref[...])
    logits = (both[:, :LANES] + (both[:, LANES:] + _dot(lo, wrh_ref[...]))) + br_ref[...]
    rw, lane, i1, i2 = _route(logits)
    rw_ref[...] = rw

    @pl.when(i == 0)
    def _():
        carry[...] = jnp.zeros_like(carry)

    o1 = (lane == i1).astype(F32)
    o2 = (lane == i2).astype(F32)
    both_hot = o1 + o2
    r = lax.broadcasted_iota(jnp.int32, (TW, TW), 0)
    c = lax.broadcasted_iota(jnp.int32, (TW, TW), 1)
    before = (r > c).astype(BF16)
    seen = _dot(before, both_hot.astype(BF16)) + carry[...]
    rank1 = jnp.sum(seen * o1, axis=-1, keepdims=True)
    rank2 = jnp.sum(seen * o2, axis=-1, keepdims=True)
    total = carry[...] + jnp.sum(both_hot, axis=0, keepdims=True)
    carry[...] = total
    cnt_ref[...] = jnp.broadcast_to(total, cnt_ref.shape).astype(jnp.int32)
    re = jnp.where(lane == 0.0, i1 - N_GROUPS_MOE,
                   jnp.where(lane == 1.0, i2 - N_GROUPS_MOE,
                             jnp.where(lane == 2.0, rank1, jnp.where(lane == 3.0, rank2, 0.0))))
    re_ref[...] = re.astype(jnp.int32)


def _wo(merged, w_o, x_prompt_rows, x_sample_rows, sample_block0, g2, mod, wr_cat, wr_hi, b_r, l):
    row = lambda i: (i, 0)
    n_ptiles = NP // TW
    return pl.pallas_call(
        _wo_kernel,
        out_shape=(jax.ShapeDtypeStruct((NT, D), F32),
                   jax.ShapeDtypeStruct((NT, D // LANES, LANES), BF16),
                   jax.ShapeDtypeStruct((NT, LANES), F32),
                   jax.ShapeDtypeStruct((NT, LANES), jnp.int32),
                   jax.ShapeDtypeStruct((SUBLANES, LANES), jnp.int32)),
        grid=(NT // TW,),
        in_specs=[
            pl.BlockSpec((TW, D), row),
            pl.BlockSpec((D, D), lambda i: (0, 0)),
            pl.BlockSpec((TW, D), lambda i: (jnp.minimum(i, n_ptiles - 1), 0)),
            pl.BlockSpec((TW, D), lambda i: (sample_block0 + jnp.maximum(i - n_ptiles, 0), 0)),
            pl.BlockSpec((None, 1, D), lambda i: (l, 0, 0)),
        ] + _mod_specs(l, 2) + _mod_specs(l, 4) + _mod_specs(l, 3) + [
            pl.BlockSpec((None, D, 2 * LANES), lambda i: (l, 0, 0)),
            pl.BlockSpec((None, D, LANES), lambda i: (l, 0, 0)),
            pl.BlockSpec((None, 1, LANES), lambda i: (l, 0, 0)),
        ],
        out_specs=(pl.BlockSpec((TW, D), row), pl.BlockSpec((TW, D // LANES, LANES), lambda i: (i, 0, 0)),
                   pl.BlockSpec((TW, LANES), row), pl.BlockSpec((TW, LANES), row),
                   pl.BlockSpec((SUBLANES, LANES), lambda i: (0, 0))),
        scratch_shapes=[pltpu.VMEM((1, LANES), F32)],
        compiler_params=_params(("arbitrary",)),
        name="wo_router",
    )(merged, w_o, x_prompt_rows, x_sample_rows, g2, mod, mod, mod, mod, mod, mod, wr_cat, wr_hi, b_r)


def _row_gather_copy(src_hbm, idx, dst, sem):
    return pltpu.make_async_copy(src_hbm.at[pl.ds(idx, 1)], dst, sem)


def _moe_kernel(te_ref, tf_ref, nu_ref, pad_ref, dest_ref, h2_hbm, wg_ref, wu_ref, wd_ref, o_ref,
                src_ref, xbuf, sem, wg_s, wu_s, wd_s):
    del te_ref
    t = pl.program_id(0)
    n_used = nu_ref[0]

    def issue(tile, slot):
        base = tile * TE

        def body(r, carry):
            _row_gather_copy(h2_hbm, src_ref[base + r], xbuf.at[slot, pl.ds(r, 1)], sem.at[slot]).start()
            return carry

        lax.fori_loop(0, TE, body, 0, unroll=8)

    @pl.when(t == 0)
    def _():
        def clear(p, carry):
            src_ref[p] = 0
            return carry

        for e in range(N_EXPERTS):
            lax.fori_loop(pad_ref[e], pad_ref[N_EXPERTS + e], clear, 0)

        def invert(n, carry):
            for k in range(TOP_K):
                src_ref[dest_ref[TOP_K * n + k]] = n
            return carry

        lax.fori_loop(0, NT, invert, 0, unroll=8)
        for k in range(MOE_SLOTS - 1):
            issue(jnp.minimum(k, n_used - 1), k)

    @pl.when((t < n_used) & (tf_ref[t] == 1))
    def _():
        wg_s[...] = wg_ref[...].astype(BF16)
        wu_s[...] = wu_ref[...].astype(BF16)
        wd_s[...] = wd_ref[...].astype(BF16)

    def wait_tile(slot):
        pltpu.make_async_copy(h2_hbm.at[pl.ds(0, TE)], xbuf.at[slot], sem.at[slot]).wait()

    @pl.when(t < n_used)
    def _():
        slot = t % MOE_SLOTS
        wait_tile(slot)
        x = xbuf[slot].reshape(TE, D)
        ahead = (t + MOE_SLOTS - 1) % MOE_SLOTS
        base = jnp.minimum(t + MOE_SLOTS - 1, n_used - 1) * TE
        for r in range(TE):
            _row_gather_copy(h2_hbm, src_ref[base + r], xbuf.at[ahead, pl.ds(r, 1)], sem.at[ahead]).start()
        a = _dot(x, wg_s[...])
        b = _dot(x, wu_s[...])
        hid = (a * _sigmoid(a)) * b
        o_ref[...] = _dot(hid.astype(BF16), wd_s[...])

    @pl.when(t == n_used - 1)
    def _():
        for k in range(1, MOE_SLOTS):
            wait_tile((t + k) % MOE_SLOTS)

    @pl.when(t >= n_used)
    def _():
        o_ref[...] = jnp.zeros_like(o_ref)


def _moe(h2, tile_expert, tile_first, n_used, pads, dest, w_gate, w_up, w_down, l):
    grid_spec = pltpu.PrefetchScalarGridSpec(
        num_scalar_prefetch=5,
        grid=(MAX_TILES,),
        in_specs=[
            pl.BlockSpec(memory_space=pl.ANY),
            pl.BlockSpec((None, None, D, D_FF), lambda t, te, *_: (l, te[t], 0, 0)),
            pl.BlockSpec((None, None, D, D_FF), lambda t, te, *_: (l, te[t], 0, 0)),
            pl.BlockSpec((None, None, D_FF, D), lambda t, te, *_: (l, te[t], 0, 0)),
        ],
        out_specs=pl.BlockSpec((TE, D), lambda t, *_: (t, 0)),
        scratch_shapes=[
            pltpu.SMEM((P_ROWS,), jnp.int32),
            pltpu.VMEM((MOE_SLOTS, TE, D // LANES, LANES), BF16),
            pltpu.SemaphoreType.DMA((MOE_SLOTS,)),
            pltpu.VMEM((D, D_FF), BF16),
            pltpu.VMEM((D, D_FF), BF16),
            pltpu.VMEM((D_FF, D), BF16),
        ],
    )
    return pl.pallas_call(
        _moe_kernel,
        out_shape=jax.ShapeDtypeStruct((P_ROWS, D), F32),
        grid_spec=grid_spec,
        compiler_params=_params(("arbitrary",)),
        name="moe_experts",
    )(tile_expert, tile_first, n_used, pads, dest, h2, w_gate, w_up, w_down)


def _dispatch(route_e, counts_lanes):
    counts = counts_lanes[0, N_GROUPS_MOE:N_GROUPS_MOE + N_EXPERTS]
    tiles = (counts + TE - 1) // TE
    eids = jnp.arange(N_EXPERTS, dtype=jnp.int32)
    tile_end = jnp.sum(jnp.where(eids[None, :] <= eids[:, None], tiles[None, :], 0), axis=1)
    tile_start = tile_end - tiles
    hot = route_e[:, :TOP_K, None] == eids[None, None, :]
    row0 = jnp.sum(jnp.where(hot, tile_start[None, None, :] * TE, 0), axis=-1)
    dest = (row0 + route_e[:, TOP_K:2 * TOP_K]).reshape(-1).astype(jnp.int32)
    tids = jnp.arange(MAX_TILES, dtype=jnp.int32)
    tile_expert = jnp.minimum(jnp.sum((tile_end[None, :] <= tids[:, None]).astype(jnp.int32), axis=1),
                              N_EXPERTS - 1)
    first_tile = jnp.sum(jnp.where(tile_expert[:, None] == eids[None, :], tile_start[None, :], 0), axis=1)
    tile_first = (tids == first_tile).astype(jnp.int32)
    n_used = tile_end[-1:].astype(jnp.int32)
    pads = jnp.concatenate([tile_start * TE + counts, tile_end * TE]).astype(jnp.int32)
    return dest, tile_expert, tile_first, n_used, pads


def _combine_kernel(final, tile0, dest_ref, ys_hbm, x_ref, rw_ref, g_ref, gtp, gts, *rest):
    if final:
        y_ref, ybuf, sem = rest
    else:
        scp, scs, shp, shs, x2_ref, h_ref, ybuf, sem = rest
    step = pl.program_id(0)
    nsteps = pl.num_programs(0)
    i = step + tile0

    def issue(tile, slot, r):
        for k in range(TOP_K):
            _row_gather_copy(ys_hbm, dest_ref[tile * (TK * TOP_K) + TOP_K * r + k],
                             ybuf.at[slot, k, pl.ds(r, 1)], sem.at[slot]).start()

    def wait_tile(slot):
        for k in range(TOP_K):
            pltpu.make_async_copy(ys_hbm.at[pl.ds(0, TK)], ybuf.at[slot, k], sem.at[slot]).wait()

    last = tile0 + nsteps - 1

    @pl.when(step == 0)
    def _():
        for k in range(GATHER_SLOTS - 1):
            def body(r, carry, k=k):
                issue(jnp.minimum(i + k, last), k, r)
                return carry

            lax.fori_loop(0, TK, body, 0, unroll=8)

    slot = step % GATHER_SLOTS
    wait_tile(slot)
    ahead = (step + GATHER_SLOTS - 1) % GATHER_SLOTS
    nxt = jnp.minimum(i + GATHER_SLOTS - 1, last)
    for r in range(TK):
        issue(nxt, ahead, r)
    gt = _mod_rows(i, TK, gtp, gts)
    rw = rw_ref[...]
    moe = rw[:, 0:1] * ybuf[slot, 0] + rw[:, 1:2] * ybuf[slot, 1]
    x2 = x_ref[...] + gt * moe
    xn = (x2 * _rms(x2)) * g_ref[...]
    if final:
        y_ref[...] = xn
    else:
        x2_ref[...] = x2
        sc = _mod_rows(i, TK, scp, scs)
        sh = _mod_rows(i, TK, shp, shs)
        h_ref[...] = (xn * (1.0 + sc) + sh).astype(BF16)

    @pl.when(step == nsteps - 1)
    def _():
        for k in range(1, GATHER_SLOTS):
            wait_tile((step + k) % GATHER_SLOTS)


def _combine(dest, ys, x1, route_w, g, mod, l, final, tile0=0, ntiles=NT // TK):
    row_in = lambda s, d: (s + tile0, 0)
    row_out = lambda s, d: (s, 0)
    nmod = lambda q, ll: [
        pl.BlockSpec((None, SUBLANES, D), lambda s, d: (ll, MOD_PROMPT_BLOCK, q)),
        pl.BlockSpec((None, DEC_BATCH, D), lambda s, d: (ll, 0, q)),
    ]
    in_specs = [pl.BlockSpec(memory_space=pl.ANY), pl.BlockSpec((TK, D), row_in),
                pl.BlockSpec((TK, LANES), row_in),
                pl.BlockSpec((1, D), lambda s, d: (0, 0))] + nmod(5, l)
    args = [ys, x1, route_w, g, mod, mod]
    rows = ntiles * TK
    if final:
        out_shape = jax.ShapeDtypeStruct((rows, D), F32)
        out_specs = pl.BlockSpec((TK, D), row_out)
    else:
        in_specs += nmod(1, l + 1) + nmod(0, l + 1)
        args += [mod, mod, mod, mod]
        out_shape = (jax.ShapeDtypeStruct((rows, D), F32), jax.ShapeDtypeStruct((rows, D), BF16))
        out_specs = (pl.BlockSpec((TK, D), row_out), pl.BlockSpec((TK, D), row_out))
    grid_spec = pltpu.PrefetchScalarGridSpec(
        num_scalar_prefetch=1,
        grid=(ntiles,),
        in_specs=in_specs,
        out_specs=out_specs,
        scratch_shapes=[pltpu.VMEM((GATHER_SLOTS, TOP_K, TK, D), F32),
                        pltpu.SemaphoreType.DMA((GATHER_SLOTS,))],
    )
    return pl.pallas_call(
        functools.partial(_combine_kernel, final, tile0),
        out_shape=out_shape,
        grid_spec=grid_spec,
        compiler_params=_params(("arbitrary",)),
        name=("combine_final_%d" % tile0) if final else "combine",
    )(dest, *args)


def kernel(x_prompt, x_sample, c_prompt, c_sample, state_conv, state_pool, w_ada, b_ada, g_norm1, w_in,
           conv_w, pool_w, pool_scale, g_v, w_s, b_s, w_br, w_o, g_norm2, w_rg, b_rg, w_re, b_re,
           w_gate, w_up, w_down, g_final):
    xp_rows = x_prompt.reshape(NP, D)
    xs_rows = x_sample.transpose(1, 0, 2).reshape(NS, D)
    sample_block0 = 0
    c_all = jnp.concatenate([c_sample, c_prompt,
                             jnp.zeros((MOD_ROWS - DEC_BATCH - BATCH, D), F32)], axis=0).astype(BF16)
    mod = _ada(c_all, w_ada, b_ada)

    pool_w_b = pool_w.astype(BF16)
    tril = jnp.tril(jnp.ones((CHUNK, CHUNK), dtype=bool))
    ws_tril = jnp.where(tril[None, None], w_s, 0.0).astype(BF16)
    bs_tile = jnp.broadcast_to(b_s[:, :, :, None], (DEPTH, HEADS, CHUNK, CHUNK))
    small = jnp.tril(jnp.ones((DEC_SEQ, DEC_SEQ), dtype=bool))
    ws_small = jnp.where(small[None, None], w_s[:, :, :DEC_SEQ, :DEC_SEQ], 0.0)
    wvec = jnp.repeat(ws_small.transpose(0, 2, 3, 1).reshape(DEPTH, DEC_SEQ * DEC_SEQ, HEADS), CHUNK, axis=-1)
    bvec = jnp.repeat(b_s[:, :, :DEC_SEQ].transpose(0, 2, 1), CHUNK, axis=-1)
    w_r = jnp.concatenate([w_rg, w_re, jnp.zeros((DEPTH, D, LANES - N_GROUPS_MOE - N_EXPERTS), F32)], axis=-1)
    wr_hi = w_r.astype(BF16)
    wr_lo = (w_r - wr_hi.astype(F32)).astype(BF16)
    wr_cat = jnp.concatenate([wr_hi, wr_lo], axis=-1)
    b_r = jnp.concatenate([b_rg, b_re, jnp.zeros((DEPTH, LANES - N_GROUPS_MOE - N_EXPERTS), F32)],
                          axis=-1).reshape(DEPTH, 1, LANES)
    conv_t = state_conv.transpose(0, 2, 1, 3)
    pool_t = state_pool.transpose(0, 2, 1, 3)
    g1 = g_norm1.reshape(DEPTH, 1, D)
    g2 = g_norm2.reshape(DEPTH, 1, D)
    gv = g_v.reshape(DEPTH, 1, W)
    pscale = pool_scale.reshape(DEPTH, 1, W)

    h = _norm(xp_rows, xs_rows, g1, mod, 0)
    conv_p, pool_p, conv_s, pool_s, v_s = [], [], [], [], []
    y_prompt = y_sample = None
    for l in range(DEPTH):
        ya, cst_p, w_gates_b = _conv_p(h, w_in, conv_w, l)
        yb, pst_p, w_br_b = _pool_p(h, w_in, pool_w_b, pscale, w_br, l)
        yc, w_o_b = _gmlp_p(h, w_in, gv, ws_tril, bs_tile, w_o, l)
        ya_s, yb_s, cst_s, pst_s = _mix_s(h, w_in, conv_w, pool_w_b, pscale, conv_t, pool_t, l)
        yc_s, vn_s = _gmlp_s(h, w_in, gv, wvec, bvec, l)
        merged = _merge(h, (ya, yb, yc), (ya_s, yb_s, yc_s), w_gates_b, w_br_b)
        x1, h2, route_w, route_e, counts = _wo(merged, w_o_b, xp_rows, xs_rows, sample_block0, g2, mod,
                                               wr_cat, wr_hi, b_r, l)
        dest, tile_expert, tile_first, n_used, pads = _dispatch(route_e, counts)
        ys = _moe(h2, tile_expert, tile_first, n_used, pads, dest, w_gate, w_up, w_down, l)
        if l + 1 < DEPTH:
            x, h = _combine(dest, ys, x1, route_w, g1[l + 1], mod, l, final=False)
            xp_rows = xs_rows = x
            sample_block0 = NP // TW
        else:
            gf = g_final.reshape(1, D)
            y_prompt = _combine(dest, ys, x1, route_w, gf, mod, l, final=True, tile0=0, ntiles=NP // TK)
            y_sample = _combine(dest, ys, x1, route_w, gf, mod, l, final=True, tile0=NP // TK,
                                ntiles=NS // TK)
        conv_p.append(cst_p[:, CONV_TAIL - (CONV_K - 1):, :])
        pool_p.append(pst_p[:, POOL_TAIL - POOL_STATE:, :])
        conv_s.append(cst_s.transpose(1, 0, 2))
        pool_s.append(pst_s.transpose(1, 0, 2))
        v_s.append(vn_s.reshape(DEC_SEQ, DEC_BATCH, W).transpose(1, 0, 2))

    y_prompt = y_prompt.reshape(BATCH, SEQ, D)
    y_sample = y_sample.reshape(DEC_SEQ, DEC_BATCH, D).transpose(1, 0, 2)
    return (y_prompt, y_sample, jnp.stack(conv_p), jnp.stack(pool_p), jnp.stack(conv_s),
            jnp.stack(pool_s), jnp.stack(v_s))
```

```python
import functools

import jax
import jax.numpy as jnp
from jax import lax
from jax.experimental import pallas as pl
from jax.experimental.pallas import tpu as pltpu

F32 = jnp.float32
BF16 = jnp.bfloat16

D = 2048
BATCH = 4
SEQ = 2048
DEPTH = 2
DEC_BATCH = 128
DEC_SEQ = 4
W = D // 2
CONV_K = 3
POOL_STATE = 15
N_POOL_GROUPS = 4
POOL_GROUP = W // N_POOL_GROUPS
CHUNK = 128
HEADS = W // CHUNK
N_GROUPS_MOE = 4
EXP_PER_GROUP = 4
N_EXPERTS = 16
TOP_K = 2
D_FF = D // 4
EPS = 1e-6

NP = BATCH * SEQ
NS = DEC_BATCH * DEC_SEQ
NT = NP + NS
TM = 512
N_TILES = NT // TM
CB = 256
MIX_ROWS = 256
POOL_SHIFTS = (1, 2, 4, 8)
TE = 256
MAX_TILES = (NT * TOP_K) // TE + N_EXPERTS
P_ROWS = MAX_TILES * TE
TK = 256
TW = 256
GATHER_SLOTS = 3
MOE_SLOTS = 4
LANES = 128
SUBLANES = 8
CONV_TAIL = SUBLANES
POOL_TAIL = 2 * SUBLANES
MOD_ROWS = 144
MOD_PROMPT_BLOCK = DEC_BATCH // SUBLANES

VMEM_LIMIT_V7X = 56 * 1024 * 1024


def _dot(a, b):
    return jnp.dot(a, b, preferred_element_type=F32)


def _params(sem, vmem=VMEM_LIMIT_V7X):
    return pltpu.CompilerParams(dimension_semantics=sem, vmem_limit_bytes=vmem)


def _sigmoid(x):
    return 1.0 / (1.0 + jnp.exp(-x))


def _rms(x):
    return lax.rsqrt(jnp.mean(x * x, axis=-1, keepdims=True) + EPS)


def _ada_kernel(c_ref, w_ref, b_ref, o_ref):
    o_ref[...] = _dot(c_ref[...], w_ref[...].astype(BF16)) + b_ref[...]


def _ada(c_all, w_ada, b_ada):
    nb = 1024
    return pl.pallas_call(
        _ada_kernel,
        out_shape=jax.ShapeDtypeStruct((DEPTH, MOD_ROWS, 6 * D), F32),
        grid=(DEPTH, 6 * D // nb),
        in_specs=[
            pl.BlockSpec((MOD_ROWS, D), lambda l, j: (0, 0)),
            pl.BlockSpec((None, D, nb), lambda l, j: (l, 0, j)),
            pl.BlockSpec((None, 1, nb), lambda l, j: (l, 0, j)),
        ],
        out_specs=pl.BlockSpec((None, MOD_ROWS, nb), lambda l, j: (l, 0, j)),
        compiler_params=_params(("arbitrary", "arbitrary")),
        name="ada",
    )(c_all, w_ada, b_ada.reshape(DEPTH, 1, 6 * D))


def _mod_specs(l, q):
    return [
        pl.BlockSpec((None, SUBLANES, D), lambda i, *_: (l, MOD_PROMPT_BLOCK, q)),
        pl.BlockSpec((None, DEC_BATCH, D), lambda i, *_: (l, 0, q)),
    ]


def _mod_rows(i, tile_rows, mp_ref, ms_ref):
    tiles_per_seq = SEQ // tile_rows
    is_sample = i >= NP // tile_rows
    b = jnp.minimum(i // tiles_per_seq, BATCH - 1)
    mp = mp_ref[pl.ds(b, 1), :]
    ms = jnp.concatenate([ms_ref[...]] * (tile_rows // DEC_BATCH), axis=0)
    return jnp.where(is_sample, ms, mp)


def _norm_kernel(xp_ref, xs_ref, g_ref, scp, scs, shp, shs, h_ref):
    i = pl.program_id(0)
    x = jnp.where(i >= NP // TM, xs_ref[...], xp_ref[...])
    sc = _mod_rows(i, TM, scp, scs)
    sh = _mod_rows(i, TM, shp, shs)
    h_ref[...] = (((x * _rms(x)) * g_ref[...]) * (1.0 + sc) + sh).astype(BF16)


def _norm(x_prompt_rows, x_sample_rows, g, mod, l):
    n_ptiles = NP // TM
    return pl.pallas_call(
        _norm_kernel,
        out_shape=jax.ShapeDtypeStruct((NT, D), BF16),
        grid=(N_TILES,),
        in_specs=[pl.BlockSpec((TM, D), lambda i: (jnp.minimum(i, n_ptiles - 1), 0)),
                  pl.BlockSpec((TM, D), lambda i: (jnp.maximum(i - n_ptiles, 0), 0)),
                  pl.BlockSpec((None, 1, D), lambda i: (l, 0, 0))]
                 + _mod_specs(l, 1) + _mod_specs(l, 0),
        out_specs=pl.BlockSpec((TM, D), lambda i: (i, 0)),
        compiler_params=_params(("arbitrary",)),
        name="norm1",
    )(x_prompt_rows, x_sample_rows, g, mod, mod, mod, mod)


def _shift_rows(cur, prev, s, row):
    head = 0.0 if prev is None else pltpu.roll(prev, s, 0)
    return jnp.where(row >= s, pltpu.roll(cur, s, 0), head)


def _conv_p_kernel(h_ref, wb_ref, wc_ref, wh_ref, cw_ref, wcast_ref, y_ref, st_ref, wcast_out):
    wcast_out[...] = wcast_ref[...].astype(BF16)
    wb = wb_ref[...].astype(BF16)
    wc = wc_ref[...].astype(BF16)
    wh = wh_ref[...].astype(BF16)
    cw = cw_ref[...]
    row = lax.broadcasted_iota(jnp.int32, (MIX_ROWS, 1), 0)
    prev = None
    for r0 in range(0, SEQ, MIX_ROWS):
        h = h_ref[r0:r0 + MIX_ROWS, :]
        z = _dot(h, wc) * _dot(h, wh)
        y = cw[0:1] * _shift_rows(z, prev, 2, row) + cw[1:2] * _shift_rows(z, prev, 1, row) + cw[2:3] * z
        y_ref[r0:r0 + MIX_ROWS, :] = (_dot(h, wb) * y).astype(BF16)
        prev = z
    st_ref[...] = prev[MIX_ROWS - CONV_TAIL:, :]


def _conv_p(h, w_in, conv_w, l):
    nc = W // CB
    steps = BATCH * nc
    gcols = 3 * D // steps
    g0 = 6 * W // gcols
    return pl.pallas_call(
        _conv_p_kernel,
        out_shape=(jax.ShapeDtypeStruct((NP, W), BF16),
                   jax.ShapeDtypeStruct((BATCH, CONV_TAIL, W), F32),
                   jax.ShapeDtypeStruct((D, 3 * D), BF16)),
        grid=(BATCH, nc),
        in_specs=[
            pl.BlockSpec((SEQ, D), lambda b, c: (b, 0)),
            pl.BlockSpec((None, D, CB), lambda b, c: (l, 0, c)),
            pl.BlockSpec((None, D, CB), lambda b, c: (l, 0, nc + c)),
            pl.BlockSpec((None, D, CB), lambda b, c: (l, 0, 2 * nc + c)),
            pl.BlockSpec((None, CONV_K, CB), lambda b, c: (l, 0, c)),
            pl.BlockSpec((None, D, gcols), lambda b, c: (l, 0, g0 + b * nc + c)),
        ],
        out_specs=(pl.BlockSpec((SEQ, CB), lambda b, c: (b, c)),
                   pl.BlockSpec((None, CONV_TAIL, CB), lambda b, c: (b, 0, c)),
                   pl.BlockSpec((D, gcols), lambda b, c: (0, b * nc + c))),
        compiler_params=_params(("arbitrary", "arbitrary")),
        name="conv_prompt",
    )(h, w_in, w_in, w_in, conv_w, w_in)


def _pool_p_kernel(h_ref, wp_ref, pw_ref, ps_ref, wcast_ref, y_ref, st_ref, wcast_out):
    wcast_out[...] = wcast_ref[...].astype(BF16)
    g = pl.program_id(1)
    wp = wp_ref[...].astype(BF16)
    ps = [_dot(h_ref[r0:r0 + MIX_ROWS, :], wp) for r0 in range(0, SEQ, MIX_ROWS)]
    row = lax.broadcasted_iota(jnp.int32, (MIX_ROWS, 1), 0)
    window = jnp.left_shift(2, g)
    pw = pw_ref[...]
    scale = ps_ref[...]
    prev = [None] * len(POOL_SHIFTS)
    for r0, p in zip(range(0, SEQ, MIX_ROWS), ps):
        sums = [p]
        for level, s in enumerate(POOL_SHIFTS):
            sums.append(sums[-1] + _shift_rows(sums[-1], prev[level], s, row))
        prev = sums[:-1]
        total = jnp.where(g >= 3, sums[4], jnp.where(g >= 2, sums[3], jnp.where(g >= 1, sums[2], sums[1])))
        cnt = jnp.minimum(row + (r0 + 1), window).astype(F32)
        d = total / cnt - p
        y_ref[r0:r0 + MIX_ROWS, :] = (_dot(d.astype(BF16), pw) * scale).astype(BF16)
    st_ref[...] = ps[-1][MIX_ROWS - POOL_TAIL:, :]


def _pool_p(h, w_in, pool_w, pool_scale, w_br, l):
    col0 = 3 * W // CB
    bcols = D // (BATCH * N_POOL_GROUPS)
    return pl.pallas_call(
        _pool_p_kernel,
        out_shape=(jax.ShapeDtypeStruct((NP, W), BF16),
                   jax.ShapeDtypeStruct((BATCH, POOL_TAIL, W), F32),
                   jax.ShapeDtypeStruct((3, W, D), BF16)),
        grid=(BATCH, N_POOL_GROUPS),
        in_specs=[
            pl.BlockSpec((SEQ, D), lambda b, g: (b, 0)),
            pl.BlockSpec((None, D, CB), lambda b, g: (l, 0, col0 + g)),
            pl.BlockSpec((None, None, POOL_GROUP, POOL_GROUP), lambda b, g: (l, g, 0, 0)),
            pl.BlockSpec((None, 1, CB), lambda b, g: (l, 0, g)),
            pl.BlockSpec((None, 3, W, bcols), lambda b, g: (l, 0, 0, b * N_POOL_GROUPS + g)),
        ],
        out_specs=(pl.BlockSpec((SEQ, CB), lambda b, g: (b, g)),
                   pl.BlockSpec((None, POOL_TAIL, CB), lambda b, g: (b, 0, g)),
                   pl.BlockSpec((3, W, bcols), lambda b, g: (0, 0, b * N_POOL_GROUPS + g))),
        compiler_params=_params(("arbitrary", "arbitrary")),
        name="pool_prompt",
    )(h, w_in, pool_w, pool_scale, w_br)


def _gmlp_p_kernel(h_ref, wv_ref, wu_ref, gv_ref, ws_ref, bs_ref, wcast_ref, y_ref, wcast_out, vn_s):
    wcast_out[...] = wcast_ref[...].astype(BF16)
    j = pl.program_id(1)

    @pl.when(j == 0)
    def _():
        wv = wv_ref[...].astype(BF16)
        for r0 in range(0, SEQ, TM):
            v = _dot(h_ref[r0:r0 + TM, :], wv)
            vn = (v * _rms(v)) * gv_ref[...]
            for hd in range(HEADS):
                vn_s[hd, r0:r0 + TM, :] = vn[:, hd * CHUNK:(hd + 1) * CHUNK].astype(BF16)

    u = _dot(h_ref[...], wu_ref[...].astype(BF16))
    heads_per_block = CB // CHUNK
    for hh in range(heads_per_block):
        head = heads_per_block * j + hh
        wsh = ws_ref[head]
        bsh = bs_ref[head]
        for n in range(SEQ // CHUNK):
            rows = slice(n * CHUNK, (n + 1) * CHUNK)
            cols = slice(hh * CHUNK, (hh + 1) * CHUNK)
            s = _dot(wsh, vn_s[head, pl.ds(n * CHUNK, CHUNK), :]) + bsh
            y_ref[rows, cols] = (u[rows, cols] * s).astype(BF16)


def _gmlp_p(h, w_in, g_v, ws_tril, bs_tile, w_o, l):
    ucol0 = 4 * W // CB
    nj = W // CB
    ocols = D // (BATCH * nj)
    return pl.pallas_call(
        _gmlp_p_kernel,
        out_shape=(jax.ShapeDtypeStruct((NP, W), BF16), jax.ShapeDtypeStruct((D, D), BF16)),
        grid=(BATCH, nj),
        in_specs=[
            pl.BlockSpec((SEQ, D), lambda b, j: (b, 0)),
            pl.BlockSpec((None, D, W), lambda b, j: (l, 0, 5), pipeline_mode=pl.Buffered(1)),
            pl.BlockSpec((None, D, CB), lambda b, j: (l, 0, ucol0 + j)),
            pl.BlockSpec((None, 1, W), lambda b, j: (l, 0, 0)),
            pl.BlockSpec((None, HEADS, CHUNK, CHUNK), lambda b, j: (l, 0, 0, 0)),
            pl.BlockSpec((None, HEADS, CHUNK, CHUNK), lambda b, j: (l, 0, 0, 0)),
            pl.BlockSpec((None, D, ocols), lambda b, j: (l, 0, b * nj + j)),
        ],
        out_specs=(pl.BlockSpec((SEQ, CB), lambda b, j: (b, j)),
                   pl.BlockSpec((D, ocols), lambda b, j: (0, b * nj + j))),
        scratch_shapes=[pltpu.VMEM((HEADS, SEQ, CHUNK), BF16)],
        compiler_params=_params(("arbitrary", "arbitrary")),
        name="gmlp_prompt",
    )(h, w_in, w_in, g_v, ws_tril, bs_tile, w_o)


def _mix_s_kernel(h_ref, wb_ref, wc_ref, wh_ref, wp_ref, cw_ref, pc_ref, pp_ref,
                  pw_ref, ps_ref, ya_ref, yb_ref, cst_ref, pst_ref):
    g = pl.program_id(0)
    h = h_ref[...]
    nb = DEC_BATCH
    z = _dot(h, wc_ref[...].astype(BF16)) * _dot(h, wh_ref[...].astype(BF16))
    bg = _dot(h, wb_ref[...].astype(BF16))
    zs = [pc_ref[0], pc_ref[1]] + [z[t * nb:(t + 1) * nb] for t in range(DEC_SEQ)]
    cw = cw_ref[...]
    for t in range(DEC_SEQ):
        y = cw[0:1] * zs[t] + cw[1:2] * zs[t + 1] + cw[2:3] * zs[t + 2]
        ya_ref[t * nb:(t + 1) * nb, :] = (bg[t * nb:(t + 1) * nb] * y).astype(BF16)
    cst_ref[0] = zs[DEC_SEQ]
    cst_ref[1] = zs[DEC_SEQ + 1]
    p = _dot(h, wp_ref[...].astype(BF16))
    pp = [pp_ref[k] for k in range(POOL_STATE)] + [p[t * nb:(t + 1) * nb] for t in range(DEC_SEQ)]
    window = jnp.left_shift(2, g).astype(F32)
    ds = []
    for t in range(DEC_SEQ):
        e = POOL_STATE + t
        s = pp[e] + pp[e - 1]
        s4 = s + (pp[e - 2] + pp[e - 3])
        s8 = s4 + ((pp[e - 4] + pp[e - 5]) + (pp[e - 6] + pp[e - 7]))
        s16 = s8 + (((pp[e - 8] + pp[e - 9]) + (pp[e - 10] + pp[e - 11]))
                    + ((pp[e - 12] + pp[e - 13]) + (pp[e - 14] + pp[e - 15])))
        s = jnp.where(g >= 1, s4, s)
        s = jnp.where(g >= 2, s8, s)
        s = jnp.where(g >= 3, s16, s)
        ds.append(s / window - pp[e])
    d = jnp.concatenate(ds, axis=0)
    yb_ref[...] = (_dot(d.astype(BF16), pw_ref[...]) * ps_ref[...]).astype(BF16)
    for k in range(POOL_STATE):
        pst_ref[k] = pp[DEC_SEQ + k]


def _mix_s(h, w_in, conv_w, pool_w, pool_scale, conv_t, pool_t, l):
    nc = W // CB
    srow = NP // NS
    return pl.pallas_call(
        _mix_s_kernel,
        out_shape=(jax.ShapeDtypeStruct((NS, W), BF16),
                   jax.ShapeDtypeStruct((NS, W), BF16),
                   jax.ShapeDtypeStruct((CONV_K - 1, DEC_BATCH, W), F32),
                   jax.ShapeDtypeStruct((POOL_STATE, DEC_BATCH, W), F32)),
        grid=(nc,),
        in_specs=[
            pl.BlockSpec((NS, D), lambda c: (srow, 0)),
            pl.BlockSpec((None, D, CB), lambda c: (l, 0, c)),
            pl.BlockSpec((None, D, CB), lambda c: (l, 0, nc + c)),
            pl.BlockSpec((None, D, CB), lambda c: (l, 0, 2 * nc + c)),
            pl.BlockSpec((None, D, CB), lambda c: (l, 0, 3 * nc + c)),
            pl.BlockSpec((None, CONV_K, CB), lambda c: (l, 0, c)),
            pl.BlockSpec((None, CONV_K - 1, DEC_BATCH, CB), lambda c: (l, 0, 0, c)),
            pl.BlockSpec((None, POOL_STATE, DEC_BATCH, CB), lambda c: (l, 0, 0, c)),
            pl.BlockSpec((None, None, POOL_GROUP, POOL_GROUP), lambda c: (l, c, 0, 0)),
            pl.BlockSpec((None, 1, CB), lambda c: (l, 0, c)),
        ],
        out_specs=(pl.BlockSpec((NS, CB), lambda c: (0, c)),
                   pl.BlockSpec((NS, CB), lambda c: (0, c)),
                   pl.BlockSpec((CONV_K - 1, DEC_BATCH, CB), lambda c: (0, 0, c)),
                   pl.BlockSpec((POOL_STATE, DEC_BATCH, CB), lambda c: (0, 0, c))),
        compiler_params=_params(("arbitrary",)),
        name="mix_sample",
    )(h, w_in, w_in, w_in, w_in, conv_w, conv_t, pool_t, pool_w, pool_scale)


def _gmlp_s_kernel(h_ref, wu_ref, wv_ref, gv_ref, wvec_ref, bvec_ref, yc_ref, vn_ref):
    h = h_ref[...]
    nb = DEC_BATCH
    v = _dot(h, wv_ref[...].astype(BF16))
    vn = (v * _rms(v)) * gv_ref[...]
    vn_ref[...] = vn
    u = _dot(h, wu_ref[...].astype(BF16))
    for t in range(DEC_SEQ):
        s = bvec_ref[t:t + 1, :]
        for sp in range(t + 1):
            k = t * DEC_SEQ + sp
            s = s + wvec_ref[k:k + 1, :] * vn[sp * nb:(sp + 1) * nb]
        yc_ref[t * nb:(t + 1) * nb, :] = (u[t * nb:(t + 1) * nb] * s).astype(BF16)


def _gmlp_s(h, w_in, g_v, wvec, bvec, l):
    srow = NP // NS
    return pl.pallas_call(
        _gmlp_s_kernel,
        out_shape=(jax.ShapeDtypeStruct((NS, W), BF16),
                   jax.ShapeDtypeStruct((NS, W), F32)),
        grid=(1,),
        in_specs=[
            pl.BlockSpec((NS, D), lambda i: (srow, 0)),
            pl.BlockSpec((None, D, W), lambda i: (l, 0, 4)),
            pl.BlockSpec((None, D, W), lambda i: (l, 0, 5)),
            pl.BlockSpec((None, 1, W), lambda i: (l, 0, 0)),
            pl.BlockSpec((None, DEC_SEQ * DEC_SEQ, W), lambda i: (l, 0, 0)),
            pl.BlockSpec((None, DEC_SEQ, W), lambda i: (l, 0, 0)),
        ],
        out_specs=(pl.BlockSpec((NS, W), lambda i: (0, 0)),
                   pl.BlockSpec((NS, W), lambda i: (0, 0))),
        compiler_params=_params(("arbitrary",)),
        name="gmlp_sample",
    )(h, w_in, w_in, g_v, wvec, bvec)


def _merge_kernel(h_ref, yap, ybp, ycp, yas, ybs, ycs, wg0, wg1, wg2, wbr_ref, o_ref):
    is_sample = pl.program_id(0) >= NP // TM
    h = h_ref[...]
    acc = None
    for n, (yp, ys, wg) in enumerate(((yap, yas, wg0), (ybp, ybs, wg1), (ycp, ycs, wg2))):
        y = jnp.where(is_sample, ys[...], yp[...])
        term = _sigmoid(_dot(h, wg[...])) * _dot(y, wbr_ref[n])
        acc = term if acc is None else acc + term
    o_ref[...] = acc.astype(BF16)


def _merge(h, y_prompt, y_sample, w_gates, w_br):
    db = 512
    g0 = 0
    gstep = D // db
    return pl.pallas_call(
        _merge_kernel,
        out_shape=jax.ShapeDtypeStruct((NT, D), BF16),
        grid=(N_TILES, D // db),
        in_specs=[
            pl.BlockSpec((TM, D), lambda i, d: (i, 0)),
            pl.BlockSpec((TM, W), lambda i, d: (jnp.minimum(i, NP // TM - 1), 0)),
            pl.BlockSpec((TM, W), lambda i, d: (jnp.minimum(i, NP // TM - 1), 0)),
            pl.BlockSpec((TM, W), lambda i, d: (jnp.minimum(i, NP // TM - 1), 0)),
            pl.BlockSpec((NS, W), lambda i, d: (0, 0)),
            pl.BlockSpec((NS, W), lambda i, d: (0, 0)),
            pl.BlockSpec((NS, W), lambda i, d: (0, 0)),
            pl.BlockSpec((D, db), lambda i, d: (0, g0 + d)),
            pl.BlockSpec((D, db), lambda i, d: (0, g0 + gstep + d)),
            pl.BlockSpec((D, db), lambda i, d: (0, g0 + 2 * gstep + d)),
            pl.BlockSpec((3, W, db), lambda i, d: (0, 0, d)),
        ],
        out_specs=pl.BlockSpec((TM, db), lambda i, d: (i, d)),
        compiler_params=_params(("arbitrary", "arbitrary")),
        name="merge",
    )(h, *y_prompt, *y_sample, w_gates, w_gates, w_gates, w_br)


def _route(logits):
    lane = lax.broadcasted_iota(jnp.int32, logits.shape, 1).astype(F32)
    neg = -jnp.inf
    big = float(LANES)
    is_grp = lane < N_GROUPS_MOE
    gl = jnp.where(is_grp, logits, neg)
    gmax = jnp.max(gl, axis=-1, keepdims=True)
    gsel = jnp.min(jnp.where(gl == gmax, lane, big), axis=-1, keepdims=True)
    gp = 1.0 / jnp.sum(jnp.where(is_grp, jnp.exp(logits - gmax), 0.0), axis=-1, keepdims=True)
    lo = N_GROUPS_MOE + gsel * EXP_PER_GROUP
    in_grp = (lane >= lo) & (lane < lo + EXP_PER_GROUP)
    el = jnp.where(in_grp, logits, neg)
    m1 = jnp.max(el, axis=-1, keepdims=True)
    i1 = jnp.min(jnp.where(el == m1, lane, big), axis=-1, keepdims=True)
    el2 = jnp.where(lane == i1, neg, el)
    m2 = jnp.max(el2, axis=-1, keepdims=True)
    i2 = jnp.min(jnp.where(el2 == m2, lane, big), axis=-1, keepdims=True)
    e = jnp.exp(m2 - m1)
    w1 = gp / (1.0 + e)
    w2 = gp * (e / (1.0 + e))
    rw = jnp.where(lane == 0.0, w1, jnp.where(lane == 1.0, w2, 0.0))
    return rw, lane, i1, i2


def _wo_kernel(m_ref, wo_ref, xp_ref, xs_ref, g_ref, gtp, gts, scp, scs, shp, shs, wrc_ref, wrh_ref, br_ref,
               x1_ref, h2_ref, rw_ref, re_ref, cnt_ref, carry):
    i = pl.program_id(0)
    x = jnp.where(i >= NP // TW, xs_ref[...], xp_ref[...])
    gt = _mod_rows(i, TW, gtp, gts)
    x1 = x + gt * _dot(m_ref[...], wo_ref[...])
    x1_ref[...] = x1
    sc = _mod_rows(i, TW, scp, scs)
    sh = _mod_rows(i, TW, shp, shs)
    h2 = ((x1 * _rms(x1)) * g_ref[...]) * (1.0 + sc) + sh
    h2_ref[...] = h2.astype(BF16).reshape(TW, D // LANES, LANES)
    hi = h2.astype(BF16)
    lo = (h2 - hi.astype(F32)).astype(BF16)
    both = _dot(hi, wrc_ref[...])
    logits = (both[:, :LANES] + (both[:, LANES:] + _dot(lo, wrh_ref[...]))) + br_ref[...]
    rw, lane, i1, i2 = _route(logits)
    rw_ref[...] = rw

    @pl.when(i == 0)
    def _():
        carry[...] = jnp.zeros_like(carry)

    o1 = (lane == i1).astype(F32)
    o2 = (lane == i2).astype(F32)
    both_hot = o1 + o2
    r = lax.broadcasted_iota(jnp.int32, (TW, TW), 0)
    c = lax.broadcasted_iota(jnp.int32, (TW, TW), 1)
    before = (r > c).astype(BF16)
    seen = _dot(before, both_hot.astype(BF16)) + carry[...]
    rank1 = jnp.sum(seen * o1, axis=-1, keepdims=True)
    rank2 = jnp.sum(seen * o2, axis=-1, keepdims=True)
    total = carry[...] + jnp.sum(both_hot, axis=0, keepdims=True)
    carry[...] = total
    cnt_ref[...] = jnp.broadcast_to(total, cnt_ref.shape).astype(jnp.int32)
    re = jnp.where(lane == 0.0, i1 - N_GROUPS_MOE,
                   jnp.where(lane == 1.0, i2 - N_GROUPS_MOE,
                             jnp.where(lane == 2.0, rank1, jnp.where(lane == 3.0, rank2, 0.0))))
    re_ref[...] = re.astype(jnp.int32)


def _wo(merged, w_o, x_prompt_rows, x_sample_rows, sample_block0, g2, mod, wr_cat, wr_hi, b_r, l):
    row = lambda i: (i, 0)
    n_ptiles = NP // TW
    return pl.pallas_call(
        _wo_kernel,
        out_shape=(jax.ShapeDtypeStruct((NT, D), F32),
                   jax.ShapeDtypeStruct((NT, D // LANES, LANES), BF16),
                   jax.ShapeDtypeStruct((NT, LANES), F32),
                   jax.ShapeDtypeStruct((NT, LANES), jnp.int32),
                   jax.ShapeDtypeStruct((SUBLANES, LANES), jnp.int32)),
        grid=(NT // TW,),
        in_specs=[
            pl.BlockSpec((TW, D), row),
            pl.BlockSpec((D, D), lambda i: (0, 0)),
            pl.BlockSpec((TW, D), lambda i: (jnp.minimum(i, n_ptiles - 1), 0)),
            pl.BlockSpec((TW, D), lambda i: (sample_block0 + jnp.maximum(i - n_ptiles, 0), 0)),
            pl.BlockSpec((None, 1, D), lambda i: (l, 0, 0)),
        ] + _mod_specs(l, 2) + _mod_specs(l, 4) + _mod_specs(l, 3) + [
            pl.BlockSpec((None, D, 2 * LANES), lambda i: (l, 0, 0)),
            pl.BlockSpec((None, D, LANES), lambda i: (l, 0, 0)),
            pl.BlockSpec((None, 1, LANES), lambda i: (l, 0, 0)),
        ],
        out_specs=(pl.BlockSpec((TW, D), row), pl.BlockSpec((TW, D // LANES, LANES), lambda i: (i, 0, 0)),
                   pl.BlockSpec((TW, LANES), row), pl.BlockSpec((TW, LANES), row),
                   pl.BlockSpec((SUBLANES, LANES), lambda i: (0, 0))),
        scratch_shapes=[pltpu.VMEM((1, LANES), F32)],
        compiler_params=_params(("arbitrary",)),
        name="wo_router",
    )(merged, w_o, x_prompt_rows, x_sample_rows, g2, mod, mod, mod, mod, mod, mod, wr_cat, wr_hi, b_r)


def _row_gather_copy(src_hbm, idx, dst, sem):
    return pltpu.make_async_copy(src_hbm.at[pl.ds(idx, 1)], dst, sem)


def _moe_kernel(te_ref, tf_ref, nu_ref, pad_ref, dest_ref, h2_hbm, wg_ref, wu_ref, wd_ref, o_ref,
                src_ref, xbuf, sem, wg_s, wu_s, wd_s):
    del te_ref
    t = pl.program_id(0)
    n_used = nu_ref[0]

    def issue(tile, slot):
        base = tile * TE

        def body(r, carry):
            _row_gather_copy(h2_hbm, src_ref[base + r], xbuf.at[slot, pl.ds(r, 1)], sem.at[slot]).start()
            return carry

        lax.fori_loop(0, TE, body, 0, unroll=8)

    @pl.when(t == 0)
    def _():
        def clear(p, carry):
            src_ref[p] = 0
            return carry

        for e in range(N_EXPERTS):
            lax.fori_loop(pad_ref[e], pad_ref[N_EXPERTS + e], clear, 0)

        def invert(n, carry):
            for k in range(TOP_K):
                src_ref[dest_ref[TOP_K * n + k]] = n
            return carry

        lax.fori_loop(0, NT, invert, 0, unroll=8)
        for k in range(MOE_SLOTS - 1):
            issue(jnp.minimum(k, n_used - 1), k)

    @pl.when((t < n_used) & (tf_ref[t] == 1))
    def _():
        wg_s[...] = wg_ref[...].astype(BF16)
        wu_s[...] = wu_ref[...].astype(BF16)
        wd_s[...] = wd_ref[...].astype(BF16)

    def wait_tile(slot):
        pltpu.make_async_copy(h2_hbm.at[pl.ds(0, TE)], xbuf.at[slot], sem.at[slot]).wait()

    @pl.when(t < n_used)
    def _():
        slot = t % MOE_SLOTS
        wait_tile(slot)
        x = xbuf[slot].reshape(TE, D)
        ahead = (t + MOE_SLOTS - 1) % MOE_SLOTS
        base = jnp.minimum(t + MOE_SLOTS - 1, n_used - 1) * TE
        for r in range(TE):
            _row_gather_copy(h2_hbm, src_ref[base + r], xbuf.at[ahead, pl.ds(r, 1)], sem.at[ahead]).start()
        a = _dot(x, wg_s[...])
        b = _dot(x, wu_s[...])
        hid = (a * _sigmoid(a)) * b
        o_ref[...] = _dot(hid.astype(BF16), wd_s[...])

    @pl.when(t == n_used - 1)
    def _():
        for k in range(1, MOE_SLOTS):
            wait_tile((t + k) % MOE_SLOTS)

    @pl.when(t >= n_used)
    def _():
        o_ref[...] = jnp.zeros_like(o_ref)


def _moe(h2, tile_expert, tile_first, n_used, pads, dest, w_gate, w_up, w_down, l):
    grid_spec = pltpu.PrefetchScalarGridSpec(
        num_scalar_prefetch=5,
        grid=(MAX_TILES,),
        in_specs=[
            pl.BlockSpec(memory_space=pl.ANY),
            pl.BlockSpec((None, None, D, D_FF), lambda t, te, *_: (l, te[t], 0, 0)),
            pl.BlockSpec((None, None, D, D_FF), lambda t, te, *_: (l, te[t], 0, 0)),
            pl.BlockSpec((None, None, D_FF, D), lambda t, te, *_: (l, te[t], 0, 0)),
        ],
        out_specs=pl.BlockSpec((TE, D), lambda t, *_: (t, 0)),
        scratch_shapes=[
            pltpu.SMEM((P_ROWS,), jnp.int32),
            pltpu.VMEM((MOE_SLOTS, TE, D // LANES, LANES), BF16),
            pltpu.SemaphoreType.DMA((MOE_SLOTS,)),
            pltpu.VMEM((D, D_FF), BF16),
            pltpu.VMEM((D, D_FF), BF16),
            pltpu.VMEM((D_FF, D), BF16),
        ],
    )
    return pl.pallas_call(
        _moe_kernel,
        out_shape=jax.ShapeDtypeStruct((P_ROWS, D), F32),
        grid_spec=grid_spec,
        compiler_params=_params(("arbitrary",)),
        name="moe_experts",
    )(tile_expert, tile_first, n_used, pads, dest, h2, w_gate, w_up, w_down)


def _dispatch(route_e, counts_lanes):
    counts = counts_lanes[0, N_GROUPS_MOE:N_GROUPS_MOE + N_EXPERTS]
    tiles = (counts + TE - 1) // TE
    eids = jnp.arange(N_EXPERTS, dtype=jnp.int32)
    tile_end = jnp.sum(jnp.where(eids[None, :] <= eids[:, None], tiles[None, :], 0), axis=1)
    tile_start = tile_end - tiles
    hot = route_e[:, :TOP_K, None] == eids[None, None, :]
    row0 = jnp.sum(jnp.where(hot, tile_start[None, None, :] * TE, 0), axis=-1)
    dest = (row0 + route_e[:, TOP_K:2 * TOP_K]).reshape(-1).astype(jnp.int32)
    tids = jnp.arange(MAX_TILES, dtype=jnp.int32)
    tile_expert = jnp.minimum(jnp.sum((tile_end[None, :] <= tids[:, None]).astype(jnp.int32), axis=1),
                              N_EXPERTS - 1)
    first_tile = jnp.sum(jnp.where(tile_expert[:, None] == eids[None, :], tile_start[None, :], 0), axis=1)
    tile_first = (tids == first_tile).astype(jnp.int32)
    n_used = tile_end[-1:].astype(jnp.int32)
    pads = jnp.concatenate([tile_start * TE + counts, tile_end * TE]).astype(jnp.int32)
    return dest, tile_expert, tile_first, n_used, pads


def _combine_kernel(final, tile0, dest_ref, ys_hbm, x_ref, rw_ref, g_ref, gtp, gts, *rest):
    if final:
        y_ref, ybuf, sem = rest
    else:
        scp, scs, shp, shs, x2_ref, h_ref, ybuf, sem = rest
    step = pl.program_id(0)
    nsteps = pl.num_programs(0)
    i = step + tile0

    def issue(tile, slot, r):
        for k in range(TOP_K):
            _row_gather_copy(ys_hbm, dest_ref[tile * (TK * TOP_K) + TOP_K * r + k],
                             ybuf.at[slot, k, pl.ds(r, 1)], sem.at[slot]).start()

    def wait_tile(slot):
        for k in range(TOP_K):
            pltpu.make_async_copy(ys_hbm.at[pl.ds(0, TK)], ybuf.at[slot, k], sem.at[slot]).wait()

    last = tile0 + nsteps - 1

    @pl.when(step == 0)
    def _():
        for k in range(GATHER_SLOTS - 1):
            def body(r, carry, k=k):
                issue(jnp.minimum(i + k, last), k, r)
                return carry

            lax.fori_loop(0, TK, body, 0, unroll=8)

    slot = step % GATHER_SLOTS
    wait_tile(slot)
    ahead = (step + GATHER_SLOTS - 1) % GATHER_SLOTS
    nxt = jnp.minimum(i + GATHER_SLOTS - 1, last)
    for r in range(TK):
        issue(nxt, ahead, r)
    gt = _mod_rows(i, TK, gtp, gts)
    rw = rw_ref[...]
    moe = rw[:, 0:1] * ybuf[slot, 0] + rw[:, 1:2] * ybuf[slot, 1]
    x2 = x_ref[...] + gt * moe
    xn = (x2 * _rms(x2)) * g_ref[...]
    if final:
        y_ref[...] = xn
    else:
        x2_ref[...] = x2
        sc = _mod_rows(i, TK, scp, scs)
        sh = _mod_rows(i, TK, shp, shs)
        h_ref[...] = (xn * (1.0 + sc) + sh).astype(BF16)

    @pl.when(step == nsteps - 1)
    def _():
        for k in range(1, GATHER_SLOTS):
            wait_tile((step + k) % GATHER_SLOTS)


def _combine(dest, ys, x1, route_w, g, mod, l, final, tile0=0, ntiles=NT // TK):
    row_in = lambda s, d: (s + tile0, 0)
    row_out = lambda s, d: (s, 0)
    nmod = lambda q, ll: [
        pl.BlockSpec((None, SUBLANES, D), lambda s, d: (ll, MOD_PROMPT_BLOCK, q)),
        pl.BlockSpec((None, DEC_BATCH, D), lambda s, d: (ll, 0, q)),
    ]
    in_specs = [pl.BlockSpec(memory_space=pl.ANY), pl.BlockSpec((TK, D), row_in),
                pl.BlockSpec((TK, LANES), row_in),
                pl.BlockSpec((1, D), lambda s, d: (0, 0))] + nmod(5, l)
    args = [ys, x1, route_w, g, mod, mod]
    rows = ntiles * TK
    if final:
        out_shape = jax.ShapeDtypeStruct((rows, D), F32)
        out_specs = pl.BlockSpec((TK, D), row_out)
    else:
        in_specs += nmod(1, l + 1) + nmod(0, l + 1)
        args += [mod, mod, mod, mod]
        out_shape = (jax.ShapeDtypeStruct((rows, D), F32), jax.ShapeDtypeStruct((rows, D), BF16))
        out_specs = (pl.BlockSpec((TK, D), row_out), pl.BlockSpec((TK, D), row_out))
    grid_spec = pltpu.PrefetchScalarGridSpec(
        num_scalar_prefetch=1,
        grid=(ntiles,),
        in_specs=in_specs,
        out_specs=out_specs,
        scratch_shapes=[pltpu.VMEM((GATHER_SLOTS, TOP_K, TK, D), F32),
                        pltpu.SemaphoreType.DMA((GATHER_SLOTS,))],
    )
    return pl.pallas_call(
        functools.partial(_combine_kernel, final, tile0),
        out_shape=out_shape,
        grid_spec=grid_spec,
        compiler_params=_params(("arbitrary",)),
        name=("combine_final_%d" % tile0) if final else "combine",
    )(dest, *args)


def kernel(x_prompt, x_sample, c_prompt, c_sample, state_conv, state_pool, w_ada, b_ada, g_norm1, w_in,
           conv_w, pool_w, pool_scale, g_v, w_s, b_s, w_br, w_o, g_norm2, w_rg, b_rg, w_re, b_re,
           w_gate, w_up, w_down, g_final):
    xp_rows = x_prompt.reshape(NP, D)
    xs_rows = x_sample.transpose(1, 0, 2).reshape(NS, D)
    sample_block0 = 0
    c_all = jnp.concatenate([c_sample, c_prompt,
                             jnp.zeros((MOD_ROWS - DEC_BATCH - BATCH, D), F32)], axis=0).astype(BF16)
    mod = _ada(c_all, w_ada, b_ada)

    pool_w_b = pool_w.astype(BF16)
    tril = jnp.tril(jnp.ones((CHUNK, CHUNK), dtype=bool))
    ws_tril = jnp.where(tril[None, None], w_s, 0.0).astype(BF16)
    bs_tile = jnp.broadcast_to(b_s[:, :, :, None], (DEPTH, HEADS, CHUNK, CHUNK))
    small = jnp.tril(jnp.ones((DEC_SEQ, DEC_SEQ), dtype=bool))
    ws_small = jnp.where(small[None, None], w_s[:, :, :DEC_SEQ, :DEC_SEQ], 0.0)
    wvec = jnp.repeat(ws_small.transpose(0, 2, 3, 1).reshape(DEPTH, DEC_SEQ * DEC_SEQ, HEADS), CHUNK, axis=-1)
    bvec = jnp.repeat(b_s[:, :, :DEC_SEQ].transpose(0, 2, 1), CHUNK, axis=-1)
    w_r = jnp.concatenate([w_rg, w_re, jnp.zeros((DEPTH, D, LANES - N_GROUPS_MOE - N_EXPERTS), F32)], axis=-1)
    wr_hi = w_r.astype(BF16)
    wr_lo = (w_r - wr_hi.astype(F32)).astype(BF16)
    wr_cat = jnp.concatenate([wr_hi, wr_lo], axis=-1)
    b_r = jnp.concatenate([b_rg, b_re, jnp.zeros((DEPTH, LANES - N_GROUPS_MOE - N_EXPERTS), F32)],
                          axis=-1).reshape(DEPTH, 1, LANES)
    conv_t = state_conv.transpose(0, 2, 1, 3)
    pool_t = state_pool.transpose(0, 2, 1, 3)
    g1 = g_norm1.reshape(DEPTH, 1, D)
    g2 = g_norm2.reshape(DEPTH, 1, D)
    gv = g_v.reshape(DEPTH, 1, W)
    pscale = pool_scale.reshape(DEPTH, 1, W)

    h = _norm(xp_rows, xs_rows, g1, mod, 0)
    conv_p, pool_p, conv_s, pool_s, v_s = [], [], [], [], []
    y_prompt = y_sample = None
    for l in range(DEPTH):
        ya, cst_p, w_gates_b = _conv_p(h, w_in, conv_w, l)
        yb, pst_p, w_br_b = _pool_p(h, w_in, pool_w_b, pscale, w_br, l)
        yc, w_o_b = _gmlp_p(h, w_in, gv, ws_tril, bs_tile, w_o, l)
        ya_s, yb_s, cst_s, pst_s = _mix_s(h, w_in, conv_w, pool_w_b, pscale, conv_t, pool_t, l)
        yc_s, vn_s = _gmlp_s(h, w_in, gv, wvec, bvec, l)
        merged = _merge(h, (ya, yb, yc), (ya_s, yb_s, yc_s), w_gates_b, w_br_b)
        x1, h2, route_w, route_e, counts = _wo(merged, w_o_b, xp_rows, xs_rows, sample_block0, g2, mod,
                                               wr_cat, wr_hi, b_r, l)
        dest, tile_expert, tile_first, n_used, pads = _dispatch(route_e, counts)
        ys = _moe(h2, tile_expert, tile_first, n_used, pads, dest, w_gate, w_up, w_down, l)
        if l + 1 < DEPTH:
            x, h = _combine(dest, ys, x1, route_w, g1[l + 1], mod, l, final=False)
            xp_rows = xs_rows = x
            sample_block0 = NP // TW
        else:
            gf = g_final.reshape(1, D)
            y_prompt = _combine(dest, ys, x1, route_w, gf, mod, l, final=True, tile0=0, ntiles=NP // TK)
            y_sample = _combine(dest, ys, x1, route_w, gf, mod, l, final=True, tile0=NP // TK,
                                ntiles=NS // TK)
        conv_p.append(cst_p[:, CONV_TAIL - (CONV_K - 1):, :])
        pool_p.append(pst_p[:, POOL_TAIL - POOL_STATE:, :])
        conv_s.append(cst_s.transpose(1, 0, 2))
        pool_s.append(pst_s.transpose(1, 0, 2))
        v_s.append(vn_s.reshape(DEC_SEQ, DEC_BATCH, W).transpose(1, 0, 2))

    y_prompt = y_prompt.reshape(BATCH, SEQ, D)
    y_sample = y_sample.reshape(DEC_SEQ, DEC_BATCH, D).transpose(1, 0, 2)
    return (y_prompt, y_sample, jnp.stack(conv_p), jnp.stack(pool_p), jnp.stack(conv_s),
            jnp.stack(pool_s), jnp.stack(v_s))
```

```python
import functools

import jax
import jax.numpy as jnp
from jax import lax
from jax.experimental import pallas as pl
from jax.experimental.pallas import tpu as pltpu

F32 = jnp.float32
BF16 = jnp.bfloat16

D = 2048
BATCH = 4
SEQ = 2048
DEPTH = 2
DEC_BATCH = 128
DEC_SEQ = 4
W = D // 2
CONV_K = 3
POOL_STATE = 15
N_POOL_GROUPS = 4
POOL_GROUP = W // N_POOL_GROUPS
CHUNK = 128
HEADS = W // CHUNK
N_GROUPS_MOE = 4
EXP_PER_GROUP = 4
N_EXPERTS = 16
TOP_K = 2
D_FF = D // 4
EPS = 1e-6

NP = BATCH * SEQ
NS = DEC_BATCH * DEC_SEQ
NT = NP + NS
TM = 512
N_TILES = NT // TM
CB = 256
MIX_ROWS = 256
POOL_SHIFTS = (1, 2, 4, 8)
TE = 256
MAX_TILES = (NT * TOP_K) // TE + N_EXPERTS
P_ROWS = MAX_TILES * TE
TK = 256
TW = 256
GATHER_SLOTS = 3
MOE_SLOTS = 4
LANES = 128
SUBLANES = 8
CONV_TAIL = SUBLANES
POOL_TAIL = 2 * SUBLANES
MOD_ROWS = 144
MOD_PROMPT_BLOCK = DEC_BATCH // SUBLANES

VMEM_LIMIT_V7X = 56 * 1024 * 1024


def _dot(a, b):
    return jnp.dot(a, b, preferred_element_type=F32)


def _params(sem, vmem=VMEM_LIMIT_V7X):
    return pltpu.CompilerParams(dimension_semantics=sem, vmem_limit_bytes=vmem)


def _sigmoid(x):
    return 1.0 / (1.0 + jnp.exp(-x))


def _rms(x):
    return lax.rsqrt(jnp.mean(x * x, axis=-1, keepdims=True) + EPS)


def _ada_kernel(c_ref, w_ref, b_ref, o_ref):
    o_ref[...] = _dot(c_ref[...], w_ref[...].astype(BF16)) + b_ref[...]


def _ada(c_all, w_ada, b_ada):
    nb = 1024
    return pl.pallas_call(
        _ada_kernel,
        out_shape=jax.ShapeDtypeStruct((DEPTH, MOD_ROWS, 6 * D), F32),
        grid=(DEPTH, 6 * D // nb),
        in_specs=[
            pl.BlockSpec((MOD_ROWS, D), lambda l, j: (0, 0)),
            pl.BlockSpec((None, D, nb), lambda l, j: (l, 0, j)),
            pl.BlockSpec((None, 1, nb), lambda l, j: (l, 0, j)),
        ],
        out_specs=pl.BlockSpec((None, MOD_ROWS, nb), lambda l, j: (l, 0, j)),
        compiler_params=_params(("arbitrary", "arbitrary")),
        name="ada",
    )(c_all, w_ada, b_ada.reshape(DEPTH, 1, 6 * D))


def _mod_specs(l, q):
    return [
        pl.BlockSpec((None, SUBLANES, D), lambda i, *_: (l, MOD_PROMPT_BLOCK, q)),
        pl.BlockSpec((None, DEC_BATCH, D), lambda i, *_: (l, 0, q)),
    ]


def _mod_rows(i, tile_rows, mp_ref, ms_ref):
    tiles_per_seq = SEQ // tile_rows
    is_sample = i >= NP // tile_rows
    b = jnp.minimum(i // tiles_per_seq, BATCH - 1)
    mp = mp_ref[pl.ds(b, 1), :]
    ms = jnp.concatenate([ms_ref[...]] * (tile_rows // DEC_BATCH), axis=0)
    return jnp.where(is_sample, ms, mp)


def _norm_kernel(xp_ref, xs_ref, g_ref, scp, scs, shp, shs, h_ref):
    i = pl.program_id(0)
    x = jnp.where(i >= NP // TM, xs_ref[...], xp_ref[...])
    sc = _mod_rows(i, TM, scp, scs)
    sh = _mod_rows(i, TM, shp, shs)
    h_ref[...] = (((x * _rms(x)) * g_ref[...]) * (1.0 + sc) + sh).astype(BF16)


def _norm(x_prompt_rows, x_sample_rows, g, mod, l):
    n_ptiles = NP // TM
    return pl.pallas_call(
        _norm_kernel,
        out_shape=jax.ShapeDtypeStruct((NT, D), BF16),
        grid=(N_TILES,),
        in_specs=[pl.BlockSpec((TM, D), lambda i: (jnp.minimum(i, n_ptiles - 1), 0)),
                  pl.BlockSpec((TM, D), lambda i: (jnp.maximum(i - n_ptiles, 0), 0)),
                  pl.BlockSpec((None, 1, D), lambda i: (l, 0, 0))]
                 + _mod_specs(l, 1) + _mod_specs(l, 0),
        out_specs=pl.BlockSpec((TM, D), lambda i: (i, 0)),
        compiler_params=_params(("arbitrary",)),
        name="norm1",
    )(x_prompt_rows, x_sample_rows, g, mod, mod, mod, mod)


def _shift_rows(cur, prev, s, row):
    head = 0.0 if prev is None else pltpu.roll(prev, s, 0)
    return jnp.where(row >= s, pltpu.roll(cur, s, 0), head)


def _conv_p_kernel(h_ref, wb_ref, wc_ref, wh_ref, cw_ref, wcast_ref, y_ref, st_ref, wcast_out):
    wcast_out[...] = wcast_ref[...].astype(BF16)
    wb = wb_ref[...].astype(BF16)
    wc = wc_ref[...].astype(BF16)
    wh = wh_ref[...].astype(BF16)
    cw = cw_ref[...]
    row = lax.broadcasted_iota(jnp.int32, (MIX_ROWS, 1), 0)
    prev = None
    for r0 in range(0, SEQ, MIX_ROWS):
        h = h_ref[r0:r0 + MIX_ROWS, :]
        z = _dot(h, wc) * _dot(h, wh)
        y = cw[0:1] * _shift_rows(z, prev, 2, row) + cw[1:2] * _shift_rows(z, prev, 1, row) + cw[2:3] * z
        y_ref[r0:r0 + MIX_ROWS, :] = (_dot(h, wb) * y).astype(BF16)
        prev = z
    st_ref[...] = prev[MIX_ROWS - CONV_TAIL:, :]


def _conv_p(h, w_in, conv_w, l):
    nc = W // CB
    steps = BATCH * nc
    gcols = 3 * D // steps
    g0 = 6 * W // gcols
    return pl.pallas_call(
        _conv_p_kernel,
        out_shape=(jax.ShapeDtypeStruct((NP, W), BF16),
                   jax.ShapeDtypeStruct((BATCH, CONV_TAIL, W), F32),
                   jax.ShapeDtypeStruct((D, 3 * D), BF16)),
        grid=(BATCH, nc),
        in_specs=[
            pl.BlockSpec((SEQ, D), lambda b, c: (b, 0)),
            pl.BlockSpec((None, D, CB), lambda b, c: (l, 0, c)),
            pl.BlockSpec((None, D, CB), lambda b, c: (l, 0, nc + c)),
            pl.BlockSpec((None, D, CB), lambda b, c: (l, 0, 2 * nc + c)),
            pl.BlockSpec((None, CONV_K, CB), lambda b, c: (l, 0, c)),
            pl.BlockSpec((None, D, gcols), lambda b, c: (l, 0, g0 + b * nc + c)),
        ],
        out_specs=(pl.BlockSpec((SEQ, CB), lambda b, c: (b, c)),
                   pl.BlockSpec((None, CONV_TAIL, CB), lambda b, c: (b, 0, c)),
                   pl.BlockSpec((D, gcols), lambda b, c: (0, b * nc + c))),
        compiler_params=_params(("arbitrary", "arbitrary")),
        name="conv_prompt",
    )(h, w_in, w_in, w_in, conv_w, w_in)


def _pool_p_kernel(h_ref, wp_ref, pw_ref, ps_ref, wcast_ref, y_ref, st_ref, wcast_out):
    wcast_out[...] = wcast_ref[...].astype(BF16)
    g = pl.program_id(1)
    wp = wp_ref[...].astype(BF16)
    ps = [_dot(h_ref[r0:r0 + MIX_ROWS, :], wp) for r0 in range(0, SEQ, MIX_ROWS)]
    row = lax.broadcasted_iota(jnp.int32, (MIX_ROWS, 1), 0)
    window = jnp.left_shift(2, g)
    pw = pw_ref[...]
    scale = ps_ref[...]
    prev = [None] * len(POOL_SHIFTS)
    for r0, p in zip(range(0, SEQ, MIX_ROWS), ps):
        sums = [p]
        for level, s in enumerate(POOL_SHIFTS):
            sums.append(sums[-1] + _shift_rows(sums[-1], prev[level], s, row))
        prev = sums[:-1]
        total = jnp.where(g >= 3, sums[4], jnp.where(g >= 2, sums[3], jnp.where(g >= 1, sums[2], sums[1])))
        cnt = jnp.minimum(row + (r0 + 1), window).astype(F32)
        d = total / cnt - p
        y_ref[r0:r0 + MIX_ROWS, :] = (_dot(d.astype(BF16), pw) * scale).astype(BF16)
    st_ref[...] = ps[-1][MIX_ROWS - POOL_TAIL:, :]


def _pool_p(h, w_in, pool_w, pool_scale, w_br, l):
    col0 = 3 * W // CB
    bcols = D // (BATCH * N_POOL_GROUPS)
    return pl.pallas_call(
        _pool_p_kernel,
        out_shape=(jax.ShapeDtypeStruct((NP, W), BF16),
                   jax.ShapeDtypeStruct((BATCH, POOL_TAIL, W), F32),
                   jax.ShapeDtypeStruct((3, W, D), BF16)),
        grid=(BATCH, N_POOL_GROUPS),
        in_specs=[
            pl.BlockSpec((SEQ, D), lambda b, g: (b, 0)),
            pl.BlockSpec((None, D, CB), lambda b, g: (l, 0, col0 + g)),
            pl.BlockSpec((None, None, POOL_GROUP, POOL_GROUP), lambda b, g: (l, g, 0, 0)),
            pl.BlockSpec((None, 1, CB), lambda b, g: (l, 0, g)),
            pl.BlockSpec((None, 3, W, bcols), lambda b, g: (l, 0, 0, b * N_POOL_GROUPS + g)),
        ],
        out_specs=(pl.BlockSpec((SEQ, CB), lambda b, g: (b, g)),
                   pl.BlockSpec((None, POOL_TAIL, CB), lambda b, g: (b, 0, g)),
                   pl.BlockSpec((3, W, bcols), lambda b, g: (0, 0, b * N_POOL_GROUPS + g))),
        compiler_params=_params(("arbitrary", "arbitrary")),
        name="pool_prompt",
    )(h, w_in, pool_w, pool_scale, w_br)


def _gmlp_p_kernel(h_ref, wv_ref, wu_ref, gv_ref, ws_ref, bs_ref, wcast_ref, y_ref, wcast_out, vn_s):
    wcast_out[...] = wcast_ref[...].astype(BF16)
    j = pl.program_id(1)

    @pl.when(j == 0)
    def _():
        wv = wv_ref[...].astype(BF16)
        for r0 in range(0, SEQ, TM):
            v = _dot(h_ref[r0:r0 + TM, :], wv)
            vn = (v * _rms(v)) * gv_ref[...]
            for hd in range(HEADS):
                vn_s[hd, r0:r0 + TM, :] = vn[:, hd * CHUNK:(hd + 1) * CHUNK].astype(BF16)

    u = _dot(h_ref[...], wu_ref[...].astype(BF16))
    heads_per_block = CB // CHUNK
    for hh in range(heads_per_block):
        head = heads_per_block * j + hh
        wsh = ws_ref[head]
        bsh = bs_ref[head]
        for n in range(SEQ // CHUNK):
            rows = slice(n * CHUNK, (n + 1) * CHUNK)
            cols = slice(hh * CHUNK, (hh + 1) * CHUNK)
            s = _dot(wsh, vn_s[head, pl.ds(n * CHUNK, CHUNK), :]) + bsh
            y_ref[rows, cols] = (u[rows, cols] * s).astype(BF16)


def _gmlp_p(h, w_in, g_v, ws_tril, bs_tile, w_o, l):
    ucol0 = 4 * W // CB
    nj = W // CB
    ocols = D // (BATCH * nj)
    return pl.pallas_call(
        _gmlp_p_kernel,
        out_shape=(jax.ShapeDtypeStruct((NP, W), BF16), jax.ShapeDtypeStruct((D, D), BF16)),
        grid=(BATCH, nj),
        in_specs=[
            pl.BlockSpec((SEQ, D), lambda b, j: (b, 0)),
            pl.BlockSpec((None, D, W), lambda b, j: (l, 0, 5), pipeline_mode=pl.Buffered(1)),
            pl.BlockSpec((None, D, CB), lambda b, j: (l, 0, ucol0 + j)),
            pl.BlockSpec((None, 1, W), lambda b, j: (l, 0, 0)),
            pl.BlockSpec((None, HEADS, CHUNK, CHUNK), lambda b, j: (l, 0, 0, 0)),
            pl.BlockSpec((None, HEADS, CHUNK, CHUNK), lambda b, j: (l, 0, 0, 0)),
            pl.BlockSpec((None, D, ocols), lambda b, j: (l, 0, b * nj + j)),
        ],
        out_specs=(pl.BlockSpec((SEQ, CB), lambda b, j: (b, j)),
                   pl.BlockSpec((D, ocols), lambda b, j: (0, b * nj + j))),
        scratch_shapes=[pltpu.VMEM((HEADS, SEQ, CHUNK), BF16)],
        compiler_params=_params(("arbitrary", "arbitrary")),
        name="gmlp_prompt",
    )(h, w_in, w_in, g_v, ws_tril, bs_tile, w_o)


def _mix_s_kernel(h_ref, wb_ref, wc_ref, wh_ref, wp_ref, cw_ref, pc_ref, pp_ref,
                  pw_ref, ps_ref, ya_ref, yb_ref, cst_ref, pst_ref):
    g = pl.program_id(0)
    h = h_ref[...]
    nb = DEC_BATCH
    z = _dot(h, wc_ref[...].astype(BF16)) * _dot(h, wh_ref[...].astype(BF16))
    bg = _dot(h, wb_ref[...].astype(BF16))
    zs = [pc_ref[0], pc_ref[1]] + [z[t * nb:(t + 1) * nb] for t in range(DEC_SEQ)]
    cw = cw_ref[...]
    for t in range(DEC_SEQ):
        y = cw[0:1] * zs[t] + cw[1:2] * zs[t + 1] + cw[2:3] * zs[t + 2]
        ya_ref[t * nb:(t + 1) * nb, :] = (bg[t * nb:(t + 1) * nb] * y).astype(BF16)
    cst_ref[0] = zs[DEC_SEQ]
    cst_ref[1] = zs[DEC_SEQ + 1]
    p = _dot(h, wp_ref[...].astype(BF16))
    pp = [pp_ref[k] for k in range(POOL_STATE)] + [p[t * nb:(t + 1) * nb] for t in range(DEC_SEQ)]
    window = jnp.left_shift(2, g).astype(F32)
    ds = []
    for t in range(DEC_SEQ):
        e = POOL_STATE + t
        s = pp[e] + pp[e - 1]
        s4 = s + (pp[e - 2] + pp[e - 3])
        s8 = s4 + ((pp[e - 4] + pp[e - 5]) + (pp[e - 6] + pp[e - 7]))
        s16 = s8 + (((pp[e - 8] + pp[e - 9]) + (pp[e - 10] + pp[e - 11]))
                    + ((pp[e - 12] + pp[e - 13]) + (pp[e - 14] + pp[e - 15])))
        s = jnp.where(g >= 1, s4, s)
        s = jnp.where(g >= 2, s8, s)
        s = jnp.where(g >= 3, s16, s)
        ds.append(s / window - pp[e])
    d = jnp.concatenate(ds, axis=0)
    yb_ref[...] = (_dot(d.astype(BF16), pw_ref[...]) * ps_ref[...]).astype(BF16)
    for k in range(POOL_STATE):
        pst_ref[k] = pp[DEC_SEQ + k]


def _mix_s(h, w_in, conv_w, pool_w, pool_scale, conv_t, pool_t, l):
    nc = W // CB
    srow = NP // NS
    return pl.pallas_call(
        _mix_s_kernel,
        out_shape=(jax.ShapeDtypeStruct((NS, W), BF16),
                   jax.ShapeDtypeStruct((NS, W), BF16),
                   jax.ShapeDtypeStruct((CONV_K - 1, DEC_BATCH, W), F32),
                   jax.ShapeDtypeStruct((POOL_STATE, DEC_BATCH, W), F32)),
        grid=(nc,),
        in_specs=[
            pl.BlockSpec((NS, D), lambda c: (srow, 0)),
            pl.BlockSpec((None, D, CB), lambda c: (l, 0, c)),
            pl.BlockSpec((None, D, CB), lambda c: (l, 0, nc + c)),
            pl.BlockSpec((None, D, CB), lambda c: (l, 0, 2 * nc + c)),
            pl.BlockSpec((None, D, CB), lambda c: (l, 0, 3 * nc + c)),
            pl.BlockSpec((None, CONV_K, CB), lambda c: (l, 0, c)),
            pl.BlockSpec((None, CONV_K - 1, DEC_BATCH, CB), lambda c: (l, 0, 0, c)),
            pl.BlockSpec((None, POOL_STATE, DEC_BATCH, CB), lambda c: (l, 0, 0, c)),
            pl.BlockSpec((None, None, POOL_GROUP, POOL_GROUP), lambda c: (l, c, 0, 0)),
            pl.BlockSpec((None, 1, CB), lambda c: (l, 0, c)),
        ],
        out_specs=(pl.BlockSpec((NS, CB), lambda c: (0, c)),
                   pl.BlockSpec((NS, CB), lambda c: (0, c)),
                   pl.BlockSpec((CONV_K - 1, DEC_BATCH, CB), lambda c: (0, 0, c)),
                   pl.BlockSpec((POOL_STATE, DEC_BATCH, CB), lambda c: (0, 0, c))),
        compiler_params=_params(("arbitrary",)),
        name="mix_sample",
    )(h, w_in, w_in, w_in, w_in, conv_w, conv_t, pool_t, pool_w, pool_scale)


def _gmlp_s_kernel(h_ref, wu_ref, wv_ref, gv_ref, wvec_ref, bvec_ref, yc_ref, vn_ref):
    h = h_ref[...]
    nb = DEC_BATCH
    v = _dot(h, wv_ref[...].astype(BF16))
    vn = (v * _rms(v)) * gv_ref[...]
    vn_ref[...] = vn
    u = _dot(h, wu_ref[...].astype(BF16))
    for t in range(DEC_SEQ):
        s = bvec_ref[t:t + 1, :]
        for sp in range(t + 1):
            k = t * DEC_SEQ + sp
            s = s + wvec_ref[k:k + 1, :] * vn[sp * nb:(sp + 1) * nb]
        yc_ref[t * nb:(t + 1) * nb, :] = (u[t * nb:(t + 1) * nb] * s).astype(BF16)


def _gmlp_s(h, w_in, g_v, wvec, bvec, l):
    srow = NP // NS
    return pl.pallas_call(
        _gmlp_s_kernel,
        out_shape=(jax.ShapeDtypeStruct((NS, W), BF16),
                   jax.ShapeDtypeStruct((NS, W), F32)),
        grid=(1,),
        in_specs=[
            pl.BlockSpec((NS, D), lambda i: (srow, 0)),
            pl.BlockSpec((None, D, W), lambda i: (l, 0, 4)),
            pl.BlockSpec((None, D, W), lambda i: (l, 0, 5)),
            pl.BlockSpec((None, 1, W), lambda i: (l, 0, 0)),
            pl.BlockSpec((None, DEC_SEQ * DEC_SEQ, W), lambda i: (l, 0, 0)),
            pl.BlockSpec((None, DEC_SEQ, W), lambda i: (l, 0, 0)),
        ],
        out_specs=(pl.BlockSpec((NS, W), lambda i: (0, 0)),
                   pl.BlockSpec((NS, W), lambda i: (0, 0))),
        compiler_params=_params(("arbitrary",)),
        name="gmlp_sample",
    )(h, w_in, w_in, g_v, wvec, bvec)


def _merge_kernel(h_ref, yap, ybp, ycp, yas, ybs, ycs, wg0, wg1, wg2, wbr_ref, o_ref):
    is_sample = pl.program_id(0) >= NP // TM
    h = h_ref[...]
    acc = None
    for n, (yp, ys, wg) in enumerate(((yap, yas, wg0), (ybp, ybs, wg1), (ycp, ycs, wg2))):
        y = jnp.where(is_sample, ys[...], yp[...])
        term = _sigmoid(_dot(h, wg[...])) * _dot(y, wbr_ref[n])
        acc = term if acc is None else acc + term
    o_ref[...] = acc.astype(BF16)


def _merge(h, y_prompt, y_sample, w_gates, w_br):
    db = 512
    g0 = 0
    gstep = D // db
    return pl.pallas_call(
        _merge_kernel,
        out_shape=jax.ShapeDtypeStruct((NT, D), BF16),
        grid=(N_TILES, D // db),
        in_specs=[
            pl.BlockSpec((TM, D), lambda i, d: (i, 0)),
            pl.BlockSpec((TM, W), lambda i, d: (jnp.minimum(i, NP // TM - 1), 0)),
            pl.BlockSpec((TM, W), lambda i, d: (jnp.minimum(i, NP // TM - 1), 0)),
            pl.BlockSpec((TM, W), lambda i, d: (jnp.minimum(i, NP // TM - 1), 0)),
            pl.BlockSpec((NS, W), lambda i, d: (0, 0)),
            pl.BlockSpec((NS, W), lambda i, d: (0, 0)),
            pl.BlockSpec((NS, W), lambda i, d: (0, 0)),
            pl.BlockSpec((D, db), lambda i, d: (0, g0 + d)),
            pl.BlockSpec((D, db), lambda i, d: (0, g0 + gstep + d)),
            pl.BlockSpec((D, db), lambda i, d: (0, g0 + 2 * gstep + d)),
            pl.BlockSpec((3, W, db), lambda i, d: (0, 0, d)),
        ],
        out_specs=pl.BlockSpec((TM, db), lambda i, d: (i, d)),
        compiler_params=_params(("arbitrary", "arbitrary")),
        name="merge",
    )(h, *y_prompt, *y_sample, w_gates, w_gates, w_gates, w_br)


def _route(logits):
    lane = lax.broadcasted_iota(jnp.int32, logits.shape, 1).astype(F32)
    neg = -jnp.inf
    big = float(LANES)
    is_grp = lane < N_GROUPS_MOE
    gl = jnp.where(is_grp, logits, neg)
    gmax = jnp.max(gl, axis=-1, keepdims=True)
    gsel = jnp.min(jnp.where(gl == gmax, lane, big), axis=-1, keepdims=True)
    gp = 1.0 / jnp.sum(jnp.where(is_grp, jnp.exp(logits - gmax), 0.0), axis=-1, keepdims=True)
    lo = N_GROUPS_MOE + gsel * EXP_PER_GROUP
    in_grp = (lane >= lo) & (lane < lo + EXP_PER_GROUP)
    el = jnp.where(in_grp, logits, neg)
    m1 = jnp.max(el, axis=-1, keepdims=True)
    i1 = jnp.min(jnp.where(el == m1, lane, big), axis=-1, keepdims=True)
    el2 = jnp.where(lane == i1, neg, el)
    m2 = jnp.max(el2, axis=-1, keepdims=True)
    i2 = jnp.min(jnp.where(el2 == m2, lane, big), axis=-1, keepdims=True)
    e = jnp.exp(m2 - m1)
    w1 = gp / (1.0 + e)
    w2 = gp * (e / (1.0 + e))
    rw = jnp.where(lane == 0.0, w1, jnp.where(lane == 1.0, w2, 0.0))
    return rw, lane, i1, i2


def _wo_kernel(m_ref, wo_ref, xp_ref, xs_ref, g_ref, gtp, gts, scp, scs, shp, shs, wrc_ref, wrh_ref, br_ref,
               x1_ref, h2_ref, rw_ref, re_ref, cnt_ref, carry):
    i = pl.program_id(0)
    x = jnp.where(i >= NP // TW, xs_ref[...], xp_ref[...])
    gt = _mod_rows(i, TW, gtp, gts)
    x1 = x + gt * _dot(m_ref[...], wo_ref[...])
    x1_ref[...] = x1
    sc = _mod_rows(i, TW, scp, scs)
    sh = _mod_rows(i, TW, shp, shs)
    h2 = ((x1 * _rms(x1)) * g_ref[...]) * (1.0 + sc) + sh
    h2_ref[...] = h2.astype(BF16).reshape(TW, D // LANES, LANES)
    hi = h2.astype(BF16)
    lo = (h2 - hi.astype(F32)).astype(BF16)
    both = _dot(hi, wrc_ref[...])
    logits = (both[:, :LANES] + (both[:, LANES:] + _dot(lo, wrh_ref[...]))) + br_ref[...]
    rw, lane, i1, i2 = _route(logits)
    rw_ref[...] = rw

    @pl.when(i == 0)
    def _():
        carry[...] = jnp.zeros_like(carry)

    o1 = (lane == i1).astype(F32)
    o2 = (lane == i2).astype(F32)
    both_hot = o1 + o2
    r = lax.broadcasted_iota(jnp.int32, (TW, TW), 0)
    c = lax.broadcasted_iota(jnp.int32, (TW, TW), 1)
    before = (r > c).astype(BF16)
    seen = _dot(before, both_hot.astype(BF16)) + carry[...]
    rank1 = jnp.sum(seen * o1, axis=-1, keepdims=True)
    rank2 = jnp.sum(seen * o2, axis=-1, keepdims=True)
    total = carry[...] + jnp.sum(both_hot, axis=0, keepdims=True)
    carry[...] = total
    cnt_ref[...] = jnp.broadcast_to(total, cnt_ref.shape).astype(jnp.int32)
    re = jnp.where(lane == 0.0, i1 - N_GROUPS_MOE,
                   jnp.where(lane == 1.0, i2 - N_GROUPS_MOE,
                             jnp.where(lane == 2.0, rank1, jnp.where(lane == 3.0, rank2, 0.0))))
    re_ref[...] = re.astype(jnp.int32)


def _wo(merged, w_o, x_prompt_rows, x_sample_rows, sample_block0, g2, mod, wr_cat, wr_hi, b_r, l):
    row = lambda i: (i, 0)
    n_ptiles = NP // TW
    return pl.pallas_call(
        _wo_kernel,
        out_shape=(jax.ShapeDtypeStruct((NT, D), F32),
                   jax.ShapeDtypeStruct((NT, D // LANES, LANES), BF16),
                   jax.ShapeDtypeStruct((NT, LANES), F32),
                   jax.ShapeDtypeStruct((NT, LANES), jnp.int32),
                   jax.ShapeDtypeStruct((SUBLANES, LANES), jnp.int32)),
        grid=(NT // TW,),
        in_specs=[
            pl.BlockSpec((TW, D), row),
            pl.BlockSpec((D, D), lambda i: (0, 0)),
            pl.BlockSpec((TW, D), lambda i: (jnp.minimum(i, n_ptiles - 1), 0)),
            pl.BlockSpec((TW, D), lambda i: (sample_block0 + jnp.maximum(i - n_ptiles, 0), 0)),
            pl.BlockSpec((None, 1, D), lambda i: (l, 0, 0)),
        ] + _mod_specs(l, 2) + _mod_specs(l, 4) + _mod_specs(l, 3) + [
            pl.BlockSpec((None, D, 2 * LANES), lambda i: (l, 0, 0)),
            pl.BlockSpec((None, D, LANES), lambda i: (l, 0, 0)),
            pl.BlockSpec((None, 1, LANES), lambda i: (l, 0, 0)),
        ],
        out_specs=(pl.BlockSpec((TW, D), row), pl.BlockSpec((TW, D // LANES, LANES), lambda i: (i, 0, 0)),
                   pl.BlockSpec((TW, LANES), row), pl.BlockSpec((TW, LANES), row),
                   pl.BlockSpec((SUBLANES, LANES), lambda i: (0, 0))),
        scratch_shapes=[pltpu.VMEM((1, LANES), F32)],
        compiler_params=_params(("arbitrary",)),
        name="wo_router",
    )(merged, w_o, x_prompt_rows, x_sample_rows, g2, mod, mod, mod, mod, mod, mod, wr_cat, wr_hi, b_r)


def _row_gather_copy(src_hbm, idx, dst, sem):
    return pltpu.make_async_copy(src_hbm.at[pl.ds(idx, 1)], dst, sem)


def _moe_kernel(l, te_ref, tf_ref, nu_ref, pad_ref, dest_ref, h2_hbm, wg_hbm, wu_hbm, wd_hbm, o_ref,
                src_ref, xbuf, sem, wg_s, wu_s, wd_s, stg_g, stg_u, stg_d, wsem):
    t = pl.program_id(0)
    n_used = nu_ref[0]

    def weight_copies(e):
        return (pltpu.make_async_copy(wg_hbm.at[l, e], stg_g, wsem.at[0]),
                pltpu.make_async_copy(wu_hbm.at[l, e], stg_u, wsem.at[1]),
                pltpu.make_async_copy(wd_hbm.at[l, e], stg_d, wsem.at[2]))

    def issue(tile, slot):
        base = tile * TE

        def body(r, carry):
            _row_gather_copy(h2_hbm, src_ref[base + r], xbuf.at[slot, pl.ds(r, 1)], sem.at[slot]).start()
            return carry

        lax.fori_loop(0, TE, body, 0, unroll=8)

    @pl.when(t == 0)
    def _():
        for c in weight_copies(te_ref[0]):
            c.start()

        def clear(p, carry):
            src_ref[p] = 0
            return carry

        for e in range(N_EXPERTS):
            lax.fori_loop(pad_ref[e], pad_ref[N_EXPERTS + e], clear, 0)

        def invert(n, carry):
            for k in range(TOP_K):
                src_ref[dest_ref[TOP_K * n + k]] = n
            return carry

        lax.fori_loop(0, NT, invert, 0, unroll=8)
        for k in range(MOE_SLOTS - 1):
            issue(jnp.minimum(k, n_used - 1), k)

    @pl.when((t < n_used) & (tf_ref[t] > 0))
    def _():
        for c in weight_copies(te_ref[t]):
            c.wait()
        wg_s[...] = stg_g[...].astype(BF16)
        wu_s[...] = stg_u[...].astype(BF16)
        wd_s[...] = stg_d[...].astype(BF16)
        nxt = tf_ref[t] - 1

        @pl.when(nxt < N_EXPERTS)
        def _():
            for c in weight_copies(nxt):
                c.start()

    def wait_tile(slot):
        pltpu.make_async_copy(h2_hbm.at[pl.ds(0, TE)], xbuf.at[slot], sem.at[slot]).wait()

    @pl.when(t < n_used)
    def _():
        slot = t % MOE_SLOTS
        wait_tile(slot)
        x = xbuf[slot].reshape(TE, D)
        ahead = (t + MOE_SLOTS - 1) % MOE_SLOTS
        base = jnp.minimum(t + MOE_SLOTS - 1, n_used - 1) * TE
        for r in range(TE):
            _row_gather_copy(h2_hbm, src_ref[base + r], xbuf.at[ahead, pl.ds(r, 1)], sem.at[ahead]).start()
        a = _dot(x, wg_s[...])
        b = _dot(x, wu_s[...])
        hid = (a * _sigmoid(a)) * b
        o_ref[...] = _dot(hid.astype(BF16), wd_s[...])

    @pl.when(t == n_used - 1)
    def _():
        for k in range(1, MOE_SLOTS):
            wait_tile((t + k) % MOE_SLOTS)

    @pl.when(t >= n_used)
    def _():
        o_ref[...] = jnp.zeros_like(o_ref)


def _moe(h2, tile_expert, tile_first, n_used, pads, dest, w_gate, w_up, w_down, l):
    any_spec = pl.BlockSpec(memory_space=pl.ANY)
    grid_spec = pltpu.PrefetchScalarGridSpec(
        num_scalar_prefetch=5,
        grid=(MAX_TILES,),
        in_specs=[any_spec, any_spec, any_spec, any_spec],
        out_specs=pl.BlockSpec((TE, D), lambda t, *_: (t, 0)),
        scratch_shapes=[
            pltpu.SMEM((P_ROWS,), jnp.int32),
            pltpu.VMEM((MOE_SLOTS, TE, D // LANES, LANES), BF16),
            pltpu.SemaphoreType.DMA((MOE_SLOTS,)),
            pltpu.VMEM((D, D_FF), BF16),
            pltpu.VMEM((D, D_FF), BF16),
            pltpu.VMEM((D_FF, D), BF16),
            pltpu.VMEM((D, D_FF), F32),
            pltpu.VMEM((D, D_FF), F32),
            pltpu.VMEM((D_FF, D), F32),
            pltpu.SemaphoreType.DMA((3,)),
        ],
    )
    return pl.pallas_call(
        functools.partial(_moe_kernel, l),
        out_shape=jax.ShapeDtypeStruct((P_ROWS, D), F32),
        grid_spec=grid_spec,
        compiler_params=_params(("arbitrary",)),
        name="moe_experts",
    )(tile_expert, tile_first, n_used, pads, dest, h2, w_gate, w_up, w_down)


def _dispatch(route_e, counts_lanes):
    counts = counts_lanes[0, N_GROUPS_MOE:N_GROUPS_MOE + N_EXPERTS]
    tiles = (counts + TE - 1) // TE
    eids = jnp.arange(N_EXPERTS, dtype=jnp.int32)
    tile_end = jnp.sum(jnp.where(eids[None, :] <= eids[:, None], tiles[None, :], 0), axis=1)
    tile_start = tile_end - tiles
    hot = route_e[:, :TOP_K, None] == eids[None, None, :]
    row0 = jnp.sum(jnp.where(hot, tile_start[None, None, :] * TE, 0), axis=-1)
    dest = (row0 + route_e[:, TOP_K:2 * TOP_K]).reshape(-1).astype(jnp.int32)
    tids = jnp.arange(MAX_TILES, dtype=jnp.int32)
    tile_expert = jnp.minimum(jnp.sum((tile_end[None, :] <= tids[:, None]).astype(jnp.int32), axis=1),
                              N_EXPERTS - 1)
    first_tile = jnp.sum(jnp.where(tile_expert[:, None] == eids[None, :], tile_start[None, :], 0), axis=1)
    n_used = tile_end[-1:].astype(jnp.int32)
    after = jnp.sum(jnp.where(tile_expert[:, None] == eids[None, :], tile_end[None, :], 0), axis=1)
    next_e = jnp.sum(jnp.where(tids[None, :] == jnp.minimum(after, MAX_TILES - 1)[:, None],
                               tile_expert[None, :], 0), axis=1)
    code = 1 + jnp.where(after < n_used[0], next_e, N_EXPERTS)
    tile_first = jnp.where(tids == first_tile, code, 0).astype(jnp.int32)
    pads = jnp.concatenate([tile_start * TE + counts, tile_end * TE]).astype(jnp.int32)
    return dest, tile_expert, tile_first, n_used, pads


def _combine_kernel(final, tile0, dest_ref, ys_hbm, x_ref, rw_ref, g_ref, gtp, gts, *rest):
    if final:
        y_ref, ybuf, sem = rest
    else:
        scp, scs, shp, shs, x2_ref, h_ref, ybuf, sem = rest
    step = pl.program_id(0)
    nsteps = pl.num_programs(0)
    i = step + tile0

    def issue(tile, slot, r):
        for k in range(TOP_K):
            _row_gather_copy(ys_hbm, dest_ref[tile * (TK * TOP_K) + TOP_K * r + k],
                             ybuf.at[slot, k, pl.ds(r, 1)], sem.at[slot]).start()

    def wait_tile(slot):
        for k in range(TOP_K):
            pltpu.make_async_copy(ys_hbm.at[pl.ds(0, TK)], ybuf.at[slot, k], sem.at[slot]).wait()

    last = tile0 + nsteps - 1

    @pl.when(step == 0)
    def _():
        for k in range(GATHER_SLOTS - 1):
            def body(r, carry, k=k):
                issue(jnp.minimum(i + k, last), k, r)
                return carry

            lax.fori_loop(0, TK, body, 0, unroll=8)

    slot = step % GATHER_SLOTS
    wait_tile(slot)
    ahead = (step + GATHER_SLOTS - 1) % GATHER_SLOTS
    nxt = jnp.minimum(i + GATHER_SLOTS - 1, last)
    for r in range(TK):
        issue(nxt, ahead, r)
    gt = _mod_rows(i, TK, gtp, gts)
    rw = rw_ref[...]
    moe = rw[:, 0:1] * ybuf[slot, 0] + rw[:, 1:2] * ybuf[slot, 1]
    x2 = x_ref[...] + gt * moe
    xn = (x2 * _rms(x2)) * g_ref[...]
    if final:
        y_ref[...] = xn
    else:
        x2_ref[...] = x2
        sc = _mod_rows(i, TK, scp, scs)
        sh = _mod_rows(i, TK, shp, shs)
        h_ref[...] = (xn * (1.0 + sc) + sh).astype(BF16)

    @pl.when(step == nsteps - 1)
    def _():
        for k in range(1, GATHER_SLOTS):
            wait_tile((step + k) % GATHER_SLOTS)


def _combine(dest, ys, x1, route_w, g, mod, l, final, tile0=0, ntiles=NT // TK):
    row_in = lambda s, d: (s + tile0, 0)
    row_out = lambda s, d: (s, 0)
    nmod = lambda q, ll: [
        pl.BlockSpec((None, SUBLANES, D), lambda s, d: (ll, MOD_PROMPT_BLOCK, q)),
        pl.BlockSpec((None, DEC_BATCH, D), lambda s, d: (ll, 0, q)),
    ]
    in_specs = [pl.BlockSpec(memory_space=pl.ANY), pl.BlockSpec((TK, D), row_in),
                pl.BlockSpec((TK, LANES), row_in),
                pl.BlockSpec((1, D), lambda s, d: (0, 0))] + nmod(5, l)
    args = [ys, x1, route_w, g, mod, mod]
    rows = ntiles * TK
    if final:
        out_shape = jax.ShapeDtypeStruct((rows, D), F32)
        out_specs = pl.BlockSpec((TK, D), row_out)
    else:
        in_specs += nmod(1, l + 1) + nmod(0, l + 1)
        args += [mod, mod, mod, mod]
        out_shape = (jax.ShapeDtypeStruct((rows, D), F32), jax.ShapeDtypeStruct((rows, D), BF16))
        out_specs = (pl.BlockSpec((TK, D), row_out), pl.BlockSpec((TK, D), row_out))
    grid_spec = pltpu.PrefetchScalarGridSpec(
        num_scalar_prefetch=1,
        grid=(ntiles,),
        in_specs=in_specs,
        out_specs=out_specs,
        scratch_shapes=[pltpu.VMEM((GATHER_SLOTS, TOP_K, TK, D), F32),
                        pltpu.SemaphoreType.DMA((GATHER_SLOTS,))],
    )
    return pl.pallas_call(
        functools.partial(_combine_kernel, final, tile0),
        out_shape=out_shape,
        grid_spec=grid_spec,
        compiler_params=_params(("arbitrary",)),
        name=("combine_final_%d" % tile0) if final else "combine",
    )(dest, *args)


def kernel(x_prompt, x_sample, c_prompt, c_sample, state_conv, state_pool, w_ada, b_ada, g_norm1, w_in,
           conv_w, pool_w, pool_scale, g_v, w_s, b_s, w_br, w_o, g_norm2, w_rg, b_rg, w_re, b_re,
           w_gate, w_up, w_down, g_final):
    xp_rows = x_prompt.reshape(NP, D)
    xs_rows = x_sample.transpose(1, 0, 2).reshape(NS, D)
    sample_block0 = 0
    c_all = jnp.concatenate([c_sample, c_prompt,
                             jnp.zeros((MOD_ROWS - DEC_BATCH - BATCH, D), F32)], axis=0).astype(BF16)
    mod = _ada(c_all, w_ada, b_ada)

    pool_w_b = pool_w.astype(BF16)
    tril = jnp.tril(jnp.ones((CHUNK, CHUNK), dtype=bool))
    ws_tril = jnp.where(tril[None, None], w_s, 0.0).astype(BF16)
    bs_tile = jnp.broadcast_to(b_s[:, :, :, None], (DEPTH, HEADS, CHUNK, CHUNK))
    small = jnp.tril(jnp.ones((DEC_SEQ, DEC_SEQ), dtype=bool))
    ws_small = jnp.where(small[None, None], w_s[:, :, :DEC_SEQ, :DEC_SEQ], 0.0)
    wvec = jnp.repeat(ws_small.transpose(0, 2, 3, 1).reshape(DEPTH, DEC_SEQ * DEC_SEQ, HEADS), CHUNK, axis=-1)
    bvec = jnp.repeat(b_s[:, :, :DEC_SEQ].transpose(0, 2, 1), CHUNK, axis=-1)
    w_r = jnp.concatenate([w_rg, w_re, jnp.zeros((DEPTH, D, LANES - N_GROUPS_MOE - N_EXPERTS), F32)], axis=-1)
    wr_hi = w_r.astype(BF16)
    wr_lo = (w_r - wr_hi.astype(F32)).astype(BF16)
    wr_cat = jnp.concatenate([wr_hi, wr_lo], axis=-1)
    b_r = jnp.concatenate([b_rg, b_re, jnp.zeros((DEPTH, LANES - N_GROUPS_MOE - N_EXPERTS), F32)],
                          axis=-1).reshape(DEPTH, 1, LANES)
    conv_t = state_conv.transpose(0, 2, 1, 3)
    pool_t = state_pool.transpose(0, 2, 1, 3)
    g1 = g_norm1.reshape(DEPTH, 1, D)
    g2 = g_norm2.reshape(DEPTH, 1, D)
    gv = g_v.reshape(DEPTH, 1, W)
    pscale = pool_scale.reshape(DEPTH, 1, W)

    h = _norm(xp_rows, xs_rows, g1, mod, 0)
    conv_p, pool_p, conv_s, pool_s, v_s = [], [], [], [], []
    y_prompt = y_sample = None
    for l in range(DEPTH):
        ya, cst_p, w_gates_b = _conv_p(h, w_in, conv_w, l)
        yb, pst_p, w_br_b = _pool_p(h, w_in, pool_w_b, pscale, w_br, l)
        yc, w_o_b = _gmlp_p(h, w_in, gv, ws_tril, bs_tile, w_o, l)
        ya_s, yb_s, cst_s, pst_s = _mix_s(h, w_in, conv_w, pool_w_b, pscale, conv_t, pool_t, l)
        yc_s, vn_s = _gmlp_s(h, w_in, gv, wvec, bvec, l)
        merged = _merge(h, (ya, yb, yc), (ya_s, yb_s, yc_s), w_gates_b, w_br_b)
        x1, h2, route_w, route_e, counts = _wo(merged, w_o_b, xp_rows, xs_rows, sample_block0, g2, mod,
                                               wr_cat, wr_hi, b_r, l)
        dest, tile_expert, tile_first, n_used, pads = _dispatch(route_e, counts)
        ys = _moe(h2, tile_expert, tile_first, n_used, pads, dest, w_gate, w_up, w_down, l)
        if l + 1 < DEPTH:
            x, h = _combine(dest, ys, x1, route_w, g1[l + 1], mod, l, final=False)
            xp_rows = xs_rows = x
            sample_block0 = NP // TW
        else:
            gf = g_final.reshape(1, D)
            y_prompt = _combine(dest, ys, x1, route_w, gf, mod, l, final=True, tile0=0, ntiles=NP // TK)
            y_sample = _combine(dest, ys, x1, route_w, gf, mod, l, final=True, tile0=NP // TK,
                                ntiles=NS // TK)
        conv_p.append(cst_p[:, CONV_TAIL - (CONV_K - 1):, :])
        pool_p.append(pst_p[:, POOL_TAIL - POOL_STATE:, :])
        conv_s.append(cst_s.transpose(1, 0, 2))
        pool_s.append(pst_s.transpose(1, 0, 2))
        v_s.append(vn_s.reshape(DEC_SEQ, DEC_BATCH, W).transpose(1, 0, 2))

    y_prompt = y_prompt.reshape(BATCH, SEQ, D)
    y_sample = y_sample.reshape(DEC_SEQ, DEC_BATCH, D).transpose(1, 0, 2)
    return (y_prompt, y_sample, jnp.stack(conv_p), jnp.stack(pool_p), jnp.stack(conv_s),
            jnp.stack(pool_s), jnp.stack(v_s))
```

```python
import functools

import jax
import jax.numpy as jnp
from jax import lax
from jax.experimental import pallas as pl
from jax.experimental.pallas import tpu as pltpu

F32 = jnp.float32
BF16 = jnp.bfloat16

D = 2048
BATCH = 4
SEQ = 2048
DEPTH = 2
DEC_BATCH = 128
DEC_SEQ = 4
W = D // 2
CONV_K = 3
POOL_STATE = 15
N_POOL_GROUPS = 4
POOL_GROUP = W // N_POOL_GROUPS
CHUNK = 128
HEADS = W // CHUNK
N_GROUPS_MOE = 4
EXP_PER_GROUP = 4
N_EXPERTS = 16
TOP_K = 2
D_FF = D // 4
EPS = 1e-6

NP = BATCH * SEQ
NS = DEC_BATCH * DEC_SEQ
NT = NP + NS
TM = 512
N_TILES = NT // TM
CB = 256
MIX_ROWS = 256
POOL_SHIFTS = (1, 2, 4, 8)
TE = 256
MAX_TILES = (NT * TOP_K) // TE + N_EXPERTS
P_ROWS = MAX_TILES * TE
assert TOP_K == 2 and P_ROWS < 2 ** 16
TK = 256
TW = 256
GATHER_SLOTS = 3
MOE_SLOTS = 4
LANES = 128
SUBLANES = 8
CONV_TAIL = SUBLANES
POOL_TAIL = 2 * SUBLANES
MOD_ROWS = 144
MOD_PROMPT_BLOCK = DEC_BATCH // SUBLANES

VMEM_LIMIT_V7X = 56 * 1024 * 1024


def _dot(a, b):
    return jnp.dot(a, b, preferred_element_type=F32)


def _params(sem, vmem=VMEM_LIMIT_V7X):
    return pltpu.CompilerParams(dimension_semantics=sem, vmem_limit_bytes=vmem)


def _sigmoid(x):
    return 1.0 / (1.0 + jnp.exp(-x))


def _rms(x):
    return lax.rsqrt(jnp.mean(x * x, axis=-1, keepdims=True) + EPS)


def _ada_kernel(c_ref, w_ref, b_ref, o_ref):
    o_ref[...] = _dot(c_ref[...], w_ref[...].astype(BF16)) + b_ref[...]


def _ada(c_all, w_ada, b_ada):
    nb = 1024
    return pl.pallas_call(
        _ada_kernel,
        out_shape=jax.ShapeDtypeStruct((DEPTH, MOD_ROWS, 6 * D), F32),
        grid=(DEPTH, 6 * D // nb),
        in_specs=[
            pl.BlockSpec((MOD_ROWS, D), lambda l, j: (0, 0)),
            pl.BlockSpec((None, D, nb), lambda l, j: (l, 0, j)),
            pl.BlockSpec((None, 1, nb), lambda l, j: (l, 0, j)),
        ],
        out_specs=pl.BlockSpec((None, MOD_ROWS, nb), lambda l, j: (l, 0, j)),
        compiler_params=_params(("arbitrary", "arbitrary")),
        name="ada",
    )(c_all, w_ada, b_ada.reshape(DEPTH, 1, 6 * D))


def _mod_specs(l, q):
    return [
        pl.BlockSpec((None, SUBLANES, D), lambda i, *_: (l, MOD_PROMPT_BLOCK, q)),
        pl.BlockSpec((None, DEC_BATCH, D), lambda i, *_: (l, 0, q)),
    ]


def _mod_rows(i, tile_rows, mp_ref, ms_ref):
    tiles_per_seq = SEQ // tile_rows
    is_sample = i >= NP // tile_rows
    b = jnp.minimum(i // tiles_per_seq, BATCH - 1)
    mp = mp_ref[pl.ds(b, 1), :]
    ms = jnp.concatenate([ms_ref[...]] * (tile_rows // DEC_BATCH), axis=0)
    return jnp.where(is_sample, ms, mp)


def _norm_kernel(xp_ref, xs_ref, g_ref, scp, scs, shp, shs, h_ref):
    i = pl.program_id(0)
    x = jnp.where(i >= NP // TM, xs_ref[...], xp_ref[...])
    sc = _mod_rows(i, TM, scp, scs)
    sh = _mod_rows(i, TM, shp, shs)
    h_ref[...] = (((x * _rms(x)) * g_ref[...]) * (1.0 + sc) + sh).astype(BF16)


def _norm(x_prompt_rows, x_sample_rows, g, mod, l):
    n_ptiles = NP // TM
    return pl.pallas_call(
        _norm_kernel,
        out_shape=jax.ShapeDtypeStruct((NT, D), BF16),
        grid=(N_TILES,),
        in_specs=[pl.BlockSpec((TM, D), lambda i: (jnp.minimum(i, n_ptiles - 1), 0)),
                  pl.BlockSpec((TM, D), lambda i: (jnp.maximum(i - n_ptiles, 0), 0)),
                  pl.BlockSpec((None, 1, D), lambda i: (l, 0, 0))]
                 + _mod_specs(l, 1) + _mod_specs(l, 0),
        out_specs=pl.BlockSpec((TM, D), lambda i: (i, 0)),
        compiler_params=_params(("arbitrary",)),
        name="norm1",
    )(x_prompt_rows, x_sample_rows, g, mod, mod, mod, mod)


def _shift_rows(cur, prev, s, row):
    head = 0.0 if prev is None else pltpu.roll(prev, s, 0)
    return jnp.where(row >= s, pltpu.roll(cur, s, 0), head)


def _conv_p_kernel(h_ref, wb_ref, wc_ref, wh_ref, cw_ref, wcast_ref, y_ref, st_ref, wcast_out):
    wcast_out[...] = wcast_ref[...].astype(BF16)
    wb = wb_ref[...].astype(BF16)
    wc = wc_ref[...].astype(BF16)
    wh = wh_ref[...].astype(BF16)
    cw = cw_ref[...]
    row = lax.broadcasted_iota(jnp.int32, (MIX_ROWS, 1), 0)
    prev = None
    for r0 in range(0, SEQ, MIX_ROWS):
        h = h_ref[r0:r0 + MIX_ROWS, :]
        z = _dot(h, wc) * _dot(h, wh)
        y = cw[0:1] * _shift_rows(z, prev, 2, row) + cw[1:2] * _shift_rows(z, prev, 1, row) + cw[2:3] * z
        y_ref[r0:r0 + MIX_ROWS, :] = (_dot(h, wb) * y).astype(BF16)
        prev = z
    st_ref[...] = prev[MIX_ROWS - CONV_TAIL:, :]


def _conv_p(h, w_in, conv_w, l):
    nc = W // CB
    steps = BATCH * nc
    gcols = 3 * D // steps
    g0 = 6 * W // gcols
    return pl.pallas_call(
        _conv_p_kernel,
        out_shape=(jax.ShapeDtypeStruct((NP, W), BF16),
                   jax.ShapeDtypeStruct((BATCH, CONV_TAIL, W), F32),
                   jax.ShapeDtypeStruct((D, 3 * D), BF16)),
        grid=(BATCH, nc),
        in_specs=[
            pl.BlockSpec((SEQ, D), lambda b, c: (b, 0)),
            pl.BlockSpec((None, D, CB), lambda b, c: (l, 0, c)),
            pl.BlockSpec((None, D, CB), lambda b, c: (l, 0, nc + c)),
            pl.BlockSpec((None, D, CB), lambda b, c: (l, 0, 2 * nc + c)),
            pl.BlockSpec((None, CONV_K, CB), lambda b, c: (l, 0, c)),
            pl.BlockSpec((None, D, gcols), lambda b, c: (l, 0, g0 + b * nc + c)),
        ],
        out_specs=(pl.BlockSpec((SEQ, CB), lambda b, c: (b, c)),
                   pl.BlockSpec((None, CONV_TAIL, CB), lambda b, c: (b, 0, c)),
                   pl.BlockSpec((D, gcols), lambda b, c: (0, b * nc + c))),
        compiler_params=_params(("arbitrary", "arbitrary")),
        name="conv_prompt",
    )(h, w_in, w_in, w_in, conv_w, w_in)


def _pool_p_kernel(h_ref, wp_ref, pw_ref, ps_ref, wcast_ref, y_ref, st_ref, wcast_out):
    wcast_out[...] = wcast_ref[...].astype(BF16)
    g = pl.program_id(1)
    wp = wp_ref[...].astype(BF16)
    ps = [_dot(h_ref[r0:r0 + MIX_ROWS, :], wp) for r0 in range(0, SEQ, MIX_ROWS)]
    row = lax.broadcasted_iota(jnp.int32, (MIX_ROWS, 1), 0)
    window = jnp.left_shift(2, g)
    pw = pw_ref[...]
    scale = ps_ref[...]
    prev = [None] * len(POOL_SHIFTS)
    for r0, p in zip(range(0, SEQ, MIX_ROWS), ps):
        sums = [p]
        for level, s in enumerate(POOL_SHIFTS):
            sums.append(sums[-1] + _shift_rows(sums[-1], prev[level], s, row))
        prev = sums[:-1]
        total = jnp.where(g >= 3, sums[4], jnp.where(g >= 2, sums[3], jnp.where(g >= 1, sums[2], sums[1])))
        cnt = jnp.minimum(row + (r0 + 1), window).astype(F32)
        d = total / cnt - p
        y_ref[r0:r0 + MIX_ROWS, :] = (_dot(d.astype(BF16), pw) * scale).astype(BF16)
    st_ref[...] = ps[-1][MIX_ROWS - POOL_TAIL:, :]


def _pool_p(h, w_in, pool_w, pool_scale, w_br, l):
    col0 = 3 * W // CB
    bcols = D // (BATCH * N_POOL_GROUPS)
    return pl.pallas_call(
        _pool_p_kernel,
        out_shape=(jax.ShapeDtypeStruct((NP, W), BF16),
                   jax.ShapeDtypeStruct((BATCH, POOL_TAIL, W), F32),
                   jax.ShapeDtypeStruct((3, W, D), BF16)),
        grid=(BATCH, N_POOL_GROUPS),
        in_specs=[
            pl.BlockSpec((SEQ, D), lambda b, g: (b, 0)),
            pl.BlockSpec((None, D, CB), lambda b, g: (l, 0, col0 + g)),
            pl.BlockSpec((None, None, POOL_GROUP, POOL_GROUP), lambda b, g: (l, g, 0, 0)),
            pl.BlockSpec((None, 1, CB), lambda b, g: (l, 0, g)),
            pl.BlockSpec((None, 3, W, bcols), lambda b, g: (l, 0, 0, b * N_POOL_GROUPS + g)),
        ],
        out_specs=(pl.BlockSpec((SEQ, CB), lambda b, g: (b, g)),
                   pl.BlockSpec((None, POOL_TAIL, CB), lambda b, g: (b, 0, g)),
                   pl.BlockSpec((3, W, bcols), lambda b, g: (0, 0, b * N_POOL_GROUPS + g))),
        compiler_params=_params(("arbitrary", "arbitrary")),
        name="pool_prompt",
    )(h, w_in, pool_w, pool_scale, w_br)


def _gmlp_p_kernel(h_ref, wv_ref, wu_ref, gv_ref, ws_ref, bs_ref, wcast_ref, y_ref, wcast_out, vn_s):
    wcast_out[...] = wcast_ref[...].astype(BF16)
    j = pl.program_id(1)

    @pl.when(j == 0)
    def _():
        wv = wv_ref[...].astype(BF16)
        for r0 in range(0, SEQ, TM):
            v = _dot(h_ref[r0:r0 + TM, :], wv)
            vn = (v * _rms(v)) * gv_ref[...]
            for hd in range(HEADS):
                vn_s[hd, r0:r0 + TM, :] = vn[:, hd * CHUNK:(hd + 1) * CHUNK].astype(BF16)

    u = _dot(h_ref[...], wu_ref[...].astype(BF16))
    heads_per_block = CB // CHUNK
    for hh in range(heads_per_block):
        head = heads_per_block * j + hh
        wsh = ws_ref[head]
        bsh = bs_ref[head]
        for n in range(SEQ // CHUNK):
            rows = slice(n * CHUNK, (n + 1) * CHUNK)
            cols = slice(hh * CHUNK, (hh + 1) * CHUNK)
            s = _dot(wsh, vn_s[head, pl.ds(n * CHUNK, CHUNK), :]) + bsh
            y_ref[rows, cols] = (u[rows, cols] * s).astype(BF16)


def _gmlp_p(h, w_in, g_v, ws_tril, bs_tile, w_o, l):
    ucol0 = 4 * W // CB
    nj = W // CB
    ocols = D // (BATCH * nj)
    return pl.pallas_call(
        _gmlp_p_kernel,
        out_shape=(jax.ShapeDtypeStruct((NP, W), BF16), jax.ShapeDtypeStruct((D, D), BF16)),
        grid=(BATCH, nj),
        in_specs=[
            pl.BlockSpec((SEQ, D), lambda b, j: (b, 0)),
            pl.BlockSpec((None, D, W), lambda b, j: (l, 0, 5), pipeline_mode=pl.Buffered(1)),
            pl.BlockSpec((None, D, CB), lambda b, j: (l, 0, ucol0 + j)),
            pl.BlockSpec((None, 1, W), lambda b, j: (l, 0, 0)),
            pl.BlockSpec((None, HEADS, CHUNK, CHUNK), lambda b, j: (l, 0, 0, 0)),
            pl.BlockSpec((None, HEADS, CHUNK, CHUNK), lambda b, j: (l, 0, 0, 0)),
            pl.BlockSpec((None, D, ocols), lambda b, j: (l, 0, b * nj + j)),
        ],
        out_specs=(pl.BlockSpec((SEQ, CB), lambda b, j: (b, j)),
                   pl.BlockSpec((D, ocols), lambda b, j: (0, b * nj + j))),
        scratch_shapes=[pltpu.VMEM((HEADS, SEQ, CHUNK), BF16)],
        compiler_params=_params(("arbitrary", "arbitrary")),
        name="gmlp_prompt",
    )(h, w_in, w_in, g_v, ws_tril, bs_tile, w_o)


def _mix_s_kernel(h_ref, wb_ref, wc_ref, wh_ref, wp_ref, cw_ref, pc_ref, pp_ref,
                  pw_ref, ps_ref, ya_ref, yb_ref, cst_ref, pst_ref):
    g = pl.program_id(0)
    h = h_ref[...]
    nb = DEC_BATCH
    z = _dot(h, wc_ref[...].astype(BF16)) * _dot(h, wh_ref[...].astype(BF16))
    bg = _dot(h, wb_ref[...].astype(BF16))
    zs = [pc_ref[0], pc_ref[1]] + [z[t * nb:(t + 1) * nb] for t in range(DEC_SEQ)]
    cw = cw_ref[...]
    for t in range(DEC_SEQ):
        y = cw[0:1] * zs[t] + cw[1:2] * zs[t + 1] + cw[2:3] * zs[t + 2]
        ya_ref[t * nb:(t + 1) * nb, :] = (bg[t * nb:(t + 1) * nb] * y).astype(BF16)
    cst_ref[0] = zs[DEC_SEQ]
    cst_ref[1] = zs[DEC_SEQ + 1]
    p = _dot(h, wp_ref[...].astype(BF16))
    pp = [pp_ref[k] for k in range(POOL_STATE)] + [p[t * nb:(t + 1) * nb] for t in range(DEC_SEQ)]
    window = jnp.left_shift(2, g).astype(F32)
    ds = []
    for t in range(DEC_SEQ):
        e = POOL_STATE + t
        s = pp[e] + pp[e - 1]
        s4 = s + (pp[e - 2] + pp[e - 3])
        s8 = s4 + ((pp[e - 4] + pp[e - 5]) + (pp[e - 6] + pp[e - 7]))
        s16 = s8 + (((pp[e - 8] + pp[e - 9]) + (pp[e - 10] + pp[e - 11]))
                    + ((pp[e - 12] + pp[e - 13]) + (pp[e - 14] + pp[e - 15])))
        s = jnp.where(g >= 1, s4, s)
        s = jnp.where(g >= 2, s8, s)
        s = jnp.where(g >= 3, s16, s)
        ds.append(s / window - pp[e])
    d = jnp.concatenate(ds, axis=0)
    yb_ref[...] = (_dot(d.astype(BF16), pw_ref[...]) * ps_ref[...]).astype(BF16)
    for k in range(POOL_STATE):
        pst_ref[k] = pp[DEC_SEQ + k]


def _mix_s(h, w_in, conv_w, pool_w, pool_scale, conv_t, pool_t, l):
    nc = W // CB
    srow = NP // NS
    return pl.pallas_call(
        _mix_s_kernel,
        out_shape=(jax.ShapeDtypeStruct((NS, W), BF16),
                   jax.ShapeDtypeStruct((NS, W), BF16),
                   jax.ShapeDtypeStruct((CONV_K - 1, DEC_BATCH, W), F32),
                   jax.ShapeDtypeStruct((POOL_STATE, DEC_BATCH, W), F32)),
        grid=(nc,),
        in_specs=[
            pl.BlockSpec((NS, D), lambda c: (srow, 0)),
            pl.BlockSpec((None, D, CB), lambda c: (l, 0, c)),
            pl.BlockSpec((None, D, CB), lambda c: (l, 0, nc + c)),
            pl.BlockSpec((None, D, CB), lambda c: (l, 0, 2 * nc + c)),
            pl.BlockSpec((None, D, CB), lambda c: (l, 0, 3 * nc + c)),
            pl.BlockSpec((None, CONV_K, CB), lambda c: (l, 0, c)),
            pl.BlockSpec((None, CONV_K - 1, DEC_BATCH, CB), lambda c: (l, 0, 0, c)),
            pl.BlockSpec((None, POOL_STATE, DEC_BATCH, CB), lambda c: (l, 0, 0, c)),
            pl.BlockSpec((None, None, POOL_GROUP, POOL_GROUP), lambda c: (l, c, 0, 0)),
            pl.BlockSpec((None, 1, CB), lambda c: (l, 0, c)),
        ],
        out_specs=(pl.BlockSpec((NS, CB), lambda c: (0, c)),
                   pl.BlockSpec((NS, CB), lambda c: (0, c)),
                   pl.BlockSpec((CONV_K - 1, DEC_BATCH, CB), lambda c: (0, 0, c)),
                   pl.BlockSpec((POOL_STATE, DEC_BATCH, CB), lambda c: (0, 0, c))),
        compiler_params=_params(("arbitrary",)),
        name="mix_sample",
    )(h, w_in, w_in, w_in, w_in, conv_w, conv_t, pool_t, pool_w, pool_scale)


def _gmlp_s_kernel(h_ref, wu_ref, wv_ref, gv_ref, wvec_ref, bvec_ref, yc_ref, vn_ref):
    h = h_ref[...]
    nb = DEC_BATCH
    v = _dot(h, wv_ref[...].astype(BF16))
    vn = (v * _rms(v)) * gv_ref[...]
    vn_ref[...] = vn
    u = _dot(h, wu_ref[...].astype(BF16))
    for t in range(DEC_SEQ):
        s = bvec_ref[t:t + 1, :]
        for sp in range(t + 1):
            k = t * DEC_SEQ + sp
            s = s + wvec_ref[k:k + 1, :] * vn[sp * nb:(sp + 1) * nb]
        yc_ref[t * nb:(t + 1) * nb, :] = (u[t * nb:(t + 1) * nb] * s).astype(BF16)


def _gmlp_s(h, w_in, g_v, wvec, bvec, l):
    srow = NP // NS
    return pl.pallas_call(
        _gmlp_s_kernel,
        out_shape=(jax.ShapeDtypeStruct((NS, W), BF16),
                   jax.ShapeDtypeStruct((NS, W), F32)),
        grid=(1,),
        in_specs=[
            pl.BlockSpec((NS, D), lambda i: (srow, 0)),
            pl.BlockSpec((None, D, W), lambda i: (l, 0, 4)),
            pl.BlockSpec((None, D, W), lambda i: (l, 0, 5)),
            pl.BlockSpec((None, 1, W), lambda i: (l, 0, 0)),
            pl.BlockSpec((None, DEC_SEQ * DEC_SEQ, W), lambda i: (l, 0, 0)),
            pl.BlockSpec((None, DEC_SEQ, W), lambda i: (l, 0, 0)),
        ],
        out_specs=(pl.BlockSpec((NS, W), lambda i: (0, 0)),
                   pl.BlockSpec((NS, W), lambda i: (0, 0))),
        compiler_params=_params(("arbitrary",)),
        name="gmlp_sample",
    )(h, w_in, w_in, g_v, wvec, bvec)


def _merge_kernel(h_ref, yap, ybp, ycp, yas, ybs, ycs, wg0, wg1, wg2, wbr_ref, o_ref):
    is_sample = pl.program_id(0) >= NP // TM
    h = h_ref[...]
    acc = None
    for n, (yp, ys, wg) in enumerate(((yap, yas, wg0), (ybp, ybs, wg1), (ycp, ycs, wg2))):
        y = jnp.where(is_sample, ys[...], yp[...])
        term = _sigmoid(_dot(h, wg[...])) * _dot(y, wbr_ref[n])
        acc = term if acc is None else acc + term
    o_ref[...] = acc.astype(BF16)


def _merge(h, y_prompt, y_sample, w_gates, w_br):
    db = 512
    g0 = 0
    gstep = D // db
    return pl.pallas_call(
        _merge_kernel,
        out_shape=jax.ShapeDtypeStruct((NT, D), BF16),
        grid=(N_TILES, D // db),
        in_specs=[
            pl.BlockSpec((TM, D), lambda i, d: (i, 0)),
            pl.BlockSpec((TM, W), lambda i, d: (jnp.minimum(i, NP // TM - 1), 0)),
            pl.BlockSpec((TM, W), lambda i, d: (jnp.minimum(i, NP // TM - 1), 0)),
            pl.BlockSpec((TM, W), lambda i, d: (jnp.minimum(i, NP // TM - 1), 0)),
            pl.BlockSpec((NS, W), lambda i, d: (0, 0)),
            pl.BlockSpec((NS, W), lambda i, d: (0, 0)),
            pl.BlockSpec((NS, W), lambda i, d: (0, 0)),
            pl.BlockSpec((D, db), lambda i, d: (0, g0 + d)),
            pl.BlockSpec((D, db), lambda i, d: (0, g0 + gstep + d)),
            pl.BlockSpec((D, db), lambda i, d: (0, g0 + 2 * gstep + d)),
            pl.BlockSpec((3, W, db), lambda i, d: (0, 0, d)),
        ],
        out_specs=pl.BlockSpec((TM, db), lambda i, d: (i, d)),
        compiler_params=_params(("arbitrary", "arbitrary")),
        name="merge",
    )(h, *y_prompt, *y_sample, w_gates, w_gates, w_gates, w_br)


def _route(logits):
    lane = lax.broadcasted_iota(jnp.int32, logits.shape, 1).astype(F32)
    neg = -jnp.inf
    big = float(LANES)
    is_grp = lane < N_GROUPS_MOE
    gl = jnp.where(is_grp, logits, neg)
    gmax = jnp.max(gl, axis=-1, keepdims=True)
    gsel = jnp.min(jnp.where(gl == gmax, lane, big), axis=-1, keepdims=True)
    gp = 1.0 / jnp.sum(jnp.where(is_grp, jnp.exp(logits - gmax), 0.0), axis=-1, keepdims=True)
    lo = N_GROUPS_MOE + gsel * EXP_PER_GROUP
    in_grp = (lane >= lo) & (lane < lo + EXP_PER_GROUP)
    el = jnp.where(in_grp, logits, neg)
    m1 = jnp.max(el, axis=-1, keepdims=True)
    i1 = jnp.min(jnp.where(el == m1, lane, big), axis=-1, keepdims=True)
    el2 = jnp.where(lane == i1, neg, el)
    m2 = jnp.max(el2, axis=-1, keepdims=True)
    i2 = jnp.min(jnp.where(el2 == m2, lane, big), axis=-1, keepdims=True)
    e = jnp.exp(m2 - m1)
    w1 = gp / (1.0 + e)
    w2 = gp * (e / (1.0 + e))
    rw = jnp.where(lane == 0.0, w1, jnp.where(lane == 1.0, w2, 0.0))
    return rw, lane, i1, i2


def _wo_kernel(m_ref, wo_ref, xp_ref, xs_ref, g_ref, gtp, gts, scp, scs, shp, shs, wrc_ref, wrh_ref, br_ref,
               x1_ref, h2_ref, rw_ref, re_ref, cnt_ref, carry):
    i = pl.program_id(0)
    x = jnp.where(i >= NP // TW, xs_ref[...], xp_ref[...])
    gt = _mod_rows(i, TW, gtp, gts)
    x1 = x + gt * _dot(m_ref[...], wo_ref[...])
    x1_ref[...] = x1
    sc = _mod_rows(i, TW, scp, scs)
    sh = _mod_rows(i, TW, shp, shs)
    h2 = ((x1 * _rms(x1)) * g_ref[...]) * (1.0 + sc) + sh
    h2_ref[...] = h2.astype(BF16).reshape(TW, D // LANES, LANES)
    hi = h2.astype(BF16)
    lo = (h2 - hi.astype(F32)).astype(BF16)
    both = _dot(hi, wrc_ref[...])
    logits = (both[:, :LANES] + (both[:, LANES:] + _dot(lo, wrh_ref[...]))) + br_ref[...]
    rw, lane, i1, i2 = _route(logits)
    rw_ref[...] = rw

    @pl.when(i == 0)
    def _():
        carry[...] = jnp.zeros_like(carry)

    o1 = (lane == i1).astype(F32)
    o2 = (lane == i2).astype(F32)
    both_hot = o1 + o2
    r = lax.broadcasted_iota(jnp.int32, (TW, TW), 0)
    c = lax.broadcasted_iota(jnp.int32, (TW, TW), 1)
    before = (r > c).astype(BF16)
    seen = _dot(before, both_hot.astype(BF16)) + carry[...]
    rank1 = jnp.sum(seen * o1, axis=-1, keepdims=True)
    rank2 = jnp.sum(seen * o2, axis=-1, keepdims=True)
    total = carry[...] + jnp.sum(both_hot, axis=0, keepdims=True)
    carry[...] = total
    cnt_ref[...] = jnp.broadcast_to(total, cnt_ref.shape).astype(jnp.int32)
    re = jnp.where(lane == 0.0, i1 - N_GROUPS_MOE,
                   jnp.where(lane == 1.0, i2 - N_GROUPS_MOE,
                             jnp.where(lane == 2.0, rank1, jnp.where(lane == 3.0, rank2, 0.0))))
    re_ref[...] = re.astype(jnp.int32)


def _wo(merged, w_o, x_prompt_rows, x_sample_rows, sample_block0, g2, mod, wr_cat, wr_hi, b_r, l):
    row = lambda i: (i, 0)
    n_ptiles = NP // TW
    return pl.pallas_call(
        _wo_kernel,
        out_shape=(jax.ShapeDtypeStruct((NT, D), F32),
                   jax.ShapeDtypeStruct((NT, D // LANES, LANES), BF16),
                   jax.ShapeDtypeStruct((NT, LANES), F32),
                   jax.ShapeDtypeStruct((NT, LANES), jnp.int32),
                   jax.ShapeDtypeStruct((SUBLANES, LANES), jnp.int32)),
        grid=(NT // TW,),
        in_specs=[
            pl.BlockSpec((TW, D), row),
            pl.BlockSpec((D, D), lambda i: (0, 0)),
            pl.BlockSpec((TW, D), lambda i: (jnp.minimum(i, n_ptiles - 1), 0)),
            pl.BlockSpec((TW, D), lambda i: (sample_block0 + jnp.maximum(i - n_ptiles, 0), 0)),
            pl.BlockSpec((None, 1, D), lambda i: (l, 0, 0)),
        ] + _mod_specs(l, 2) + _mod_specs(l, 4) + _mod_specs(l, 3) + [
            pl.BlockSpec((None, D, 2 * LANES), lambda i: (l, 0, 0)),
            pl.BlockSpec((None, D, LANES), lambda i: (l, 0, 0)),
            pl.BlockSpec((None, 1, LANES), lambda i: (l, 0, 0)),
        ],
        out_specs=(pl.BlockSpec((TW, D), row), pl.BlockSpec((TW, D // LANES, LANES), lambda i: (i, 0, 0)),
                   pl.BlockSpec((TW, LANES), row), pl.BlockSpec((TW, LANES), row),
                   pl.BlockSpec((SUBLANES, LANES), lambda i: (0, 0))),
        scratch_shapes=[pltpu.VMEM((1, LANES), F32)],
        compiler_params=_params(("arbitrary",)),
        name="wo_router",
    )(merged, w_o, x_prompt_rows, x_sample_rows, g2, mod, mod, mod, mod, mod, mod, wr_cat, wr_hi, b_r)


def _row_gather_copy(src_hbm, idx, dst, sem):
    return pltpu.make_async_copy(src_hbm.at[pl.ds(idx, 1)], dst, sem)


def _moe_kernel(l, te_ref, tf_ref, nu_ref, pad_ref, dest_ref, h2_hbm, wg_hbm, wu_hbm, wd_hbm, o_ref,
                src_ref, xbuf, sem, wg_s, wu_s, wd_s, stg_g, stg_u, stg_d, wsem):
    t = pl.program_id(0)
    n_used = nu_ref[0]

    def weight_copies(e):
        return (pltpu.make_async_copy(wg_hbm.at[l, e], stg_g, wsem.at[0]),
                pltpu.make_async_copy(wu_hbm.at[l, e], stg_u, wsem.at[1]),
                pltpu.make_async_copy(wd_hbm.at[l, e], stg_d, wsem.at[2]))

    def issue(tile, slot):
        base = tile * TE

        def body(r, carry):
            _row_gather_copy(h2_hbm, src_ref[base + r], xbuf.at[slot, pl.ds(r, 1)], sem.at[slot]).start()
            return carry

        lax.fori_loop(0, TE, body, 0, unroll=8)

    @pl.when(t == 0)
    def _():
        for c in weight_copies(te_ref[0]):
            c.start()

        for e in range(N_EXPERTS):
            lo = pad_ref[e]
            hi = pad_ref[N_EXPERTS + e]

            def clear(i, carry, lo=lo, hi=hi):
                src_ref[jnp.where(lo + i < hi, lo + i, P_ROWS)] = 0
                return carry

            lax.fori_loop(0, TE, clear, 0, unroll=16)

        def invert(n, carry):
            both = dest_ref[n]
            src_ref[both & 0xFFFF] = n
            src_ref[lax.shift_right_logical(both, 16)] = n
            return carry

        lax.fori_loop(0, NT, invert, 0, unroll=16)
        for k in range(MOE_SLOTS - 1):
            issue(jnp.minimum(k, n_used - 1), k)

    @pl.when((t < n_used) & (tf_ref[t] > 0))
    def _():
        for c in weight_copies(te_ref[t]):
            c.wait()
        wg_s[...] = stg_g[...].astype(BF16)
        wu_s[...] = stg_u[...].astype(BF16)
        wd_s[...] = stg_d[...].astype(BF16)
        nxt = tf_ref[t] - 1

        @pl.when(nxt < N_EXPERTS)
        def _():
            for c in weight_copies(nxt):
                c.start()

    def wait_tile(slot):
        pltpu.make_async_copy(h2_hbm.at[pl.ds(0, TE)], xbuf.at[slot], sem.at[slot]).wait()

    @pl.when(t < n_used)
    def _():
        slot = t % MOE_SLOTS
        wait_tile(slot)
        x = xbuf[slot].reshape(TE, D)
        ahead = (t + MOE_SLOTS - 1) % MOE_SLOTS
        base = jnp.minimum(t + MOE_SLOTS - 1, n_used - 1) * TE
        for r in range(TE):
            _row_gather_copy(h2_hbm, src_ref[base + r], xbuf.at[ahead, pl.ds(r, 1)], sem.at[ahead]).start()
        a = _dot(x, wg_s[...])
        b = _dot(x, wu_s[...])
        hid = (a * _sigmoid(a)) * b
        o_ref[...] = _dot(hid.astype(BF16), wd_s[...])

    @pl.when(t == n_used - 1)
    def _():
        for k in range(1, MOE_SLOTS):
            wait_tile((t + k) % MOE_SLOTS)

    @pl.when(t >= n_used)
    def _():
        o_ref[...] = jnp.zeros_like(o_ref)


def _moe(h2, tile_expert, tile_first, n_used, pads, dest, w_gate, w_up, w_down, l):
    any_spec = pl.BlockSpec(memory_space=pl.ANY)
    grid_spec = pltpu.PrefetchScalarGridSpec(
        num_scalar_prefetch=5,
        grid=(MAX_TILES,),
        in_specs=[any_spec, any_spec, any_spec, any_spec],
        out_specs=pl.BlockSpec((TE, D), lambda t, *_: (t, 0)),
        scratch_shapes=[
            pltpu.SMEM((P_ROWS + SUBLANES,), jnp.int32),
            pltpu.VMEM((MOE_SLOTS, TE, D // LANES, LANES), BF16),
            pltpu.SemaphoreType.DMA((MOE_SLOTS,)),
            pltpu.VMEM((D, D_FF), BF16),
            pltpu.VMEM((D, D_FF), BF16),
            pltpu.VMEM((D_FF, D), BF16),
            pltpu.VMEM((D, D_FF), F32),
            pltpu.VMEM((D, D_FF), F32),
            pltpu.VMEM((D_FF, D), F32),
            pltpu.SemaphoreType.DMA((3,)),
        ],
    )
    return pl.pallas_call(
        functools.partial(_moe_kernel, l),
        out_shape=jax.ShapeDtypeStruct((P_ROWS, D), F32),
        grid_spec=grid_spec,
        compiler_params=_params(("arbitrary",)),
        name="moe_experts",
    )(tile_expert, tile_first, n_used, pads, dest, h2, w_gate, w_up, w_down)


def _dispatch(route_e, counts_lanes):
    counts = counts_lanes[0, N_GROUPS_MOE:N_GROUPS_MOE + N_EXPERTS]
    tiles = (counts + TE - 1) // TE
    eids = jnp.arange(N_EXPERTS, dtype=jnp.int32)
    tile_end = jnp.sum(jnp.where(eids[None, :] <= eids[:, None], tiles[None, :], 0), axis=1)
    tile_start = tile_end - tiles
    hot = route_e[:, :TOP_K, None] == eids[None, None, :]
    row0 = jnp.sum(jnp.where(hot, tile_start[None, None, :] * TE, 0), axis=-1)
    slots = (row0 + route_e[:, TOP_K:2 * TOP_K]).astype(jnp.int32)
    dest = slots.reshape(-1)
    dest_packed = slots[:, 0] | (slots[:, 1] << 16)
    tids = jnp.arange(MAX_TILES, dtype=jnp.int32)
    tile_expert = jnp.minimum(jnp.sum((tile_end[None, :] <= tids[:, None]).astype(jnp.int32), axis=1),
                              N_EXPERTS - 1)
    first_tile = jnp.sum(jnp.where(tile_expert[:, None] == eids[None, :], tile_start[None, :], 0), axis=1)
    n_used = tile_end[-1:].astype(jnp.int32)
    after = jnp.sum(jnp.where(tile_expert[:, None] == eids[None, :], tile_end[None, :], 0), axis=1)
    next_e = jnp.sum(jnp.where(tids[None, :] == jnp.minimum(after, MAX_TILES - 1)[:, None],
                               tile_expert[None, :], 0), axis=1)
    code = 1 + jnp.where(after < n_used[0], next_e, N_EXPERTS)
    tile_first = jnp.where(tids == first_tile, code, 0).astype(jnp.int32)
    pads = jnp.concatenate([tile_start * TE + counts, tile_end * TE]).astype(jnp.int32)
    return dest, dest_packed, tile_expert, tile_first, n_used, pads


def _combine_kernel(final, tile0, dest_ref, ys_hbm, x_ref, rw_ref, g_ref, gtp, gts, *rest):
    if final:
        y_ref, ybuf, sem = rest
    else:
        scp, scs, shp, shs, x2_ref, h_ref, ybuf, sem = rest
    step = pl.program_id(0)
    nsteps = pl.num_programs(0)
    i = step + tile0

    def issue(tile, slot, r):
        for k in range(TOP_K):
            _row_gather_copy(ys_hbm, dest_ref[tile * (TK * TOP_K) + TOP_K * r + k],
                             ybuf.at[slot, k, pl.ds(r, 1)], sem.at[slot]).start()

    def wait_tile(slot):
        for k in range(TOP_K):
            pltpu.make_async_copy(ys_hbm.at[pl.ds(0, TK)], ybuf.at[slot, k], sem.at[slot]).wait()

    last = tile0 + nsteps - 1

    @pl.when(step == 0)
    def _():
        for k in range(GATHER_SLOTS - 1):
            def body(r, carry, k=k):
                issue(jnp.minimum(i + k, last), k, r)
                return carry

            lax.fori_loop(0, TK, body, 0, unroll=8)

    slot = step % GATHER_SLOTS
    wait_tile(slot)
    ahead = (step + GATHER_SLOTS - 1) % GATHER_SLOTS
    nxt = jnp.minimum(i + GATHER_SLOTS - 1, last)
    for r in range(TK):
        issue(nxt, ahead, r)
    gt = _mod_rows(i, TK, gtp, gts)
    rw = rw_ref[...]
    moe = rw[:, 0:1] * ybuf[slot, 0] + rw[:, 1:2] * ybuf[slot, 1]
    x2 = x_ref[...] + gt * moe
    xn = (x2 * _rms(x2)) * g_ref[...]
    if final:
        y_ref[...] = xn
    else:
        x2_ref[...] = x2
        sc = _mod_rows(i, TK, scp, scs)
        sh = _mod_rows(i, TK, shp, shs)
        h_ref[...] = (xn * (1.0 + sc) + sh).astype(BF16)

    @pl.when(step == nsteps - 1)
    def _():
        for k in range(1, GATHER_SLOTS):
            wait_tile((step + k) % GATHER_SLOTS)


def _combine(dest, ys, x1, route_w, g, mod, l, final, tile0=0, ntiles=NT // TK):
    row_in = lambda s, d: (s + tile0, 0)
    row_out = lambda s, d: (s, 0)
    nmod = lambda q, ll: [
        pl.BlockSpec((None, SUBLANES, D), lambda s, d: (ll, MOD_PROMPT_BLOCK, q)),
        pl.BlockSpec((None, DEC_BATCH, D), lambda s, d: (ll, 0, q)),
    ]
    in_specs = [pl.BlockSpec(memory_space=pl.ANY), pl.BlockSpec((TK, D), row_in),
                pl.BlockSpec((TK, LANES), row_in),
                pl.BlockSpec((1, D), lambda s, d: (0, 0))] + nmod(5, l)
    args = [ys, x1, route_w, g, mod, mod]
    rows = ntiles * TK
    if final:
        out_shape = jax.ShapeDtypeStruct((rows, D), F32)
        out_specs = pl.BlockSpec((TK, D), row_out)
    else:
        in_specs += nmod(1, l + 1) + nmod(0, l + 1)
        args += [mod, mod, mod, mod]
        out_shape = (jax.ShapeDtypeStruct((rows, D), F32), jax.ShapeDtypeStruct((rows, D), BF16))
        out_specs = (pl.BlockSpec((TK, D), row_out), pl.BlockSpec((TK, D), row_out))
    grid_spec = pltpu.PrefetchScalarGridSpec(
        num_scalar_prefetch=1,
        grid=(ntiles,),
        in_specs=in_specs,
        out_specs=out_specs,
        scratch_shapes=[pltpu.VMEM((GATHER_SLOTS, TOP_K, TK, D), F32),
                        pltpu.SemaphoreType.DMA((GATHER_SLOTS,))],
    )
    return pl.pallas_call(
        functools.partial(_combine_kernel, final, tile0),
        out_shape=out_shape,
        grid_spec=grid_spec,
        compiler_params=_params(("arbitrary",)),
        name=("combine_final_%d" % tile0) if final else "combine",
    )(dest, *args)


def kernel(x_prompt, x_sample, c_prompt, c_sample, state_conv, state_pool, w_ada, b_ada, g_norm1, w_in,
           conv_w, pool_w, pool_scale, g_v, w_s, b_s, w_br, w_o, g_norm2, w_rg, b_rg, w_re, b_re,
           w_gate, w_up, w_down, g_final):
    xp_rows = x_prompt.reshape(NP, D)
    xs_rows = x_sample.transpose(1, 0, 2).reshape(NS, D)
    sample_block0 = 0
    c_all = jnp.concatenate([c_sample, c_prompt,
                             jnp.zeros((MOD_ROWS - DEC_BATCH - BATCH, D), F32)], axis=0).astype(BF16)
    mod = _ada(c_all, w_ada, b_ada)

    pool_w_b = pool_w.astype(BF16)
    tril = jnp.tril(jnp.ones((CHUNK, CHUNK), dtype=bool))
    ws_tril = jnp.where(tril[None, None], w_s, 0.0).astype(BF16)
    bs_tile = jnp.broadcast_to(b_s[:, :, :, None], (DEPTH, HEADS, CHUNK, CHUNK))
    small = jnp.tril(jnp.ones((DEC_SEQ, DEC_SEQ), dtype=bool))
    ws_small = jnp.where(small[None, None], w_s[:, :, :DEC_SEQ, :DEC_SEQ], 0.0)
    wvec = jnp.repeat(ws_small.transpose(0, 2, 3, 1).reshape(DEPTH, DEC_SEQ * DEC_SEQ, HEADS), CHUNK, axis=-1)
    bvec = jnp.repeat(b_s[:, :, :DEC_SEQ].transpose(0, 2, 1), CHUNK, axis=-1)
    w_r = jnp.concatenate([w_rg, w_re, jnp.zeros((DEPTH, D, LANES - N_GROUPS_MOE - N_EXPERTS), F32)], axis=-1)
    wr_hi = w_r.astype(BF16)
    wr_lo = (w_r - wr_hi.astype(F32)).astype(BF16)
    wr_cat = jnp.concatenate([wr_hi, wr_lo], axis=-1)
    b_r = jnp.concatenate([b_rg, b_re, jnp.zeros((DEPTH, LANES - N_GROUPS_MOE - N_EXPERTS), F32)],
                          axis=-1).reshape(DEPTH, 1, LANES)
    conv_t = state_conv.transpose(0, 2, 1, 3)
    pool_t = state_pool.transpose(0, 2, 1, 3)
    g1 = g_norm1.reshape(DEPTH, 1, D)
    g2 = g_norm2.reshape(DEPTH, 1, D)
    gv = g_v.reshape(DEPTH, 1, W)
    pscale = pool_scale.reshape(DEPTH, 1, W)

    h = _norm(xp_rows, xs_rows, g1, mod, 0)
    conv_p, pool_p, conv_s, pool_s, v_s = [], [], [], [], []
    y_prompt = y_sample = None
    for l in range(DEPTH):
        ya, cst_p, w_gates_b = _conv_p(h, w_in, conv_w, l)
        yb, pst_p, w_br_b = _pool_p(h, w_in, pool_w_b, pscale, w_br, l)
        yc, w_o_b = _gmlp_p(h, w_in, gv, ws_tril, bs_tile, w_o, l)
        ya_s, yb_s, cst_s, pst_s = _mix_s(h, w_in, conv_w, pool_w_b, pscale, conv_t, pool_t, l)
        yc_s, vn_s = _gmlp_s(h, w_in, gv, wvec, bvec, l)
        merged = _merge(h, (ya, yb, yc), (ya_s, yb_s, yc_s), w_gates_b, w_br_b)
        x1, h2, route_w, route_e, counts = _wo(merged, w_o_b, xp_rows, xs_rows, sample_block0, g2, mod,
                                               wr_cat, wr_hi, b_r, l)
        dest, dest_packed, tile_expert, tile_first, n_used, pads = _dispatch(route_e, counts)
        ys = _moe(h2, tile_expert, tile_first, n_used, pads, dest_packed, w_gate, w_up, w_down, l)
        if l + 1 < DEPTH:
            x, h = _combine(dest, ys, x1, route_w, g1[l + 1], mod, l, final=False)
            xp_rows = xs_rows = x
            sample_block0 = NP // TW
        else:
            gf = g_final.reshape(1, D)
            y_prompt = _combine(dest, ys, x1, route_w, gf, mod, l, final=True, tile0=0, ntiles=NP // TK)
            y_sample = _combine(dest, ys, x1, route_w, gf, mod, l, final=True, tile0=NP // TK,
                                ntiles=NS // TK)
        conv_p.append(cst_p[:, CONV_TAIL - (CONV_K - 1):, :])
        pool_p.append(pst_p[:, POOL_TAIL - POOL_STATE:, :])
        conv_s.append(cst_s.transpose(1, 0, 2))
        pool_s.append(pst_s.transpose(1, 0, 2))
        v_s.append(vn_s.reshape(DEC_SEQ, DEC_BATCH, W).transpose(1, 0, 2))

    y_prompt = y_prompt.reshape(BATCH, SEQ, D)
    y_sample = y_sample.reshape(DEC_SEQ, DEC_BATCH, D).transpose(1, 0, 2)
    return (y_prompt, y_sample, jnp.stack(conv_p), jnp.stack(pool_p), jnp.stack(conv_s),
            jnp.stack(pool_s), jnp.stack(v_s))
```
